```python
import jax, jax.numpy as jnp
from jax import lax
import numpy as np

D_MODEL = 1024
BATCH = 2
SEQ = 8192
DEPTH = 2
DEC_BATCH = 32
DEC_SEQ = 4
PAST_LEN = 16384
PAGE_SIZE = 128

D_MIX = D_MODEL
H_A = 4
W_A = (3 * D_MIX) // 8
DV_A = W_A // H_A
DK_A = DV_A // 2
CHUNK_A = 128
POOL_WINDOWS = (2, 4, 8, 16)
G_B = 4
W_B = D_MIX // 4
CG_B = W_B // G_B
POOL_HIST = 15
DH_C = 64
W_C = D_MIX - W_A - W_B
H_C = W_C // DH_C
Q_BLOCK = 128
D_FF = 4 * D_MODEL
EPS = 1e-6
FGATE_BIAS_C = 3.0

IN_SIZES = (H_A * DK_A, H_A * DK_A, W_A, H_A, H_A, W_A, W_B, W_C, W_C, W_C, H_C)
N_IN = H_A * DK_A * 2 + W_A * 2 + H_A * 2 + W_B + W_C * 3 + H_C

kernel_name = "hymba_mlstm_pool_fox_decode_step"


def rmsnorm(x, g):
    xf = x.astype(jnp.float32)
    return xf * lax.rsqrt(jnp.mean(xf * xf, axis=-1, keepdims=True) + EPS) * g.astype(jnp.float32)


def split_cols(p):
    out = []
    o = 0
    for n in IN_SIZES:
        out.append(p[..., o:o + n])
        o += n
    return out


def col_offset(i):
    return sum(IN_SIZES[:i])


def mlstm(q, k, v, ig, fg, C0, n0, m0):
    B, T, H, DK = q.shape
    L = CHUNK_A if T % CHUNK_A == 0 else T
    NC = T // L

    def chunks(a):
        a = a.reshape((B, NC, L) + a.shape[2:])
        return jnp.swapaxes(jnp.moveaxis(a, 1, 0), 2, 3)

    causal = jnp.tril(jnp.ones((L, L), dtype=bool))

    def step(carry, inp):
        C, n, m = carry
        qc, kc, vc, ic, fc = inp
        bc = jnp.cumsum(jax.nn.log_sigmoid(fc), axis=-1)
        D = jnp.where(causal, bc[..., :, None] - bc[..., None, :] + ic[..., None, :], -jnp.inf)
        inter = bc + m[..., None]
        mt = jnp.maximum(inter, jnp.max(D, axis=-1))
        a = jnp.exp(inter - mt)
        S = jnp.einsum("bhtd,bhsd->bhts", qc, kc) * jnp.exp(D - mt[..., None])
        num = a[..., None] * jnp.einsum("bhvd,bhtd->bhtv", C, qc) + jnp.einsum("bhts,bhsv->bhtv", S, vc)
        den = a * jnp.einsum("bhd,bhtd->bht", n, qc) + jnp.sum(S, axis=-1)
        h = num / jnp.maximum(jnp.abs(den), jnp.exp(-mt))[..., None]
        m_new = mt[..., -1]
        g = jnp.exp(D[..., -1, :] - m_new[..., None])
        a_last = jnp.exp(inter[..., -1] - m_new)
        C_new = a_last[..., None, None] * C + jnp.einsum("bhs,bhsv,bhsd->bhvd", g, vc, kc)
        n_new = a_last[..., None] * n + jnp.einsum("bhs,bhsd->bhd", g, kc)
        return (C_new, n_new, m_new), h

    (C1, n1, m1), hs = lax.scan(step, (C0, n0, m0),
                                (chunks(q), chunks(k), chunks(v), chunks(ig), chunks(fg)))
    hs = jnp.moveaxis(jnp.swapaxes(hs, 2, 3), 0, 1).reshape(B, T, H, v.shape[-1])
    return hs, C1, n1, m1


def pool_mix(u, hist, pos0, w_pool, s_pool):
    B, T, _ = u.shape
    z = jnp.concatenate([hist.astype(jnp.float32), u.astype(jnp.float32)], axis=1)
    S = jnp.concatenate([jnp.zeros((B, 1, W_B), jnp.float32), jnp.cumsum(z, axis=1)], axis=1)
    n_avail = (pos0 + jnp.arange(T) + 1).astype(jnp.float32)
    outs = []
    for g, w in enumerate(POOL_WINDOWS):
        sl = slice(g * CG_B, (g + 1) * CG_B)
        wsum = S[:, POOL_HIST + 1:POOL_HIST + 1 + T, sl] - S[:, POOL_HIST + 1 - w:POOL_HIST + 1 - w + T, sl]
        cnt = jnp.minimum(jnp.float32(w), n_avail)[None, :, None]
        outs.append(wsum / cnt - z[:, POOL_HIST:, sl])
    y = jnp.stack(outs, axis=2)
    y = jnp.einsum("btgc,gcd->btgd", y, w_pool).reshape(B, T, W_B) * s_pool
    return y, z[:, -POOL_HIST:]


def fox_block(q, Fq, qpos, k, v, Ft, kpos):
    s = jnp.einsum("bqhd,bshd->bhqs", q, k) * (DH_C ** -0.5)
    s = s + jnp.swapaxes(Fq, 1, 2)[..., None] - Ft[:, :, None, :]
    s = jnp.where(qpos[:, None] >= kpos[None, :], s, -jnp.inf)
    p = jax.nn.softmax(s, axis=-1)
    return jnp.einsum("bhqs,bshd->bqhd", p, v)


def fox_prompt(q, k, v, lf):
    B, T, H, dh = q.shape
    QB = min(Q_BLOCK, T)
    nb = T // QB
    F = jnp.cumsum(lf, axis=1)
    Ft = jnp.swapaxes(F, 1, 2)
    kpos = jnp.arange(T)
    qb = jnp.swapaxes(q.reshape(B, nb, QB, H, dh), 0, 1)
    Fb = jnp.swapaxes(F.reshape(B, nb, QB, H), 0, 1)
    starts = jnp.arange(nb) * QB
    out = lax.map(lambda a: fox_block(a[0], a[1], a[2] + jnp.arange(QB), k, v, Ft, kpos), (qb, Fb, starts))
    return jnp.swapaxes(out, 0, 1).reshape(B, T, H, dh)


def fox_sample(q, k, v, lf, cache_k, cache_v, cache_logf, page_table, l):
    B, T, H, dh = q.shape
    kp = cache_k[l, page_table]
    P = kp.shape[1] * kp.shape[2]
    kp = kp.reshape(B, P, H, dh).astype(jnp.float32)
    vp = cache_v[l, page_table].reshape(B, P, H, dh).astype(jnp.float32)
    lfp = cache_logf[l, page_table].reshape(B, P, H).astype(jnp.float32)
    Ft = jnp.swapaxes(jnp.cumsum(jnp.concatenate([lfp, lf], axis=1), axis=1), 1, 2)
    Fq = Ft[:, :, P:]
    scale = DH_C ** -0.5
    s_past = jnp.einsum("bqhd,bphd->bhqp", q, kp) * scale + Fq[..., None] - Ft[:, :, None, :P]
    s_new = jnp.einsum("bqhd,bkhd->bhqk", q, k) * scale + Fq[..., None] - Ft[:, :, None, P:]
    s_new = jnp.where(jnp.tril(jnp.ones((T, T), dtype=bool)), s_new, -jnp.inf)
    p = jax.nn.softmax(jnp.concatenate([s_past, s_new], axis=-1), axis=-1)
    return jnp.einsum("bhqp,bphd->bqhd", p[..., :P], vp) + jnp.einsum("bhqk,bkhd->bqhd", p[..., P:], v)


def trunk_layer(l, x, c, samp, W):
    (w_ada, b_ada, g_norm1, g_norm2, w_in, b_in, g_head_a, w_pool, s_pool, w_out, w_up, w_down) = W
    f32 = jnp.float32
    B, T, _ = x.shape
    mod = jax.nn.silu(c.astype(f32)) @ w_ada[l].astype(f32) + b_ada[l].astype(f32)
    sh1, sc1, g1, sh2, sc2, g2 = jnp.split(mod[:, None, :], 6, axis=-1)
    h = (rmsnorm(x, g_norm1[l]) * (1.0 + sc1) + sh1).astype(x.dtype)
    p = (h @ w_in[l] + b_in[l]).astype(f32)
    qa, ka, va, ia, fa, oa, u, qc, kc, vc, fc = split_cols(p)
    if samp is None:
        C0 = jnp.zeros((B, H_A, DV_A, DK_A), f32)
        n0 = jnp.zeros((B, H_A, DK_A), f32)
        m0 = jnp.zeros((B, H_A), f32)
        hist = jnp.zeros((B, POOL_HIST, W_B), f32)
        pos0 = 0
    else:
        cache_k, cache_v, cache_logf, page_table, state_C, state_n, state_m, state_pool = samp
        C0 = state_C[l].astype(f32)
        n0 = state_n[l].astype(f32)
        m0 = state_m[l].astype(f32)
        hist = state_pool[l].astype(f32)
        pos0 = page_table.shape[1] * cache_k.shape[2]
    ha, C1, n1, m1 = mlstm(qa.reshape(B, T, H_A, DK_A), ka.reshape(B, T, H_A, DK_A) * (DK_A ** -0.5),
                           va.reshape(B, T, H_A, DV_A), ia, fa, C0, n0, m0)
    ha = ha * lax.rsqrt(jnp.mean(ha * ha, axis=-1, keepdims=True) + EPS) * g_head_a[l].astype(f32)
    ha = jax.nn.sigmoid(oa) * ha.reshape(B, T, W_A)
    hb, hist1 = pool_mix(u, hist, pos0, w_pool[l].astype(f32), s_pool[l].astype(f32))
    q_c = qc.reshape(B, T, H_C, DH_C)
    k_c = kc.reshape(B, T, H_C, DH_C)
    v_c = vc.reshape(B, T, H_C, DH_C)
    lf = jax.nn.log_sigmoid(fc)
    if samp is None:
        hc = fox_prompt(q_c, k_c, v_c, lf)
    else:
        hc = fox_sample(q_c, k_c, v_c, lf, cache_k, cache_v, cache_logf, page_table, l)
    mix = jnp.concatenate([ha, hb, hc.reshape(B, T, W_C)], axis=-1).astype(x.dtype) @ w_out[l]
    x = (x + g1 * mix).astype(x.dtype)
    h2 = (rmsnorm(x, g_norm2[l]) * (1.0 + sc2) + sh2).astype(x.dtype)
    ff = jnp.square(jax.nn.relu(h2 @ w_up[l])) @ w_down[l]
    x = (x + g2 * ff).astype(x.dtype)
    new = (k_c.astype(x.dtype), v_c.astype(x.dtype), lf.astype(x.dtype), C1, n1, m1, hist1)
    return x, new


def setup_inputs(seed: int = 0) -> dict:
    key = jax.random.key(seed)
    ks = jax.random.split(key, 26)
    f32 = jnp.float32

    def nrm(k, shape, s):
        return jax.random.normal(k, shape, f32) * s

    n_pages = PAST_LEN // PAGE_SIZE
    n_phys = (5 * DEC_BATCH * n_pages + 3) // 4
    page_table = jax.random.permutation(ks[0], n_phys)[:DEC_BATCH * n_pages].reshape(DEC_BATCH, n_pages).astype(jnp.int32)
    off_fa = col_offset(4)
    off_fc = col_offset(10)
    gate_off = jnp.zeros((N_IN,), f32)
    gate_off = gate_off.at[off_fa:off_fa + H_A].set(jnp.linspace(3.0, 6.0, H_A))
    gate_off = gate_off.at[off_fc:off_fc + H_C].set(FGATE_BIAS_C)
    return {
        "x_prompt": nrm(ks[1], (BATCH, SEQ, D_MODEL), 1.0),
        "x_sample": nrm(ks[2], (DEC_BATCH, DEC_SEQ, D_MODEL), 1.0),
        "c_prompt": nrm(ks[3], (BATCH, D_MODEL), 1.0),
        "c_sample": nrm(ks[4], (DEC_BATCH, D_MODEL), 1.0),
        "cache_k": nrm(ks[5], (DEPTH, n_phys, PAGE_SIZE, H_C, DH_C), 1.0),
        "cache_v": nrm(ks[6], (DEPTH, n_phys, PAGE_SIZE, H_C, DH_C), 1.0),
        "cache_logf": jax.nn.log_sigmoid(nrm(ks[7], (DEPTH, n_phys, PAGE_SIZE, H_C), 1.0) + FGATE_BIAS_C),
        "page_table": page_table,
        "state_C": nrm(ks[8], (DEPTH, DEC_BATCH, H_A, DV_A, DK_A), 0.1),
        "state_n": nrm(ks[9], (DEPTH, DEC_BATCH, H_A, DK_A), 0.1),
        "state_m": nrm(ks[10], (DEPTH, DEC_BATCH, H_A), 0.5),
        "state_pool": nrm(ks[11], (DEPTH, DEC_BATCH, POOL_HIST, W_B), 1.0),
        "w_ada": nrm(ks[12], (DEPTH, D_MODEL, 6 * D_MODEL), 0.5 * D_MODEL ** -0.5),
        "b_ada": nrm(ks[13], (DEPTH, 6 * D_MODEL), 0.02),
        "g_norm1": 1.0 + nrm(ks[14], (DEPTH, D_MODEL), 0.05),
        "g_norm2": 1.0 + nrm(ks[15], (DEPTH, D_MODEL), 0.05),
        "w_in": nrm(ks[16], (DEPTH, D_MODEL, N_IN), D_MODEL ** -0.5),
        "b_in": nrm(ks[17], (DEPTH, N_IN), 0.02) + gate_off[None, :],
        "g_head_a": 1.0 + nrm(ks[18], (DEPTH, H_A, DV_A), 0.05),
        "w_pool": nrm(ks[19], (DEPTH, G_B, CG_B, CG_B), CG_B ** -0.5),
        "s_pool": 1.0 + nrm(ks[20], (DEPTH, W_B), 0.1),
        "w_out": nrm(ks[21], (DEPTH, D_MIX, D_MODEL), D_MIX ** -0.5),
        "w_up": nrm(ks[22], (DEPTH, D_MODEL, D_FF), D_MODEL ** -0.5),
        "w_down": nrm(ks[23], (DEPTH, D_FF, D_MODEL), D_FF ** -0.5),
        "g_final": 1.0 + nrm(ks[24], (D_MODEL,), 0.05),
    }


def reference(x_prompt, x_sample, c_prompt, c_sample, cache_k, cache_v, cache_logf, page_table,
              state_C, state_n, state_m, state_pool, w_ada, b_ada, g_norm1, g_norm2, w_in, b_in,
              g_head_a, w_pool, s_pool, w_out, w_up, w_down, g_final):
    W = (w_ada, b_ada, g_norm1, g_norm2, w_in, b_in, g_head_a, w_pool, s_pool, w_out, w_up, w_down)
    samp = (cache_k, cache_v, cache_logf, page_table, state_C, state_n, state_m, state_pool)
    xp, xs = x_prompt, x_sample
    rows_p, rows_s = [], []
    for l in range(DEPTH):
        xp, sp = trunk_layer(l, xp, c_prompt, None, W)
        xs, ss = trunk_layer(l, xs, c_sample, samp, W)
        rows_p.append(sp)
        rows_s.append(ss)
    k_p, v_p, lf_p, C_p, n_p, m_p, pool_p = [jnp.stack(a) for a in zip(*rows_p)]
    k_s, v_s, lf_s, C_s, n_s, m_s, pool_s = [jnp.stack(a) for a in zip(*rows_s)]
    y_prompt = rmsnorm(xp, g_final).astype(x_prompt.dtype)
    y_sample = rmsnorm(xs, g_final).astype(x_sample.dtype)
    return (y_prompt, y_sample, k_p, v_p, lf_p, C_p, n_p, m_p, pool_p, k_s, v_s, lf_s, C_s, n_s, m_s, pool_s)
```

```python
import functools

import numpy as np
import jax
import jax.numpy as jnp
from jax import lax
from jax.experimental import pallas as pl
from jax.experimental.pallas import tpu as pltpu

F32 = jnp.float32
BF16 = jnp.bfloat16

D_MODEL = 1024
H_A = 4
W_A = 384
DV_A = 96
DK_A = 48
W_B = 256
CG_B = 64
POOL_HIST = 15
POOL_WINDOWS = (2, 4, 8, 16)
DH_C = 64
W_C = 384
H_C = 6
D_FF = 4096
EPS = 1e-6
CHUNK = 128
LOG2E = 1.4426950408889634
NEG = -1e30

LANES = 128
QK_PAD = 256
ST_COLS = W_A + LANES
MIX_AB = W_A + W_B
AUG_ONE0 = DH_C
AUG_F0 = DH_C + 8
VMEM_LIMIT = 56 * 1024 * 1024


def _dot(a, b):
    return jnp.dot(a, b, preferred_element_type=F32)


def _dot_nt(a, b):
    return lax.dot_general(a, b, (((1,), (1,)), ((), ())), preferred_element_type=F32)


def _dot_tn(a, b):
    return lax.dot_general(a, b, (((0,), (0,)), ((), ())), preferred_element_type=F32)


def _split3(x):
    a = x.astype(BF16)
    r = x - a.astype(F32)
    b = r.astype(BF16)
    c = (r - b.astype(F32)).astype(BF16)
    return a, b, c


def _dot3_rhs(m01, x):
    a, b, c = _split3(x)
    return _dot(m01, a) + _dot(m01, b) + _dot(m01, c)


def _dot3_lhs(x, m01):
    a, b, c = _split3(x)
    return _dot(a, m01) + _dot(b, m01) + _dot(c, m01)


def _log_sigmoid(x):
    return jnp.minimum(x, 0.0) - jnp.log1p(jnp.exp(-jnp.abs(x)))


def _sigmoid(x):
    return 1.0 / (1.0 + jnp.exp(-x))


def _iota(shape, dim):
    return lax.broadcasted_iota(jnp.int32, shape, dim)


def _expand_heads(src, width, per_head):
    lane = _iota((1, width), 1)
    out = jnp.zeros(src.shape[:-1] + (width,), F32)
    for h in range(H_A):
        out = jnp.where((lane >= h * per_head) & (lane < (h + 1) * per_head), src[:, h:h + 1], out)
    return out


def _mod_kernel(c_ref, w_ref, b_ref, o_ref):
    c = c_ref[...]
    s = (c * _sigmoid(c)).astype(BF16)
    o_ref[0] = _dot(s, w_ref[0].astype(BF16)) + b_ref[0]


def _modulation(c_all, w_ada, b_ada):
    depth, d, n6 = w_ada.shape
    rows = c_all.shape[0]
    tn = 1536
    return pl.pallas_call(
        _mod_kernel,
        grid=(depth, n6 // tn),
        in_specs=[
            pl.BlockSpec((rows, d), lambda l, j: (0, 0)),
            pl.BlockSpec((1, d, tn), lambda l, j: (l, 0, j)),
            pl.BlockSpec((1, 1, tn), lambda l, j: (l, 0, j)),
        ],
        out_specs=pl.BlockSpec((1, rows, tn), lambda l, j: (l, 0, j)),
        out_shape=jax.ShapeDtypeStruct((depth, rows, n6), F32),
        compiler_params=pltpu.CompilerParams(dimension_semantics=("arbitrary", "arbitrary"), vmem_limit_bytes=VMEM_LIMIT),
        name="modulation",
    )(c_all, w_ada, b_ada.reshape(depth, 1, n6))


NC_QA, NC_KA, NC_VA, NC_OA, NC_U, NC_G = 0, 256, 512, 896, 1280, 1536
NC_QC = 1664
N_NORMAL_PROMPT = NC_QC + H_C * LANES
N_NORMAL_SAMPLE = NC_QC + 3 * W_C
TR_K, TR_V, TR_G = 0, 384, 768
N_TRANS = 784


def _norm_mod(x, g, scale, shift):
    ms = jnp.mean(x * x, axis=-1, keepdims=True)
    return (x * lax.rsqrt(ms + EPS) * g) * (1.0 + scale) + shift


def _gate_rows(graw):
    r = _iota(graw.shape, 0)
    ls = _log_sigmoid(graw)
    return jnp.where((r < H_C) | (r >= 12), ls, jnp.where(r < 8, 0.0, graw))


def _gate_cols(graw):
    ln = _iota(graw.shape, 1)
    return jnp.where(ln < H_A, _log_sigmoid(graw), jnp.where(ln < 2 * H_A, graw, 0.0))


def _inproj_prompt_kernel(x_ref, mod_ref, g_ref, wn_ref, bn_ref, sn_ref, wt_ref, bt_ref, tri_ref,
                          qk_ref, va_ref, oa_ref, u_ref, gc_ref, qaug_ref, kta_ref, vta_ref, kt_ref, vt_ref, gt_ref,
                          carry_ref, *, tiles_per_seq):
    i = pl.program_id(0)
    tm = x_ref.shape[0]
    h = _norm_mod(x_ref[...], g_ref[...], mod_ref[1, 0], mod_ref[0, 0])
    hb = h.astype(BF16)

    def seg(off, width):
        return (_dot(hb, wn_ref[:, off:off + width]) + bn_ref[:, off:off + width]) * sn_ref[:, off:off + width]

    qk_ref[...] = seg(NC_QA, 2 * QK_PAD).astype(BF16)
    va_ref[...] = seg(NC_VA, W_A).astype(BF16)
    oa_ref[...] = seg(NC_OA, W_A)
    u_ref[...] = seg(NC_U, W_B)
    gc_ref[...] = _gate_cols(seg(NC_G, LANES))

    def tseg(off, rows):
        return _dot_nt(wt_ref[off:off + rows, :], hb) + bt_ref[off:off + rows, :]

    kt = tseg(TR_K, W_C)
    vt = tseg(TR_V, W_C)
    gt = _gate_rows(tseg(TR_G, 16))
    kt_ref[0] = kt
    vt_ref[0] = vt
    gt_ref[0] = gt

    first = (i % tiles_per_seq) == 0
    carry = jnp.where(first, 0.0, carry_ref[:, 0:1])
    ft = _dot3_lhs(gt[0:8] * LOG2E, tri_ref[...]) + carry
    carry_ref[...] = jnp.broadcast_to(ft[:, tm - 1:tm], carry_ref.shape)
    f1, f2, f3 = [p.astype(F32) for p in _split3(ft)]

    r8 = _iota((8, tm), 0)
    xaug = jnp.concatenate([jnp.zeros((AUG_ONE0, tm), F32), jnp.where(r8 < 3, 1.0, 0.0), f1, f2, f3,
                            jnp.zeros((LANES - AUG_F0 - 24, tm), F32)], axis=0)
    e = xaug.T
    r64 = _iota((LANES - DH_C, tm), 0)
    for hh in range(H_C):
        qaug_ref[hh] = (seg(NC_QC + hh * LANES, LANES) + e).astype(BF16)
        ka = jnp.where(r64 == 0, -f1[hh:hh + 1], jnp.where(r64 == 1, -f2[hh:hh + 1], jnp.where(r64 == 2, -f3[hh:hh + 1], 0.0)))
        ka = jnp.where((r64 == 8 + hh) | (r64 == 16 + hh) | (r64 == 24 + hh), 1.0, ka)
        kta_ref[0, hh] = jnp.concatenate([kt[hh * DH_C:(hh + 1) * DH_C].astype(BF16), ka.astype(BF16)], axis=0)
        vta_ref[0, hh] = jnp.concatenate([vt[hh * DH_C:(hh + 1) * DH_C].astype(BF16),
                                          jnp.where(r64 == 0, 1.0, 0.0).astype(BF16)], axis=0)


def _inproj_sample_kernel(x_ref, mod_ref, g_ref, wn_ref, bn_ref, sn_ref, wt_ref, bt_ref,
                          qk_ref, va_ref, oa_ref, u_ref, gc_ref, qc_ref, kc_ref, vc_ref, kt_ref, vt_ref, gt_ref):
    h = _norm_mod(x_ref[...], g_ref[...], mod_ref[1, 0], mod_ref[0, 0])
    hb = h.astype(BF16)

    def seg(off, width):
        return (_dot(hb, wn_ref[:, off:off + width]) + bn_ref[:, off:off + width]) * sn_ref[:, off:off + width]

    qk_ref[...] = seg(NC_QA, 2 * QK_PAD).astype(BF16)
    va_ref[...] = seg(NC_VA, W_A).astype(BF16)
    oa_ref[...] = seg(NC_OA, W_A)
    u_ref[...] = seg(NC_U, W_B)
    gc_ref[...] = _gate_cols(seg(NC_G, LANES))
    qc_ref[...] = seg(NC_QC, W_C)
    kc_ref[...] = seg(NC_QC + W_C, W_C)
    vc_ref[...] = seg(NC_QC + 2 * W_C, W_C)

    def tseg(off, rows):
        return _dot_nt(wt_ref[off:off + rows, :], hb) + bt_ref[off:off + rows, :]

    kt_ref[...] = tseg(TR_K, W_C)
    vt_ref[...] = tseg(TR_V, W_C)
    gt_ref[...] = _gate_rows(tseg(TR_G, 16))


def _inproj_weights(w_in, b_in, prompt):
    o = np.cumsum((0, H_A * DK_A, H_A * DK_A, W_A, H_A, H_A, W_A, W_B, W_C, W_C, W_C, H_C))
    col = lambda k: (w_in[:, o[k]:o[k + 1]], b_in[o[k]:o[k + 1]])
    (wqa, bqa), (wka, bka), (wva, bva), (wia, bia), (wfa, bfa), (woa, boa), (wu, bu), (wqc, bqc), (wkc, bkc), (wvc, bvc), (wfc, bfc) = [col(k) for k in range(11)]
    d = w_in.shape[0]
    zw = lambda n: jnp.zeros((d, n), F32)
    zb = lambda n: jnp.zeros((n,), F32)
    qscale = DH_C ** -0.5 * LOG2E
    ws = [wqa, zw(QK_PAD - 192), wka, zw(QK_PAD - 192), wva, woa, wu, wfa, wia, zw(LANES - 8)]
    bs = [bqa, zb(QK_PAD - 192), bka, zb(QK_PAD - 192), bva, boa, bu, bfa, bia, zb(LANES - 8)]
    ss = [jnp.ones((QK_PAD,), F32), jnp.full((QK_PAD,), DK_A ** -0.5, F32), jnp.ones((W_A + W_A + W_B + LANES,), F32)]
    if prompt:
        for hh in range(H_C):
            ws += [wqc[:, hh * DH_C:(hh + 1) * DH_C], zw(LANES - DH_C)]
            bs += [bqc[hh * DH_C:(hh + 1) * DH_C], zb(LANES - DH_C)]
        ss += [jnp.full((H_C * LANES,), qscale, F32)]
    else:
        ws += [wqc, wkc, wvc]
        bs += [bqc, bkc, bvc]
        ss += [jnp.full((W_C,), qscale, F32), jnp.ones((2 * W_C,), F32)]
    wn = jnp.concatenate(ws, axis=1).astype(BF16)
    bn = jnp.concatenate(bs)[None, :]
    sn = jnp.concatenate(ss)[None, :]
    wt = jnp.concatenate([wkc, wvc, wfc, zw(2), wia, wfa], axis=1).T.astype(BF16)
    bt = jnp.concatenate([bkc, bvc, bfc, zb(2), bia, bfa])[:, None]
    return wn, bn, sn, wt, bt


def _inproj_prompt(x2, mod4, g1, weights, batch, seq):
    wn, bn, sn, wt, bt = weights
    rows, d = x2.shape
    tm = min(512, seq)
    tps = seq // tm
    tri = jnp.asarray(np.triu(np.ones((tm, tm), np.float32)), BF16)
    full = lambda a: pl.BlockSpec(a.shape, lambda i: (0,) * a.ndim)
    rowblk = lambda w: pl.BlockSpec((tm, w), lambda i: (i, 0))
    tblk = lambda r: pl.BlockSpec((1, r, tm), lambda i: (i // tps, 0, i % tps))
    tblk4 = pl.BlockSpec((1, H_C, LANES, tm), lambda i: (i // tps, 0, 0, i % tps))
    outs = pl.pallas_call(
        functools.partial(_inproj_prompt_kernel, tiles_per_seq=tps),
        grid=(rows // tm,),
        in_specs=[rowblk(d), pl.BlockSpec((6, 1, 1, d), lambda i: (0, i // tps, 0, 0)), full(g1),
                  full(wn), full(bn), full(sn), full(wt), full(bt), full(tri)],
        out_specs=[rowblk(2 * QK_PAD), rowblk(W_A), rowblk(W_A), rowblk(W_B), rowblk(LANES),
                   pl.BlockSpec((H_C, tm, LANES), lambda i: (0, i, 0)), tblk4, tblk4, tblk(W_C), tblk(W_C), tblk(16)],
        out_shape=[jax.ShapeDtypeStruct((rows, 2 * QK_PAD), BF16), jax.ShapeDtypeStruct((rows, W_A), BF16),
                   jax.ShapeDtypeStruct((rows, W_A), F32), jax.ShapeDtypeStruct((rows, W_B), F32),
                   jax.ShapeDtypeStruct((rows, LANES), F32),
                   jax.ShapeDtypeStruct((H_C, rows, LANES), BF16),
                   jax.ShapeDtypeStruct((batch, H_C, LANES, seq), BF16), jax.ShapeDtypeStruct((batch, H_C, LANES, seq), BF16),
                   jax.ShapeDtypeStruct((batch, W_C, seq), F32), jax.ShapeDtypeStruct((batch, W_C, seq), F32),
                   jax.ShapeDtypeStruct((batch, 16, seq), F32)],
        scratch_shapes=[pltpu.VMEM((8, LANES), F32)],
        compiler_params=pltpu.CompilerParams(dimension_semantics=("arbitrary",), vmem_limit_bytes=VMEM_LIMIT),
        name="inproj_prompt",
    )(x2, mod4, g1, wn, bn, sn, wt, bt, tri)
    return outs


def _inproj_sample(x2, mod4, g1, weights):
    wn, bn, sn, wt, bt = weights
    rows, d = x2.shape
    full = lambda a: pl.BlockSpec(a.shape, lambda i: (0,) * a.ndim)
    o2 = lambda r, c: pl.BlockSpec((r, c), lambda i: (0, 0))
    sds = lambda r, c, t: jax.ShapeDtypeStruct((r, c), t)
    return pl.pallas_call(
        _inproj_sample_kernel,
        grid=(1,),
        in_specs=[full(x2), full(mod4), full(g1), full(wn), full(bn), full(sn), full(wt), full(bt)],
        out_specs=[o2(rows, 2 * QK_PAD), o2(rows, W_A), o2(rows, W_A), o2(rows, W_B), o2(rows, LANES),
                   o2(rows, W_C), o2(rows, W_C), o2(rows, W_C), o2(W_C, rows), o2(W_C, rows), o2(16, rows)],
        out_shape=[sds(rows, 2 * QK_PAD, BF16), sds(rows, W_A, BF16), sds(rows, W_A, F32), sds(rows, W_B, F32), sds(rows, LANES, F32),
                   sds(rows, W_C, F32), sds(rows, W_C, F32), sds(rows, W_C, F32), sds(W_C, rows, F32), sds(W_C, rows, F32), sds(16, rows, F32)],
        compiler_params=pltpu.CompilerParams(dimension_semantics=("arbitrary",), vmem_limit_bytes=VMEM_LIMIT),
        name="inproj_sample",
    )(x2, mod4, g1, wn, bn, sn, wt, bt)


def _state_mask():
    r = np.arange(QK_PAD)[:, None]
    c = np.arange(ST_COLS)[None, :]
    m = np.zeros((QK_PAD, ST_COLS), np.float32)
    for h in range(H_A):
        rows = (r >= h * DK_A) & (r < (h + 1) * DK_A)
        cols = ((c >= h * DV_A) & (c < (h + 1) * DV_A)) | (c == W_A + h)
        m[rows & cols] = 1.0
    return m


def _seq_kernel(qk_ref, v_ref, o_ref, u_ref, gc_ref, gt_ref, st0_ref, m0_ref, hist0_ref, gh_ref, wp_ref, sp_ref, mask_ref,
                mix_ref, st_out_ref, m_out_ref, hist_out_ref, st_scr, m_scr, z_scr, *, pos0):
    c = pl.program_id(1)
    L = CHUNK

    @pl.when(c == 0)
    def _():
        st_scr[...] = st0_ref[0]
        m_scr[...] = m0_ref[0]
        z_scr[0:16, :] = hist0_ref[0]

    q = qk_ref[:, 0:QK_PAD]
    k = qk_ref[:, QK_PAD:2 * QK_PAD]
    v = v_ref[...]
    gc = gc_ref[...]
    gt = gt_ref[0]
    st = st_scr[...]
    m_row = m_scr[0:1, :]

    row = _iota((L, L), 0)
    colm = _iota((L, L), 1)
    tri_l = jnp.where(colm <= row, 1.0, 0.0).astype(BF16)
    tri_u = jnp.where(row <= colm, 1.0, 0.0).astype(BF16)
    lane128 = _iota((1, LANES), 1)
    lane_q = _iota((1, QK_PAD), 1)
    lane_v = _iota((1, W_A), 1)

    bc_col = _dot3_rhs(tri_l, gc)
    bc_row = _dot3_lhs(gt[8:16], tri_u)
    inter = bc_col + m_row
    ia_col = pltpu.roll(gc, LANES - H_A, axis=1)

    causal = colm <= row
    mt_all = jnp.zeros((L, LANES), F32)
    s_list = []
    vblk = []
    for h in range(H_A):
        d = bc_col[:, h:h + 1] - bc_row[4 + h:5 + h, :] + gt[8 + h:9 + h, :]
        d = jnp.where(causal, d, NEG)
        mt = jnp.maximum(inter[:, h:h + 1], jnp.max(d, axis=-1, keepdims=True))
        qm = jnp.where((lane_q >= h * DK_A) & (lane_q < (h + 1) * DK_A), q, jnp.zeros_like(q))
        s = _dot_nt(qm, k) * jnp.exp(d - mt)
        s_list.append(s.astype(BF16))
        mt_all = jnp.where(lane128 == h, mt, mt_all)
        vm = jnp.where((lane_v >= h * DV_A) & (lane_v < (h + 1) * DV_A), v, jnp.zeros_like(v))
        vblk.append(jnp.concatenate([vm, jnp.broadcast_to(jnp.where(lane128 == h, 1.0, 0.0).astype(BF16), (L, LANES))], axis=1))
    s_cat = jnp.concatenate(s_list, axis=1)
    v_blk = jnp.concatenate(vblk, axis=0)
    pv = _dot(s_cat, v_blk)
    qs = _dot(q, st.astype(BF16))

    valid = lane128 < H_A
    a_all = jnp.where(valid, jnp.exp(inter - mt_all), 0.0)
    num = _expand_heads(a_all, W_A, DV_A) * qs[:, 0:W_A] + pv[:, 0:W_A]
    den = a_all * qs[:, W_A:] + pv[:, W_A:]
    inv = 1.0 / jnp.maximum(jnp.abs(den), jnp.exp(-mt_all))
    hs = num * _expand_heads(jnp.where(valid, inv, 0.0), W_A, DV_A)

    sq = hs * hs
    ss_all = jnp.zeros((L, LANES), F32)
    for h in range(H_A):
        ssum = jnp.sum(jnp.where((lane_v >= h * DV_A) & (lane_v < (h + 1) * DV_A), sq, 0.0), axis=-1, keepdims=True)
        ss_all = jnp.where(lane128 == h, ssum, ss_all)
    r_all = lax.rsqrt(ss_all * (1.0 / DV_A) + EPS)
    ha = _sigmoid(o_ref[...]) * hs * _expand_heads(r_all, W_A, DV_A) * gh_ref[...]

    m_new = mt_all[L - 1:L, :]
    g_col = jnp.where(valid, jnp.exp(bc_col[L - 1:L, :] - bc_col + ia_col - m_new), 0.0)
    a_last = jnp.where(valid, jnp.exp(inter[L - 1:L, :] - m_new), 0.0)
    gv = (v.astype(F32) * _expand_heads(g_col, W_A, DV_A)).astype(BF16)
    upd = _dot_tn(k, jnp.concatenate([gv, g_col.astype(BF16)], axis=1))
    a512 = jnp.concatenate([_expand_heads(a_last, W_A, DV_A), a_last], axis=1)
    st_new = a512 * st + mask_ref[...] * upd
    st_scr[...] = st_new
    m_scr[...] = jnp.broadcast_to(jnp.where(valid, m_new, 0.0), m_scr.shape)

    u = u_ref[...]
    z_scr[16:16 + L, :] = u
    z = z_scr[...]
    s2 = z + pltpu.roll(z, 1, axis=0)
    s4 = s2 + pltpu.roll(s2, 2, axis=0)
    s8 = s4 + pltpu.roll(s4, 4, axis=0)
    s16 = s8 + pltpu.roll(s8, 8, axis=0)
    navail = (pos0 + c * L + 1 + _iota((L, 1), 0)).astype(F32)
    lane_u = _iota((1, W_B), 1)
    y = None
    for g, (w, sw) in enumerate(zip(POOL_WINDOWS, (s2, s4, s8, s16))):
        yg = sw[16:16 + L, :] * (1.0 / jnp.minimum(float(w), navail))
        y = yg if y is None else jnp.where(lane_u >= g * CG_B, yg, y)
    y = y - u
    hb = _dot(y.astype(BF16), wp_ref[...]) * sp_ref[...]
    z_scr[0:16, :] = u[L - 16:L, :]
    mix_ref[...] = jnp.concatenate([ha, hb], axis=1).astype(BF16)

    @pl.when(c == pl.num_programs(1) - 1)
    def _():
        st_out_ref[0] = st_new
        m_out_ref[0] = m_scr[...]
        hist_out_ref[0] = u[L - 16:L, :]


def _sequence(qk, va, oa, u, gc, gt, st0, m0, hist0, gh, wp, sp, nb, nchunk, pos0):
    L = CHUNK
    mask = jnp.asarray(_state_mask())
    rowblk = lambda w: pl.BlockSpec((L, w), lambda b, c: (b * nchunk + c, 0))
    perb = lambda a: pl.BlockSpec((1,) + a.shape[1:], lambda b, c: (b,) + (0,) * (a.ndim - 1))
    full = lambda a: pl.BlockSpec(a.shape, lambda b, c: (0,) * a.ndim)
    rows = nb * nchunk * L
    return pl.pallas_call(
        functools.partial(_seq_kernel, pos0=pos0),
        grid=(nb, nchunk),
        in_specs=[rowblk(2 * QK_PAD), rowblk(W_A), rowblk(W_A), rowblk(W_B), rowblk(LANES),
                  pl.BlockSpec((1, 16, L), lambda b, c: (b, 0, c)),
                  perb(st0), perb(m0), perb(hist0), full(gh), full(wp), full(sp), full(mask)],
        out_specs=[rowblk(MIX_AB), pl.BlockSpec((1, QK_PAD, ST_COLS), lambda b, c: (b, 0, 0)),
                   pl.BlockSpec((1, 8, LANES), lambda b, c: (b, 0, 0)), pl.BlockSpec((1, 16, W_B), lambda b, c: (b, 0, 0))],
        out_shape=[jax.ShapeDtypeStruct((rows, MIX_AB), BF16), jax.ShapeDtypeStruct((nb, QK_PAD, ST_COLS), F32),
                   jax.ShapeDtypeStruct((nb, 8, LANES), F32), jax.ShapeDtypeStruct((nb, 16, W_B), F32)],
        scratch_shapes=[pltpu.VMEM((QK_PAD, ST_COLS), F32), pltpu.VMEM((8, LANES), F32), pltpu.VMEM((16 + L, W_B), F32)],
        compiler_params=pltpu.CompilerParams(dimension_semantics=("arbitrary", "arbitrary")),
        name="sequence",
    )(qk, va, oa, u, gc, gt, st0, m0, hist0, gh, wp, sp, mask)


def _pack_state(C, n, m):
    nb = C.shape[0]
    st = jnp.zeros((nb, QK_PAD, ST_COLS), F32)
    for h in range(H_A):
        st = st.at[:, h * DK_A:(h + 1) * DK_A, h * DV_A:(h + 1) * DV_A].set(jnp.swapaxes(C[:, h], 1, 2))
        st = st.at[:, h * DK_A:(h + 1) * DK_A, W_A + h].set(n[:, h])
    mm = jnp.zeros((nb, 8, LANES), F32).at[:, :, 0:H_A].set(jnp.broadcast_to(m[:, None, :], (nb, 8, H_A)))
    return st, mm


def _unpack_state(st, mm):
    C = jnp.stack([jnp.swapaxes(st[:, h * DK_A:(h + 1) * DK_A, h * DV_A:(h + 1) * DV_A], 1, 2) for h in range(H_A)], axis=1)
    n = jnp.stack([st[:, h * DK_A:(h + 1) * DK_A, W_A + h] for h in range(H_A)], axis=1)
    return C, n, mm[:, 0, 0:H_A]


def _attn_prompt_kernel(q_ref, kt_ref, vt_ref, o_ref, *, tq, tk):
    qi = pl.program_id(2)
    outs = []
    for hh in range(2):
        q = q_ref[hh]

        def step(j, carry, masked):
            m, acc = carry
            start = pl.multiple_of(j * tk, tk)
            ktb = kt_ref[0, hh, :, pl.ds(start, tk)]
            vtb = vt_ref[0, hh, :, pl.ds(start, tk)]
            s = _dot(q, ktb)
            if masked:
                rowg = qi * tq + _iota((tq, tk), 0)
                colg = j * tk + _iota((tq, tk), 1)
                s = jnp.where(colg <= rowg, s, NEG)
            m_new = jnp.maximum(m, jnp.max(s, axis=-1, keepdims=True))
            p = jnp.exp2(s - m_new)
            acc = jnp.exp2(m - m_new) * acc + _dot_nt(p.astype(BF16), vtb)
            return m_new, acc

        jd = (qi * tq) // tk
        carry = (jnp.full((tq, 1), NEG, F32), jnp.zeros((tq, LANES), F32))
        carry = lax.fori_loop(0, jd, lambda j, cr: step(j, cr, False), carry)
        m, acc = step(jd, carry, True)
        outs.append(acc * (1.0 / acc[:, DH_C:DH_C + 1]))
    lane = _iota((1, LANES), 1)
    o_ref[...] = jnp.where(lane < DH_C, outs[0], pltpu.roll(outs[1], DH_C, axis=1)).astype(BF16)


def _attn_prompt(q_aug, kta, vta, batch, seq):
    rows = batch * seq
    tq = min(256, seq)
    tk = min(512, seq)
    nq = seq // tq
    return pl.pallas_call(
        functools.partial(_attn_prompt_kernel, tq=tq, tk=tk),
        grid=(batch, H_C // 2, nq),
        in_specs=[pl.BlockSpec((2, tq, LANES), lambda b, p, i: (p, b * nq + i, 0)),
                  pl.BlockSpec((1, 2, LANES, seq), lambda b, p, i: (b, p, 0, 0)),
                  pl.BlockSpec((1, 2, LANES, seq), lambda b, p, i: (b, p, 0, 0))],
        out_specs=pl.BlockSpec((tq, LANES), lambda b, p, i: (b * nq + i, p)),
        out_shape=jax.ShapeDtypeStruct((rows, W_C), BF16),
        compiler_params=pltpu.CompilerParams(dimension_semantics=("arbitrary", "arbitrary", "arbitrary"), vmem_limit_bytes=VMEM_LIMIT),
        name="attn_prompt",
    )(q_aug, kta, vta)


def _attn_sample_kernel(pt_ref, q_ref, knt_ref, vnt_ref, lfn_ref, *refs, n_pages_step):
    P = n_pages_step
    k_refs, v_refs, lf_refs = refs[0:P], refs[P:2 * P], refs[2 * P:3 * P]
    o_ref = refs[3 * P]
    m_scr, acc_scr, carry_scr, cq_scr = refs[3 * P + 1:]
    j = pl.program_id(1)
    R = 8 * H_C
    lane_c = _iota((1, W_C), 1)
    lane = _iota((R, LANES), 1)
    rq = _iota((R, LANES), 0) % 8

    q8 = q_ref[0]
    qbd = jnp.concatenate([jnp.where((lane_c >= h * DH_C) & (lane_c < (h + 1) * DH_C), q8, 0.0) for h in range(H_C)], axis=0).astype(BF16)

    def rows_of_heads(x8):
        return jnp.concatenate([jnp.broadcast_to(x8[h:h + 1], (8, LANES)) for h in range(H_C)], axis=0)

    @pl.when(j == 0)
    def _():
        a = rows_of_heads(lfn_ref[0] * LOG2E)
        cq = jnp.sum(jnp.where(lane <= rq, a, 0.0), axis=-1, keepdims=True)
        incl = jnp.where(_iota((LANES, LANES), 0) <= _iota((LANES, LANES), 1), 1.0, 0.0).astype(BF16)
        crow = _dot3_lhs(a, incl)
        s = _dot(qbd, knt_ref[0].astype(BF16)) + cq - crow
        s = jnp.where((lane <= rq) & (lane < 4), s, NEG)
        m = jnp.max(s, axis=-1, keepdims=True)
        p = jnp.exp2(s - m)
        l = jnp.sum(p, axis=-1, keepdims=True)
        acc_scr[:, 0:W_C] = _dot_nt(p.astype(BF16), vnt_ref[0].astype(BF16))
        acc_scr[:, W_C:] = jnp.broadcast_to(l, (R, LANES))
        m_scr[...] = jnp.broadcast_to(m, (R, LANES))
        cq_scr[...] = jnp.broadcast_to(cq, (R, LANES))
        carry_scr[...] = jnp.zeros_like(carry_scr)

    cq = cq_scr[:, 0:1]
    carry = carry_scr[:, 0:1]
    strict = jnp.where(_iota((LANES, LANES), 0) > _iota((LANES, LANES), 1), 1.0, 0.0).astype(BF16)
    s_pages = [None] * P
    for i in reversed(range(P)):
        lf8 = lf_refs[i][0] * LOG2E
        suf = _dot3_lhs(lf8, strict) + carry
        carry = suf[:, 0:1] + lf8[:, 0:1]
        s_pages[i] = _dot(qbd, k_refs[i][0].astype(BF16)) + rows_of_heads(suf) + cq
    carry_scr[...] = jnp.broadcast_to(carry, carry_scr.shape)

    m_old = m_scr[:, 0:1]
    m_new = m_old
    for s in s_pages:
        m_new = jnp.maximum(m_new, jnp.max(s, axis=-1, keepdims=True))
    alpha = jnp.exp2(m_old - m_new)
    acc = alpha * acc_scr[:, 0:W_C]
    l = alpha * acc_scr[:, W_C:W_C + 1]
    for i in range(P):
        p = jnp.exp2(s_pages[i] - m_new)
        l = l + jnp.sum(p, axis=-1, keepdims=True)
        acc = acc + _dot_nt(p.astype(BF16), v_refs[i][0].astype(BF16))
    acc_scr[:, 0:W_C] = acc
    acc_scr[:, W_C:] = jnp.broadcast_to(l, (R, LANES))
    m_scr[...] = jnp.broadcast_to(m_new, (R, LANES))

    @pl.when(j == pl.num_programs(1) - 1)
    def _():
        o48 = acc * (1.0 / l)
        out = jnp.zeros((8, W_C), F32)
        for h in range(H_C):
            out = jnp.where((lane_c >= h * DH_C) & (lane_c < (h + 1) * DH_C), o48[8 * h:8 * h + 8], out)
        o_ref[0] = out


def _attn_sample(page_table, q8, knt, vnt, lfn, ckt, cvt, clf, layer, n_phys):
    nb, n_pages = page_table.shape
    P = min(8, n_pages)
    nstep = n_pages // P
    base = layer * n_phys

    def page_map(i):
        return lambda b, j, pt: (base + pt[b * n_pages + (nstep - 1 - j) * P + i], 0, 0)

    perb = lambda a: pl.BlockSpec((1,) + a.shape[1:], lambda b, j, pt: (b, 0, 0))
    in_specs = [perb(q8), perb(knt), perb(vnt), perb(lfn)]
    in_specs += [pl.BlockSpec((1, W_C, LANES), page_map(i)) for i in range(P)]
    in_specs += [pl.BlockSpec((1, W_C, LANES), page_map(i)) for i in range(P)]
    in_specs += [pl.BlockSpec((1, 8, LANES), page_map(i)) for i in range(P)]
    R = 8 * H_C
    grid_spec = pltpu.PrefetchScalarGridSpec(
        num_scalar_prefetch=1, grid=(nb, nstep), in_specs=in_specs,
        out_specs=pl.BlockSpec((1, 8, W_C), lambda b, j, pt: (b, 0, 0)),
        scratch_shapes=[pltpu.VMEM((R, LANES), F32), pltpu.VMEM((R, W_C + LANES), F32),
                        pltpu.VMEM((8, LANES), F32), pltpu.VMEM((R, LANES), F32)])
    return pl.pallas_call(
        functools.partial(_attn_sample_kernel, n_pages_step=P),
        grid_spec=grid_spec,
        out_shape=jax.ShapeDtypeStruct((nb, 8, W_C), F32),
        compiler_params=pltpu.CompilerParams(dimension_semantics=("arbitrary", "arbitrary"), vmem_limit_bytes=VMEM_LIMIT),
        name="attn_sample",
    )(page_table.reshape(-1), q8, knt, vnt, lfn, *([ckt] * P), *([cvt] * P), *([clf] * P))


def _mlp_kernel(x_ref, mab_ref, hc_ref, mod_ref, g2_ref, gf_ref, wo_ref, wu_ref, wd_ref, *out_refs, final):
    x = x_ref[...]
    wo = wo_ref
    mix = _dot(mab_ref[...], wo[0:MIX_AB, :]) + _dot(hc_ref[...], wo[MIX_AB:, :])
    x1 = x + mod_ref[2, 0] * mix
    h2 = _norm_mod(x1, g2_ref[...], mod_ref[4, 0], mod_ref[3, 0]).astype(BF16)
    ff = jnp.zeros_like(x)
    fc = 1024
    for f in range(D_FF // fc):
        a = jnp.maximum(_dot(h2, wu_ref[:, f * fc:(f + 1) * fc]), 0.0)
        ff = ff + _dot((a * a).astype(BF16), wd_ref[f * fc:(f + 1) * fc, :])
    x2 = x1 + mod_ref[5, 0] * ff
    out_refs[0][...] = x2
    if final:
        ms = jnp.mean(x2 * x2, axis=-1, keepdims=True)
        out_refs[1][...] = x2 * lax.rsqrt(ms + EPS) * gf_ref[...]


def _mlp(x2, mab, hc, mod4, g2, gf, wo, wu, wd, tm, rows_per_mod, final):
    rows, d = x2.shape
    s = mod4.shape[2]
    const = lambda a: pl.BlockSpec(a.shape, lambda i: (0,) * a.ndim, pipeline_mode=pl.Buffered(1))
    rowblk = lambda w: pl.BlockSpec((tm, w), lambda i: (i, 0))
    n_out = 2 if final else 1
    outs = pl.pallas_call(
        functools.partial(_mlp_kernel, final=final),
        grid=(rows // tm,),
        in_specs=[rowblk(d), rowblk(MIX_AB), rowblk(W_C),
                  pl.BlockSpec((6, 1, s, d), lambda i: (0, (i * tm) // rows_per_mod, 0, 0)),
                  const(g2), const(gf), const(wo), const(wu), const(wd)],
        out_specs=[rowblk(d)] * n_out,
        out_shape=[jax.ShapeDtypeStruct((rows, d), F32)] * n_out,
        compiler_params=pltpu.CompilerParams(dimension_semantics=("arbitrary",), vmem_limit_bytes=VMEM_LIMIT),
        name="mlp",
    )(x2, mab, hc, mod4, g2, gf, wo, wu, wd)
    return outs


def kernel(x_prompt, x_sample, c_prompt, c_sample, cache_k, cache_v, cache_logf, page_table, state_C, state_n, state_m, state_pool, w_ada, b_ada, g_norm1, g_norm2, w_in, b_in, g_head_a, w_pool, s_pool, w_out, w_up, w_down, g_final):
    depth = w_ada.shape[0]
    B, T, D = x_prompt.shape
    SB, ST, _ = x_sample.shape
    n_phys, page = cache_k.shape[1], cache_k.shape[2]
    n_pages = page_table.shape[1]
    rows_p, rows_s = B * T, SB * ST
    pos0_s = n_pages * page
    assert T % CHUNK == 0 and ST <= 8 and page == LANES

    nc = B + SB
    c_all = jnp.concatenate([c_prompt, c_sample, jnp.zeros((-nc % 8, D), F32)], axis=0)
    mod = _modulation(c_all, w_ada, b_ada)

    ckt = jnp.transpose(cache_k, (0, 1, 3, 4, 2)).reshape(depth * n_phys, W_C, page)
    cvt = jnp.transpose(cache_v, (0, 1, 3, 4, 2)).reshape(depth * n_phys, W_C, page)
    clf = jnp.pad(jnp.transpose(cache_logf, (0, 1, 3, 2)), ((0, 0), (0, 0), (0, 8 - H_C), (0, 0))).reshape(depth * n_phys, 8, page)

    xp = x_prompt.reshape(rows_p, D)
    xs = x_sample.reshape(rows_s, D)
    zeros_state = _pack_state(jnp.zeros((B, H_A, DV_A, DK_A), F32), jnp.zeros((B, H_A, DK_A), F32), jnp.zeros((B, H_A), F32))
    outs_p, outs_s = [], []
    yp = ys = None
    for l in range(depth):
        final = l == depth - 1
        g1, g2, gf = g_norm1[l][None, :], g_norm2[l][None, :], g_final[None, :]
        gh = g_head_a[l].reshape(1, W_A)
        wp = jax.scipy.linalg.block_diag(*[w_pool[l, g] for g in range(len(POOL_WINDOWS))]).astype(BF16)
        sp = s_pool[l][None, :]
        wo, wu, wd = w_out[l].astype(BF16), w_up[l].astype(BF16), w_down[l].astype(BF16)
        modl = mod[l].reshape(-1, 6, D)
        mod_p = jnp.transpose(modl[0:B], (1, 0, 2))[:, :, None, :]
        mod_s = jnp.transpose(jnp.repeat(modl[B:B + SB], ST, axis=0), (1, 0, 2))[:, None]

        (qk, va, oa, u, gc, q_aug, kta, vta, kt, vt, gt) = _inproj_prompt(xp, mod_p, g1, _inproj_weights(w_in[l], b_in[l], True), B, T)
        mab, st1, m1, hist1 = _sequence(qk, va, oa, u, gc, gt, zeros_state[0], zeros_state[1], jnp.zeros((B, 16, W_B), F32),
                                        gh, wp, sp, B, T // CHUNK, 0)
        hc = _attn_prompt(q_aug, kta, vta, B, T)
        res = _mlp(xp, mab, hc, mod_p, g2, gf, wo, wu, wd, min(512, T), T, final)
        xp = res[0]
        if final:
            yp = res[1]
        C1, n1, mm1 = _unpack_state(st1, m1)
        to_bthd = lambda a: jnp.transpose(a.reshape(B, H_C, DH_C, T), (0, 3, 1, 2))
        outs_p.append((to_bthd(kt), to_bthd(vt), jnp.transpose(gt[:, 0:H_C, :], (0, 2, 1)), C1, n1, mm1, hist1[:, 1:]))

        (qk, va, oa, u, gc, qc, kc, vc, kt, vt, gt) = _inproj_sample(xs, mod_s, g1, _inproj_weights(w_in[l], b_in[l], False))

        def pad_rows(a, fill=0.0):
            a3 = a.reshape(SB, ST, a.shape[-1])
            return jnp.pad(a3, ((0, 0), (0, CHUNK - ST), (0, 0)), constant_values=fill).reshape(SB * CHUNK, a.shape[-1])

        lane = jnp.arange(LANES)[None, :]
        gc_pad = jnp.where((lane >= H_A) & (lane < 2 * H_A), pad_rows(gc, NEG), pad_rows(gc))
        gt3 = jnp.transpose(gt.reshape(16, SB, ST), (1, 0, 2))
        rr = jnp.arange(16)[None, :, None]
        gt_pad = jnp.where((rr >= 8) & (rr < 12), jnp.pad(gt3, ((0, 0), (0, 0), (0, CHUNK - ST)), constant_values=NEG),
                           jnp.pad(gt3, ((0, 0), (0, 0), (0, CHUNK - ST))))
        st0, m0 = _pack_state(state_C[l], state_n[l], state_m[l])
        hist0 = jnp.pad(state_pool[l], ((0, 0), (1, 0), (0, 0)))
        mab, st1, m1, _ = _sequence(pad_rows(qk), pad_rows(va), pad_rows(oa), pad_rows(u), gc_pad, gt_pad, st0, m0, hist0,
                                    gh, wp, sp, SB, 1, pos0_s)
        mab = mab.reshape(SB, CHUNK, MIX_AB)[:, 0:ST].reshape(rows_s, MIX_AB)
        q8 = jnp.pad(qc.reshape(SB, ST, W_C), ((0, 0), (0, 8 - ST), (0, 0)))
        tpad = lambda a: jnp.pad(jnp.transpose(a.reshape(a.shape[0], SB, ST), (1, 0, 2)), ((0, 0), (0, 0), (0, LANES - ST)))
        knt, vnt = tpad(kt), tpad(vt)
        lfn = jnp.where(rr < H_C, jnp.pad(gt3, ((0, 0), (0, 0), (0, LANES - ST))), 0.0)[:, 0:8]
        hc8 = _attn_sample(page_table, q8, knt, vnt, lfn, ckt, cvt, clf, l, n_phys)
        hc = hc8[:, 0:ST].reshape(rows_s, W_C).astype(BF16)
        res = _mlp(xs, mab, hc, mod_s, g2, gf, wo, wu, wd, rows_s, rows_s, final)
        xs = res[0]
        if final:
            ys = res[1]
        C1, n1, mm1 = _unpack_state(st1, m1)
        lf_s = jnp.transpose(gt3[:, 0:H_C, :], (0, 2, 1))
        pool_s = jnp.concatenate([state_pool[l], u.reshape(SB, ST, W_B)], axis=1)[:, -POOL_HIST:]
        outs_s.append((kc.reshape(SB, ST, H_C, DH_C), vc.reshape(SB, ST, H_C, DH_C), lf_s, C1, n1, mm1, pool_s))

    sp_ = [jnp.stack(a) for a in zip(*outs_p)]
    ss_ = [jnp.stack(a) for a in zip(*outs_s)]
    return (yp.reshape(B, T, D), ys.reshape(SB, ST, D), *sp_, *ss_)
```

```python
import functools

import numpy as np
import jax
import jax.numpy as jnp
from jax import lax
from jax.experimental import pallas as pl
from jax.experimental.pallas import tpu as pltpu

F32 = jnp.float32
BF16 = jnp.bfloat16

D_MODEL = 1024
H_A = 4
W_A = 384
DV_A = 96
DK_A = 48
W_B = 256
CG_B = 64
POOL_HIST = 15
POOL_WINDOWS = (2, 4, 8, 16)
DH_C = 64
W_C = 384
H_C = 6
D_FF = 4096
EPS = 1e-6
CHUNK = 128
LOG2E = 1.4426950408889634
NEG = -1e30

LANES = 128
QK_PAD = 256
ST_COLS = W_A + LANES
MIX_AB = W_A + W_B
AUG_ONE0 = DH_C
AUG_F0 = DH_C + 8
VMEM_LIMIT = 56 * 1024 * 1024
SAMPLE_Q = 4
SAMPLE_PAGES_PER_STEP = 16


def _dot(a, b):
    return jnp.dot(a, b, preferred_element_type=F32)


def _dot_nt(a, b):
    return lax.dot_general(a, b, (((1,), (1,)), ((), ())), preferred_element_type=F32)


def _dot_tn(a, b):
    return lax.dot_general(a, b, (((0,), (0,)), ((), ())), preferred_element_type=F32)


def _split3(x):
    a = x.astype(BF16)
    r = x - a.astype(F32)
    b = r.astype(BF16)
    c = (r - b.astype(F32)).astype(BF16)
    return a, b, c


def _dot3_rhs(m01, x):
    a, b, c = _split3(x)
    return _dot(m01, a) + _dot(m01, b) + _dot(m01, c)


def _dot3_lhs(x, m01):
    a, b, c = _split3(x)
    return _dot(a, m01) + _dot(b, m01) + _dot(c, m01)


def _log_sigmoid(x):
    return jnp.minimum(x, 0.0) - jnp.log1p(jnp.exp(-jnp.abs(x)))


def _sigmoid(x):
    return 1.0 / (1.0 + jnp.exp(-x))


def _iota(shape, dim):
    return lax.broadcasted_iota(jnp.int32, shape, dim)


def _expand_heads(src, width, per_head):
    lane = _iota((1, width), 1)
    out = jnp.zeros(src.shape[:-1] + (width,), F32)
    for h in range(H_A):
        out = jnp.where((lane >= h * per_head) & (lane < (h + 1) * per_head), src[:, h:h + 1], out)
    return out


def _mod_kernel(c_ref, w_ref, b_ref, o_ref):
    c = c_ref[...]
    s = (c * _sigmoid(c)).astype(BF16)
    o_ref[0] = _dot(s, w_ref[0].astype(BF16)) + b_ref[0]


def _modulation(c_all, w_ada, b_ada):
    depth, d, n6 = w_ada.shape
    rows = c_all.shape[0]
    tn = 1536
    return pl.pallas_call(
        _mod_kernel,
        grid=(depth, n6 // tn),
        in_specs=[
            pl.BlockSpec((rows, d), lambda l, j: (0, 0)),
            pl.BlockSpec((1, d, tn), lambda l, j: (l, 0, j)),
            pl.BlockSpec((1, 1, tn), lambda l, j: (l, 0, j)),
        ],
        out_specs=pl.BlockSpec((1, rows, tn), lambda l, j: (l, 0, j)),
        out_shape=jax.ShapeDtypeStruct((depth, rows, n6), F32),
        compiler_params=pltpu.CompilerParams(dimension_semantics=("arbitrary", "arbitrary"), vmem_limit_bytes=VMEM_LIMIT),
        name="modulation",
    )(c_all, w_ada, b_ada.reshape(depth, 1, n6))


NC_QA, NC_KA, NC_VA, NC_OA, NC_U, NC_G = 0, 256, 512, 896, 1280, 1536
NC_QC = 1664
N_NORMAL_PROMPT = NC_QC + H_C * LANES
N_NORMAL_SAMPLE = NC_QC + 3 * W_C
TR_K, TR_V, TR_G = 0, 384, 768
N_TRANS = 784


def _norm_mod(x, g, scale, shift):
    ms = jnp.mean(x * x, axis=-1, keepdims=True)
    return (x * lax.rsqrt(ms + EPS) * g) * (1.0 + scale) + shift


def _gate_rows(graw):
    r = _iota(graw.shape, 0)
    ls = _log_sigmoid(graw)
    return jnp.where((r < H_C) | (r >= 12), ls, jnp.where(r < 8, 0.0, graw))


def _gate_cols(graw):
    ln = _iota(graw.shape, 1)
    return jnp.where(ln < H_A, _log_sigmoid(graw), jnp.where(ln < 2 * H_A, graw, 0.0))


def _inproj_prompt_kernel(x_ref, mod_ref, g_ref, wn_ref, bn_ref, sn_ref, wt_ref, bt_ref, tri_ref,
                          qk_ref, va_ref, oa_ref, u_ref, gc_ref, qaug_ref, kta_ref, vta_ref, kt_ref, vt_ref, gt_ref,
                          carry_ref, *, tiles_per_seq):
    i = pl.program_id(0)
    tm = x_ref.shape[0]
    h = _norm_mod(x_ref[...], g_ref[...], mod_ref[1, 0], mod_ref[0, 0])
    hb = h.astype(BF16)

    def seg(off, width):
        return (_dot(hb, wn_ref[:, off:off + width]) + bn_ref[:, off:off + width]) * sn_ref[:, off:off + width]

    qk_ref[...] = seg(NC_QA, 2 * QK_PAD).astype(BF16)
    va_ref[...] = seg(NC_VA, W_A).astype(BF16)
    oa_ref[...] = seg(NC_OA, W_A)
    u_ref[...] = seg(NC_U, W_B)
    gc_ref[...] = _gate_cols(seg(NC_G, LANES))

    def tseg(off, rows):
        return _dot_nt(wt_ref[off:off + rows, :], hb) + bt_ref[off:off + rows, :]

    kt = tseg(TR_K, W_C)
    vt = tseg(TR_V, W_C)
    gt = _gate_rows(tseg(TR_G, 16))
    kt_ref[0] = kt
    vt_ref[0] = vt
    gt_ref[0] = gt

    first = (i % tiles_per_seq) == 0
    carry = jnp.where(first, 0.0, carry_ref[:, 0:1])
    ft = _dot3_lhs(gt[0:8] * LOG2E, tri_ref[...]) + carry
    carry_ref[...] = jnp.broadcast_to(ft[:, tm - 1:tm], carry_ref.shape)
    f1, f2, f3 = [p.astype(F32) for p in _split3(ft)]

    r8 = _iota((8, tm), 0)
    xaug = jnp.concatenate([jnp.zeros((AUG_ONE0, tm), F32), jnp.where(r8 < 3, 1.0, 0.0), f1, f2, f3,
                            jnp.zeros((LANES - AUG_F0 - 24, tm), F32)], axis=0)
    e = xaug.T
    r64 = _iota((LANES - DH_C, tm), 0)
    for hh in range(H_C):
        qaug_ref[hh] = (seg(NC_QC + hh * LANES, LANES) + e).astype(BF16)
        ka = jnp.where(r64 == 0, -f1[hh:hh + 1], jnp.where(r64 == 1, -f2[hh:hh + 1], jnp.where(r64 == 2, -f3[hh:hh + 1], 0.0)))
        ka = jnp.where((r64 == 8 + hh) | (r64 == 16 + hh) | (r64 == 24 + hh), 1.0, ka)
        kta_ref[0, hh] = jnp.concatenate([kt[hh * DH_C:(hh + 1) * DH_C].astype(BF16), ka.astype(BF16)], axis=0)
        vta_ref[0, hh] = jnp.concatenate([vt[hh * DH_C:(hh + 1) * DH_C].astype(BF16),
                                          jnp.where(r64 == 0, 1.0, 0.0).astype(BF16)], axis=0)


def _inproj_sample_kernel(x_ref, mod_ref, g_ref, wn_ref, bn_ref, sn_ref, wt_ref, bt_ref,
                          qk_ref, va_ref, oa_ref, u_ref, gc_ref, qc_ref, kc_ref, vc_ref, kt_ref, vt_ref, gt_ref):
    h = _norm_mod(x_ref[...], g_ref[...], mod_ref[1, 0], mod_ref[0, 0])
    hb = h.astype(BF16)

    def seg(off, width):
        return (_dot(hb, wn_ref[:, off:off + width]) + bn_ref[:, off:off + width]) * sn_ref[:, off:off + width]

    qk_ref[...] = seg(NC_QA, 2 * QK_PAD).astype(BF16)
    va_ref[...] = seg(NC_VA, W_A).astype(BF16)
    oa_ref[...] = seg(NC_OA, W_A)
    u_ref[...] = seg(NC_U, W_B)
    gc_ref[...] = _gate_cols(seg(NC_G, LANES))
    qc_ref[...] = seg(NC_QC, W_C)
    kc_ref[...] = seg(NC_QC + W_C, W_C)
    vc_ref[...] = seg(NC_QC + 2 * W_C, W_C)

    def tseg(off, rows):
        return _dot_nt(wt_ref[off:off + rows, :], hb) + bt_ref[off:off + rows, :]

    kt_ref[...] = tseg(TR_K, W_C)
    vt_ref[...] = tseg(TR_V, W_C)
    gt_ref[...] = _gate_rows(tseg(TR_G, 16))


def _inproj_weights(w_in, b_in, prompt):
    o = np.cumsum((0, H_A * DK_A, H_A * DK_A, W_A, H_A, H_A, W_A, W_B, W_C, W_C, W_C, H_C))
    col = lambda k: (w_in[:, o[k]:o[k + 1]], b_in[o[k]:o[k + 1]])
    (wqa, bqa), (wka, bka), (wva, bva), (wia, bia), (wfa, bfa), (woa, boa), (wu, bu), (wqc, bqc), (wkc, bkc), (wvc, bvc), (wfc, bfc) = [col(k) for k in range(11)]
    d = w_in.shape[0]
    zw = lambda n: jnp.zeros((d, n), F32)
    zb = lambda n: jnp.zeros((n,), F32)
    qscale = DH_C ** -0.5 * LOG2E
    ws = [wqa, zw(QK_PAD - 192), wka, zw(QK_PAD - 192), wva, woa, wu, wfa, wia, zw(LANES - 8)]
    bs = [bqa, zb(QK_PAD - 192), bka, zb(QK_PAD - 192), bva, boa, bu, bfa, bia, zb(LANES - 8)]
    ss = [jnp.ones((QK_PAD,), F32), jnp.full((QK_PAD,), DK_A ** -0.5, F32), jnp.ones((W_A + W_A + W_B + LANES,), F32)]
    if prompt:
        for hh in range(H_C):
            ws += [wqc[:, hh * DH_C:(hh + 1) * DH_C], zw(LANES - DH_C)]
            bs += [bqc[hh * DH_C:(hh + 1) * DH_C], zb(LANES - DH_C)]
        ss += [jnp.full((H_C * LANES,), qscale, F32)]
    else:
        ws += [wqc, wkc, wvc]
        bs += [bqc, bkc, bvc]
        ss += [jnp.full((W_C,), qscale, F32), jnp.ones((2 * W_C,), F32)]
    wn = jnp.concatenate(ws, axis=1).astype(BF16)
    bn = jnp.concatenate(bs)[None, :]
    sn = jnp.concatenate(ss)[None, :]
    wt = jnp.concatenate([wkc, wvc, wfc, zw(2), wia, wfa], axis=1).T.astype(BF16)
    bt = jnp.concatenate([bkc, bvc, bfc, zb(2), bia, bfa])[:, None]
    return wn, bn, sn, wt, bt


def _inproj_prompt(x2, mod4, g1, weights, batch, seq):
    wn, bn, sn, wt, bt = weights
    rows, d = x2.shape
    tm = min(512, seq)
    tps = seq // tm
    tri = jnp.asarray(np.triu(np.ones((tm, tm), np.float32)), BF16)
    full = lambda a: pl.BlockSpec(a.shape, lambda i: (0,) * a.ndim)
    rowblk = lambda w: pl.BlockSpec((tm, w), lambda i: (i, 0))
    tblk = lambda r: pl.BlockSpec((1, r, tm), lambda i: (i // tps, 0, i % tps))
    tblk4 = pl.BlockSpec((1, H_C, LANES, tm), lambda i: (i // tps, 0, 0, i % tps))
    outs = pl.pallas_call(
        functools.partial(_inproj_prompt_kernel, tiles_per_seq=tps),
        grid=(rows // tm,),
        in_specs=[rowblk(d), pl.BlockSpec((6, 1, 1, d), lambda i: (0, i // tps, 0, 0)), full(g1),
                  full(wn), full(bn), full(sn), full(wt), full(bt), full(tri)],
        out_specs=[rowblk(2 * QK_PAD), rowblk(W_A), rowblk(W_A), rowblk(W_B), rowblk(LANES),
                   pl.BlockSpec((H_C, tm, LANES), lambda i: (0, i, 0)), tblk4, tblk4, tblk(W_C), tblk(W_C), tblk(16)],
        out_shape=[jax.ShapeDtypeStruct((rows, 2 * QK_PAD), BF16), jax.ShapeDtypeStruct((rows, W_A), BF16),
                   jax.ShapeDtypeStruct((rows, W_A), F32), jax.ShapeDtypeStruct((rows, W_B), F32),
                   jax.ShapeDtypeStruct((rows, LANES), F32),
                   jax.ShapeDtypeStruct((H_C, rows, LANES), BF16),
                   jax.ShapeDtypeStruct((batch, H_C, LANES, seq), BF16), jax.ShapeDtypeStruct((batch, H_C, LANES, seq), BF16),
                   jax.ShapeDtypeStruct((batch, W_C, seq), F32), jax.ShapeDtypeStruct((batch, W_C, seq), F32),
                   jax.ShapeDtypeStruct((batch, 16, seq), F32)],
        scratch_shapes=[pltpu.VMEM((8, LANES), F32)],
        compiler_params=pltpu.CompilerParams(dimension_semantics=("arbitrary",), vmem_limit_bytes=VMEM_LIMIT),
        name="inproj_prompt",
    )(x2, mod4, g1, wn, bn, sn, wt, bt, tri)
    return outs


def _inproj_sample(x2, mod4, g1, weights):
    wn, bn, sn, wt, bt = weights
    rows, d = x2.shape
    full = lambda a: pl.BlockSpec(a.shape, lambda i: (0,) * a.ndim)
    o2 = lambda r, c: pl.BlockSpec((r, c), lambda i: (0, 0))
    sds = lambda r, c, t: jax.ShapeDtypeStruct((r, c), t)
    return pl.pallas_call(
        _inproj_sample_kernel,
        grid=(1,),
        in_specs=[full(x2), full(mod4), full(g1), full(wn), full(bn), full(sn), full(wt), full(bt)],
        out_specs=[o2(rows, 2 * QK_PAD), o2(rows, W_A), o2(rows, W_A), o2(rows, W_B), o2(rows, LANES),
                   o2(rows, W_C), o2(rows, W_C), o2(rows, W_C), o2(W_C, rows), o2(W_C, rows), o2(16, rows)],
        out_shape=[sds(rows, 2 * QK_PAD, BF16), sds(rows, W_A, BF16), sds(rows, W_A, F32), sds(rows, W_B, F32), sds(rows, LANES, F32),
                   sds(rows, W_C, F32), sds(rows, W_C, F32), sds(rows, W_C, F32), sds(W_C, rows, F32), sds(W_C, rows, F32), sds(16, rows, F32)],
        compiler_params=pltpu.CompilerParams(dimension_semantics=("arbitrary",), vmem_limit_bytes=VMEM_LIMIT),
        name="inproj_sample",
    )(x2, mod4, g1, wn, bn, sn, wt, bt)


def _state_mask():
    r = np.arange(QK_PAD)[:, None]
    c = np.arange(ST_COLS)[None, :]
    m = np.zeros((QK_PAD, ST_COLS), np.float32)
    for h in range(H_A):
        rows = (r >= h * DK_A) & (r < (h + 1) * DK_A)
        cols = ((c >= h * DV_A) & (c < (h + 1) * DV_A)) | (c == W_A + h)
        m[rows & cols] = 1.0
    return m


def _seq_kernel(qk_ref, v_ref, o_ref, u_ref, gc_ref, gt_ref, st0_ref, m0_ref, hist0_ref, gh_ref, wp_ref, sp_ref, mask_ref,
                mix_ref, st_out_ref, m_out_ref, hist_out_ref, st_scr, m_scr, z_scr, *, pos0):
    c = pl.program_id(1)
    L = CHUNK

    @pl.when(c == 0)
    def _():
        st_scr[...] = st0_ref[0]
        m_scr[...] = m0_ref[0]
        z_scr[0:16, :] = hist0_ref[0]

    q = qk_ref[:, 0:QK_PAD]
    k = qk_ref[:, QK_PAD:2 * QK_PAD]
    v = v_ref[...]
    gc = gc_ref[...]
    gt = gt_ref[0]
    st = st_scr[...]
    m_row = m_scr[0:1, :]

    row = _iota((L, L), 0)
    colm = _iota((L, L), 1)
    tri_l = jnp.where(colm <= row, 1.0, 0.0).astype(BF16)
    tri_u = jnp.where(row <= colm, 1.0, 0.0).astype(BF16)
    lane128 = _iota((1, LANES), 1)
    lane_q = _iota((1, QK_PAD), 1)
    lane_v = _iota((1, W_A), 1)

    bc_col = _dot3_rhs(tri_l, gc)
    bc_row = _dot3_lhs(gt[8:16], tri_u)
    inter = bc_col + m_row
    ia_col = pltpu.roll(gc, LANES - H_A, axis=1)

    causal = colm <= row
    mt_all = jnp.zeros((L, LANES), F32)
    s_list = []
    vblk = []
    for h in range(H_A):
        d = bc_col[:, h:h + 1] - bc_row[4 + h:5 + h, :] + gt[8 + h:9 + h, :]
        d = jnp.where(causal, d, NEG)
        mt = jnp.maximum(inter[:, h:h + 1], jnp.max(d, axis=-1, keepdims=True))
        qm = jnp.where((lane_q >= h * DK_A) & (lane_q < (h + 1) * DK_A), q, jnp.zeros_like(q))
        s = _dot_nt(qm, k) * jnp.exp(d - mt)
        s_list.append(s.astype(BF16))
        mt_all = jnp.where(lane128 == h, mt, mt_all)
        vm = jnp.where((lane_v >= h * DV_A) & (lane_v < (h + 1) * DV_A), v, jnp.zeros_like(v))
        vblk.append(jnp.concatenate([vm, jnp.broadcast_to(jnp.where(lane128 == h, 1.0, 0.0).astype(BF16), (L, LANES))], axis=1))
    s_cat = jnp.concatenate(s_list, axis=1)
    v_blk = jnp.concatenate(vblk, axis=0)
    pv = _dot(s_cat, v_blk)
    qs = _dot(q, st.astype(BF16))

    valid = lane128 < H_A
    a_all = jnp.where(valid, jnp.exp(inter - mt_all), 0.0)
    num = _expand_heads(a_all, W_A, DV_A) * qs[:, 0:W_A] + pv[:, 0:W_A]
    den = a_all * qs[:, W_A:] + pv[:, W_A:]
    inv = 1.0 / jnp.maximum(jnp.abs(den), jnp.exp(-mt_all))
    hs = num * _expand_heads(jnp.where(valid, inv, 0.0), W_A, DV_A)

    sq = hs * hs
    ss_all = jnp.zeros((L, LANES), F32)
    for h in range(H_A):
        ssum = jnp.sum(jnp.where((lane_v >= h * DV_A) & (lane_v < (h + 1) * DV_A), sq, 0.0), axis=-1, keepdims=True)
        ss_all = jnp.where(lane128 == h, ssum, ss_all)
    r_all = lax.rsqrt(ss_all * (1.0 / DV_A) + EPS)
    ha = _sigmoid(o_ref[...]) * hs * _expand_heads(r_all, W_A, DV_A) * gh_ref[...]

    m_new = mt_all[L - 1:L, :]
    g_col = jnp.where(valid, jnp.exp(bc_col[L - 1:L, :] - bc_col + ia_col - m_new), 0.0)
    a_last = jnp.where(valid, jnp.exp(inter[L - 1:L, :] - m_new), 0.0)
    gv = (v.astype(F32) * _expand_heads(g_col, W_A, DV_A)).astype(BF16)
    upd = _dot_tn(k, jnp.concatenate([gv, g_col.astype(BF16)], axis=1))
    a512 = jnp.concatenate([_expand_heads(a_last, W_A, DV_A), a_last], axis=1)
    st_new = a512 * st + mask_ref[...] * upd
    st_scr[...] = st_new
    m_scr[...] = jnp.broadcast_to(jnp.where(valid, m_new, 0.0), m_scr.shape)

    u = u_ref[...]
    z_scr[16:16 + L, :] = u
    z = z_scr[...]
    s2 = z + pltpu.roll(z, 1, axis=0)
    s4 = s2 + pltpu.roll(s2, 2, axis=0)
    s8 = s4 + pltpu.roll(s4, 4, axis=0)
    s16 = s8 + pltpu.roll(s8, 8, axis=0)
    navail = (pos0 + c * L + 1 + _iota((L, 1), 0)).astype(F32)
    lane_u = _iota((1, W_B), 1)
    y = None
    for g, (w, sw) in enumerate(zip(POOL_WINDOWS, (s2, s4, s8, s16))):
        yg = sw[16:16 + L, :] * (1.0 / jnp.minimum(float(w), navail))
        y = yg if y is None else jnp.where(lane_u >= g * CG_B, yg, y)
    y = y - u
    hb = _dot(y.astype(BF16), wp_ref[...]) * sp_ref[...]
    z_scr[0:16, :] = u[L - 16:L, :]
    mix_ref[...] = jnp.concatenate([ha, hb], axis=1).astype(BF16)

    @pl.when(c == pl.num_programs(1) - 1)
    def _():
        st_out_ref[0] = st_new
        m_out_ref[0] = m_scr[...]
        hist_out_ref[0] = u[L - 16:L, :]


def _sequence(qk, va, oa, u, gc, gt, st0, m0, hist0, gh, wp, sp, nb, nchunk, pos0):
    L = CHUNK
    mask = jnp.asarray(_state_mask())
    rowblk = lambda w: pl.BlockSpec((L, w), lambda b, c: (b * nchunk + c, 0))
    perb = lambda a: pl.BlockSpec((1,) + a.shape[1:], lambda b, c: (b,) + (0,) * (a.ndim - 1))
    full = lambda a: pl.BlockSpec(a.shape, lambda b, c: (0,) * a.ndim)
    rows = nb * nchunk * L
    return pl.pallas_call(
        functools.partial(_seq_kernel, pos0=pos0),
        grid=(nb, nchunk),
        in_specs=[rowblk(2 * QK_PAD), rowblk(W_A), rowblk(W_A), rowblk(W_B), rowblk(LANES),
                  pl.BlockSpec((1, 16, L), lambda b, c: (b, 0, c)),
                  perb(st0), perb(m0), perb(hist0), full(gh), full(wp), full(sp), full(mask)],
        out_specs=[rowblk(MIX_AB), pl.BlockSpec((1, QK_PAD, ST_COLS), lambda b, c: (b, 0, 0)),
                   pl.BlockSpec((1, 8, LANES), lambda b, c: (b, 0, 0)), pl.BlockSpec((1, 16, W_B), lambda b, c: (b, 0, 0))],
        out_shape=[jax.ShapeDtypeStruct((rows, MIX_AB), BF16), jax.ShapeDtypeStruct((nb, QK_PAD, ST_COLS), F32),
                   jax.ShapeDtypeStruct((nb, 8, LANES), F32), jax.ShapeDtypeStruct((nb, 16, W_B), F32)],
        scratch_shapes=[pltpu.VMEM((QK_PAD, ST_COLS), F32), pltpu.VMEM((8, LANES), F32), pltpu.VMEM((16 + L, W_B), F32)],
        compiler_params=pltpu.CompilerParams(dimension_semantics=("arbitrary", "arbitrary")),
        name="sequence",
    )(qk, va, oa, u, gc, gt, st0, m0, hist0, gh, wp, sp, mask)


def _pack_state(C, n, m):
    nb = C.shape[0]
    z = lambda r, c: jnp.zeros((nb, r, c), F32)
    blocks = []
    for h in range(H_A):
        blocks.append(jnp.concatenate([z(DK_A, h * DV_A), jnp.swapaxes(C[:, h], 1, 2), z(DK_A, W_A - (h + 1) * DV_A),
                                       z(DK_A, h), n[:, h][:, :, None], z(DK_A, LANES - h - 1)], axis=2))
    st = jnp.concatenate(blocks + [z(QK_PAD - H_A * DK_A, ST_COLS)], axis=1)
    mm = jnp.concatenate([jnp.broadcast_to(m[:, None, :], (nb, 8, H_A)), z(8, LANES - H_A)], axis=2)
    return st, mm


def _unpack_state(st, mm):
    C = jnp.stack([jnp.swapaxes(st[:, h * DK_A:(h + 1) * DK_A, h * DV_A:(h + 1) * DV_A], 1, 2) for h in range(H_A)], axis=1)
    n = jnp.stack([st[:, h * DK_A:(h + 1) * DK_A, W_A + h] for h in range(H_A)], axis=1)
    return C, n, mm[:, 0, 0:H_A]


def _attn_prompt_kernel(q_ref, kt_ref, vt_ref, o_ref, *, tq, tk):
    qi = pl.program_id(2)
    qs = (q_ref[0], q_ref[1])

    def step(j, carry, masked):
        start = pl.multiple_of(j * tk, tk)
        new = []
        for hh in range(2):
            m, acc = carry[2 * hh], carry[2 * hh + 1]
            ktb = kt_ref[0, hh, :, pl.ds(start, tk)]
            vtb = vt_ref[0, hh, :, pl.ds(start, tk)]
            s = _dot(qs[hh], ktb)
            if masked:
                rowg = qi * tq + _iota((tq, tk), 0)
                colg = j * tk + _iota((tq, tk), 1)
                s = jnp.where(colg <= rowg, s, NEG)
            m_new = jnp.maximum(m, jnp.max(s, axis=-1, keepdims=True))
            p = jnp.exp2(s - m_new)
            new += [m_new, jnp.exp2(m - m_new) * acc + _dot_nt(p.astype(BF16), vtb)]
        return tuple(new)

    jd = (qi * tq) // tk
    m0, a0 = jnp.full((tq, 1), NEG, F32), jnp.zeros((tq, LANES), F32)
    carry = lax.fori_loop(0, jd, lambda j, cr: step(j, cr, False), (m0, a0, m0, a0))
    carry = step(jd, carry, True)
    outs = [acc * (1.0 / acc[:, DH_C:DH_C + 1]) for acc in (carry[1], carry[3])]
    lane = _iota((1, LANES), 1)
    o_ref[...] = jnp.where(lane < DH_C, outs[0], pltpu.roll(outs[1], DH_C, axis=1)).astype(BF16)


def _attn_prompt(q_aug, kta, vta, batch, seq):
    rows = batch * seq
    tq = min(512, seq)
    tk = min(512, seq)
    nq = seq // tq
    return pl.pallas_call(
        functools.partial(_attn_prompt_kernel, tq=tq, tk=tk),
        grid=(batch, H_C // 2, nq),
        in_specs=[pl.BlockSpec((2, tq, LANES), lambda b, p, i: (p, b * nq + i, 0)),
                  pl.BlockSpec((1, 2, LANES, seq), lambda b, p, i: (b, p, 0, 0)),
                  pl.BlockSpec((1, 2, LANES, seq), lambda b, p, i: (b, p, 0, 0))],
        out_specs=pl.BlockSpec((tq, LANES), lambda b, p, i: (b * nq + i, p)),
        out_shape=jax.ShapeDtypeStruct((rows, W_C), BF16),
        compiler_params=pltpu.CompilerParams(dimension_semantics=("arbitrary", "arbitrary", "arbitrary"), vmem_limit_bytes=VMEM_LIMIT),
        name="attn_prompt",
    )(q_aug, kta, vta)


def _attn_sample_kernel(pt_ref, q_ref, knt_ref, vnt_ref, lfn_ref, *refs, n_pages_step):
    P = n_pages_step
    k_refs, v_refs, lf_refs = refs[0:P], refs[P:2 * P], refs[2 * P:3 * P]
    o_ref = refs[3 * P]
    m_scr, acc_scr, carry_scr, cq_scr = refs[3 * P + 1:]
    j = pl.program_id(1)
    R = 8 * SAMPLE_Q
    head_of_lane = lax.shift_right_logical(_iota((R, W_C), 1), 6)
    headmask = head_of_lane == (_iota((R, W_C), 0) & 7)
    lane = _iota((R, LANES), 1)
    rq = lax.shift_right_logical(_iota((R, LANES), 0), 3)

    def tile_q(x8):
        return jnp.concatenate([x8] * SAMPLE_Q, axis=0)

    q8 = q_ref[0]
    qrep = jnp.concatenate([jnp.broadcast_to(q8[qq:qq + 1], (8, W_C)) for qq in range(SAMPLE_Q)], axis=0)
    qbd = jnp.where(headmask, qrep, 0.0).astype(BF16)

    @pl.when(j == 0)
    def _():
        a = tile_q(lfn_ref[0] * LOG2E)
        cq = jnp.sum(jnp.where(lane <= rq, a, 0.0), axis=-1, keepdims=True)
        incl = jnp.where(_iota((LANES, LANES), 0) <= _iota((LANES, LANES), 1), 1.0, 0.0).astype(BF16)
        crow = _dot3_lhs(a, incl)
        s = _dot(qbd, knt_ref[0].astype(BF16)) + cq - crow
        s = jnp.where((lane <= rq) & (lane < SAMPLE_Q), s, NEG)
        m = jnp.max(s, axis=-1, keepdims=True)
        p = jnp.exp2(s - m)
        l = jnp.sum(p, axis=-1, keepdims=True)
        acc_scr[:, 0:W_C] = _dot_nt(p.astype(BF16), vnt_ref[0].astype(BF16))
        acc_scr[:, W_C:] = jnp.broadcast_to(l, (R, LANES))
        m_scr[...] = jnp.broadcast_to(m, (R, LANES))
        cq_scr[...] = jnp.broadcast_to(cq, (R, LANES))
        carry_scr[...] = jnp.zeros_like(carry_scr)

    cq = cq_scr[:, 0:1]
    strict = jnp.where(_iota((LANES, LANES), 0) > _iota((LANES, LANES), 1), 1.0, 0.0).astype(BF16)
    lf_all = jnp.concatenate([lf_refs[i][0] for i in range(P)], axis=0) * LOG2E
    suf = _dot3_lhs(lf_all, strict)
    carry = carry_scr[:, 0:1]
    bias = [None] * P
    for i in reversed(range(P)):
        bias[i] = tile_q(suf[8 * i:8 * i + 8] + carry)
        carry = carry + suf[8 * i:8 * i + 8, 0:1] + lf_all[8 * i:8 * i + 8, 0:1]
    carry_scr[...] = jnp.broadcast_to(carry, carry_scr.shape)

    kcat = jnp.concatenate([k_refs[i][0].astype(BF16) for i in range(P)], axis=1)
    vcat = jnp.concatenate([v_refs[i][0].astype(BF16) for i in range(P)], axis=1)
    s = _dot(qbd, kcat) + jnp.concatenate(bias, axis=1) + cq
    m_old = m_scr[:, 0:1]
    m_new = jnp.maximum(m_old, jnp.max(s, axis=-1, keepdims=True))
    alpha = jnp.exp2(m_old - m_new)
    p = jnp.exp2(s - m_new)
    l = alpha * acc_scr[:, W_C:W_C + 1] + jnp.sum(p, axis=-1, keepdims=True)
    acc = alpha * acc_scr[:, 0:W_C] + _dot_nt(p.astype(BF16), vcat)
    acc_scr[:, 0:W_C] = acc
    acc_scr[:, W_C:] = jnp.broadcast_to(l, (R, LANES))
    m_scr[...] = jnp.broadcast_to(m_new, (R, LANES))

    @pl.when(j == pl.num_programs(1) - 1)
    def _():
        o = jnp.where(headmask, acc * (1.0 / l), 0.0)
        rows = [jnp.sum(o[8 * qq:8 * qq + 8], axis=0, keepdims=True) for qq in range(SAMPLE_Q)]
        o_ref[0] = jnp.concatenate(rows + [jnp.zeros((8 - SAMPLE_Q, W_C), F32)], axis=0)


def _attn_sample(page_table, q8, knt, vnt, lfn, ckt, cvt, clf, layer, n_phys):
    nb, n_pages = page_table.shape
    P = min(SAMPLE_PAGES_PER_STEP, n_pages)
    nstep = n_pages // P
    base = layer * n_phys

    def page_map(i):
        return lambda b, j, pt: (base + pt[b * n_pages + (nstep - 1 - j) * P + i], 0, 0)

    perb = lambda a: pl.BlockSpec((1,) + a.shape[1:], lambda b, j, pt: (b, 0, 0))
    in_specs = [perb(q8), perb(knt), perb(vnt), perb(lfn)]
    in_specs += [pl.BlockSpec((1, W_C, LANES), page_map(i)) for i in range(P)]
    in_specs += [pl.BlockSpec((1, W_C, LANES), page_map(i)) for i in range(P)]
    in_specs += [pl.BlockSpec((1, 8, LANES), page_map(i)) for i in range(P)]
    R = 8 * SAMPLE_Q
    grid_spec = pltpu.PrefetchScalarGridSpec(
        num_scalar_prefetch=1, grid=(nb, nstep), in_specs=in_specs,
        out_specs=pl.BlockSpec((1, 8, W_C), lambda b, j, pt: (b, 0, 0)),
        scratch_shapes=[pltpu.VMEM((R, LANES), F32), pltpu.VMEM((R, W_C + LANES), F32),
                        pltpu.VMEM((8, LANES), F32), pltpu.VMEM((R, LANES), F32)])
    return pl.pallas_call(
        functools.partial(_attn_sample_kernel, n_pages_step=P),
        grid_spec=grid_spec,
        out_shape=jax.ShapeDtypeStruct((nb, 8, W_C), F32),
        compiler_params=pltpu.CompilerParams(dimension_semantics=("arbitrary", "arbitrary"), vmem_limit_bytes=VMEM_LIMIT),
        name="attn_sample",
    )(page_table.reshape(-1), q8, knt, vnt, lfn, *([ckt] * P), *([cvt] * P), *([clf] * P))


def _mlp_kernel(x_ref, mab_ref, hc_ref, mod_ref, g2_ref, gf_ref, wo_ref, wu_ref, wd_ref, *out_refs, final):
    x = x_ref[...]
    wo = wo_ref
    mix = _dot(mab_ref[...], wo[0:MIX_AB, :]) + _dot(hc_ref[...], wo[MIX_AB:, :])
    x1 = x + mod_ref[2, 0] * mix
    h2 = _norm_mod(x1, g2_ref[...], mod_ref[4, 0], mod_ref[3, 0]).astype(BF16)
    ff = jnp.zeros_like(x)
    fc = 1024
    for f in range(D_FF // fc):
        a = jnp.maximum(_dot(h2, wu_ref[:, f * fc:(f + 1) * fc]), 0.0)
        ff = ff + _dot((a * a).astype(BF16), wd_ref[f * fc:(f + 1) * fc, :])
    x2 = x1 + mod_ref[5, 0] * ff
    out_refs[0][...] = x2
    if final:
        ms = jnp.mean(x2 * x2, axis=-1, keepdims=True)
        out_refs[1][...] = x2 * lax.rsqrt(ms + EPS) * gf_ref[...]


def _mlp(x2, mab, hc, mod4, g2, gf, wo, wu, wd, tm, rows_per_mod, final):
    rows, d = x2.shape
    s = mod4.shape[2]
    const = lambda a: pl.BlockSpec(a.shape, lambda i: (0,) * a.ndim, pipeline_mode=pl.Buffered(1))
    rowblk = lambda w: pl.BlockSpec((tm, w), lambda i: (i, 0))
    n_out = 2 if final else 1
    outs = pl.pallas_call(
        functools.partial(_mlp_kernel, final=final),
        grid=(rows // tm,),
        in_specs=[rowblk(d), rowblk(MIX_AB), rowblk(W_C),
                  pl.BlockSpec((6, 1, s, d), lambda i: (0, (i * tm) // rows_per_mod, 0, 0)),
                  const(g2), const(gf), const(wo), const(wu), const(wd)],
        out_specs=[rowblk(d)] * n_out,
        out_shape=[jax.ShapeDtypeStruct((rows, d), F32)] * n_out,
        compiler_params=pltpu.CompilerParams(dimension_semantics=("arbitrary",), vmem_limit_bytes=VMEM_LIMIT),
        name="mlp",
    )(x2, mab, hc, mod4, g2, gf, wo, wu, wd)
    return outs


def kernel(x_prompt, x_sample, c_prompt, c_sample, cache_k, cache_v, cache_logf, page_table, state_C, state_n, state_m, state_pool, w_ada, b_ada, g_norm1, g_norm2, w_in, b_in, g_head_a, w_pool, s_pool, w_out, w_up, w_down, g_final):
    depth = w_ada.shape[0]
    B, T, D = x_prompt.shape
    SB, ST, _ = x_sample.shape
    n_phys, page = cache_k.shape[1], cache_k.shape[2]
    n_pages = page_table.shape[1]
    rows_p, rows_s = B * T, SB * ST
    pos0_s = n_pages * page
    assert T % CHUNK == 0 and ST == SAMPLE_Q and page == LANES

    nc = B + SB
    c_all = jnp.concatenate([c_prompt, c_sample, jnp.zeros((-nc % 8, D), F32)], axis=0)
    mod = _modulation(c_all, w_ada, b_ada)

    ckt = jnp.transpose(cache_k, (0, 1, 3, 4, 2)).reshape(depth * n_phys, W_C, page)
    cvt = jnp.transpose(cache_v, (0, 1, 3, 4, 2)).reshape(depth * n_phys, W_C, page)
    clf = jnp.pad(jnp.transpose(cache_logf, (0, 1, 3, 2)), ((0, 0), (0, 0), (0, 8 - H_C), (0, 0))).reshape(depth * n_phys, 8, page)

    xp = x_prompt.reshape(rows_p, D)
    xs = x_sample.reshape(rows_s, D)
    zeros_state = _pack_state(jnp.zeros((B, H_A, DV_A, DK_A), F32), jnp.zeros((B, H_A, DK_A), F32), jnp.zeros((B, H_A), F32))
    outs_p, outs_s = [], []
    yp = ys = None
    for l in range(depth):
        final = l == depth - 1
        g1, g2, gf = g_norm1[l][None, :], g_norm2[l][None, :], g_final[None, :]
        gh = g_head_a[l].reshape(1, W_A)
        wp = jax.scipy.linalg.block_diag(*[w_pool[l, g] for g in range(len(POOL_WINDOWS))]).astype(BF16)
        sp = s_pool[l][None, :]
        wo, wu, wd = w_out[l].astype(BF16), w_up[l].astype(BF16), w_down[l].astype(BF16)
        modl = mod[l].reshape(-1, 6, D)
        mod_p = jnp.transpose(modl[0:B], (1, 0, 2))[:, :, None, :]
        mod_s = jnp.transpose(jnp.repeat(modl[B:B + SB], ST, axis=0), (1, 0, 2))[:, None]

        (qk, va, oa, u, gc, q_aug, kta, vta, kt, vt, gt) = _inproj_prompt(xp, mod_p, g1, _inproj_weights(w_in[l], b_in[l], True), B, T)
        mab, st1, m1, hist1 = _sequence(qk, va, oa, u, gc, gt, zeros_state[0], zeros_state[1], jnp.zeros((B, 16, W_B), F32),
                                        gh, wp, sp, B, T // CHUNK, 0)
        hc = _attn_prompt(q_aug, kta, vta, B, T)
        res = _mlp(xp, mab, hc, mod_p, g2, gf, wo, wu, wd, min(512, T), T, final)
        xp = res[0]
        if final:
            yp = res[1]
        C1, n1, mm1 = _unpack_state(st1, m1)
        to_bthd = lambda a: jnp.transpose(a.reshape(B, H_C, DH_C, T), (0, 3, 1, 2))
        outs_p.append((to_bthd(kt), to_bthd(vt), jnp.transpose(gt[:, 0:H_C, :], (0, 2, 1)), C1, n1, mm1, hist1[:, 1:]))

        (qk, va, oa, u, gc, qc, kc, vc, kt, vt, gt) = _inproj_sample(xs, mod_s, g1, _inproj_weights(w_in[l], b_in[l], False))

        def pad_rows(a, fill=0.0):
            a3 = a.reshape(SB, ST, a.shape[-1])
            return jnp.pad(a3, ((0, 0), (0, CHUNK - ST), (0, 0)), constant_values=fill).reshape(SB * CHUNK, a.shape[-1])

        lane = jnp.arange(LANES)[None, :]
        gc_pad = jnp.where((lane >= H_A) & (lane < 2 * H_A), pad_rows(gc, NEG), pad_rows(gc))
        gt3 = jnp.transpose(gt.reshape(16, SB, ST), (1, 0, 2))
        rr = jnp.arange(16)[None, :, None]
        gt_pad = jnp.where((rr >= 8) & (rr < 12), jnp.pad(gt3, ((0, 0), (0, 0), (0, CHUNK - ST)), constant_values=NEG),
                           jnp.pad(gt3, ((0, 0), (0, 0), (0, CHUNK - ST))))
        st0, m0 = _pack_state(state_C[l], state_n[l], state_m[l])
        hist0 = jnp.pad(state_pool[l], ((0, 0), (1, 0), (0, 0)))
        mab, st1, m1, _ = _sequence(pad_rows(qk), pad_rows(va), pad_rows(oa), pad_rows(u), gc_pad, gt_pad, st0, m0, hist0,
                                    gh, wp, sp, SB, 1, pos0_s)
        mab = mab.reshape(SB, CHUNK, MIX_AB)[:, 0:ST].reshape(rows_s, MIX_AB)
        q8 = jnp.pad(qc.reshape(SB, ST, W_C), ((0, 0), (0, 8 - ST), (0, 0)))
        tpad = lambda a: jnp.pad(jnp.transpose(a.reshape(a.shape[0], SB, ST), (1, 0, 2)), ((0, 0), (0, 0), (0, LANES - ST)))
        knt, vnt = tpad(kt), tpad(vt)
        lfn = jnp.where(rr < H_C, jnp.pad(gt3, ((0, 0), (0, 0), (0, LANES - ST))), 0.0)[:, 0:8]
        hc8 = _attn_sample(page_table, q8, knt, vnt, lfn, ckt, cvt, clf, l, n_phys)
        hc = hc8[:, 0:ST].reshape(rows_s, W_C).astype(BF16)
        res = _mlp(xs, mab, hc, mod_s, g2, gf, wo, wu, wd, rows_s, rows_s, final)
        xs = res[0]
        if final:
            ys = res[1]
        C1, n1, mm1 = _unpack_state(st1, m1)
        lf_s = jnp.transpose(gt3[:, 0:H_C, :], (0, 2, 1))
        pool_s = jnp.concatenate([state_pool[l], u.reshape(SB, ST, W_B)], axis=1)[:, -POOL_HIST:]
        outs_s.append((kc.reshape(SB, ST, H_C, DH_C), vc.reshape(SB, ST, H_C, DH_C), lf_s, C1, n1, mm1, pool_s))

    sp_ = [jnp.stack(a) for a in zip(*outs_p)]
    ss_ = [jnp.stack(a) for a in zip(*outs_s)]
    return (yp.reshape(B, T, D), ys.reshape(SB, ST, D), *sp_, *ss_)
```

```python
import functools

import numpy as np
import jax
import jax.numpy as jnp
from jax import lax
from jax.experimental import pallas as pl
from jax.experimental.pallas import tpu as pltpu

F32 = jnp.float32
BF16 = jnp.bfloat16

D_MODEL = 1024
H_A = 4
W_A = 384
DV_A = 96
DK_A = 48
W_B = 256
CG_B = 64
POOL_HIST = 15
POOL_WINDOWS = (2, 4, 8, 16)
DH_C = 64
W_C = 384
H_C = 6
D_FF = 4096
EPS = 1e-6
CHUNK = 128
LOG2E = 1.4426950408889634
NEG = -1e30

LANES = 128
QK_PAD = 256
ST_COLS = W_A + LANES
MIX_AB = W_A + W_B
AUG_ONE0 = DH_C
AUG_F0 = DH_C + 8
VMEM_LIMIT = 56 * 1024 * 1024
SEQ_GROUP = 2
ATTN_STRIP = 256
SAMPLE_Q = 4
SAMPLE_PAGES_PER_STEP = 16


def _dot(a, b):
    return jnp.dot(a, b, preferred_element_type=F32)


def _dot_nt(a, b):
    return lax.dot_general(a, b, (((1,), (1,)), ((), ())), preferred_element_type=F32)


def _dot_tn(a, b):
    return lax.dot_general(a, b, (((0,), (0,)), ((), ())), preferred_element_type=F32)


def _split3(x):
    a = x.astype(BF16)
    r = x - a.astype(F32)
    b = r.astype(BF16)
    c = (r - b.astype(F32)).astype(BF16)
    return a, b, c


def _dot3_rhs(m01, x):
    a, b, c = _split3(x)
    return _dot(m01, a) + _dot(m01, b) + _dot(m01, c)


def _dot3_lhs(x, m01):
    a, b, c = _split3(x)
    return _dot(a, m01) + _dot(b, m01) + _dot(c, m01)


def _log_sigmoid(x):
    return jnp.minimum(x, 0.0) - jnp.log1p(jnp.exp(-jnp.abs(x)))


def _sigmoid(x):
    return 1.0 / (1.0 + jnp.exp(-x))


def _iota(shape, dim):
    return lax.broadcasted_iota(jnp.int32, shape, dim)


def _expand_heads(src, width, per_head):
    lane = _iota((1, width), 1)
    out = jnp.zeros(src.shape[:-1] + (width,), F32)
    for h in range(H_A):
        out = jnp.where((lane >= h * per_head) & (lane < (h + 1) * per_head), src[:, h:h + 1], out)
    return out


def _mod_kernel(c_ref, w_ref, b_ref, o_ref):
    c = c_ref[...]
    s = (c * _sigmoid(c)).astype(BF16)
    o_ref[0] = _dot(s, w_ref[0].astype(BF16)) + b_ref[0]


def _modulation(c_all, w_ada, b_ada):
    depth, d, n6 = w_ada.shape
    rows = c_all.shape[0]
    tn = 1536
    return pl.pallas_call(
        _mod_kernel,
        grid=(depth, n6 // tn),
        in_specs=[
            pl.BlockSpec((rows, d), lambda l, j: (0, 0)),
            pl.BlockSpec((1, d, tn), lambda l, j: (l, 0, j)),
            pl.BlockSpec((1, 1, tn), lambda l, j: (l, 0, j)),
        ],
        out_specs=pl.BlockSpec((1, rows, tn), lambda l, j: (l, 0, j)),
        out_shape=jax.ShapeDtypeStruct((depth, rows, n6), F32),
        compiler_params=pltpu.CompilerParams(dimension_semantics=("arbitrary", "arbitrary"), vmem_limit_bytes=VMEM_LIMIT),
        name="modulation",
    )(c_all, w_ada, b_ada.reshape(depth, 1, n6))


NC_QA, NC_KA, NC_VA, NC_OA, NC_U, NC_G = 0, 256, 512, 896, 1280, 1536
N_TOKEN_MAJOR = 1664
TR_K, TR_V, TR_G, TR_Q = 0, 384, 768, 784
N_FEATURE_MAJOR = TR_Q + W_C
N_W_ROWS = N_TOKEN_MAJOR + N_FEATURE_MAJOR


def _norm_mod(x, g, scale, shift):
    ms = jnp.mean(x * x, axis=-1, keepdims=True)
    return (x * lax.rsqrt(ms + EPS) * g) * (1.0 + scale) + shift


def _gate_rows(graw):
    r = _iota(graw.shape, 0)
    ls = _log_sigmoid(graw)
    return jnp.where((r < H_C) | (r >= 12), ls, jnp.where(r < 8, 0.0, graw))


def _gate_cols(graw):
    ln = _iota(graw.shape, 1)
    return jnp.where(ln < H_A, _log_sigmoid(graw), jnp.where(ln < 2 * H_A, graw, 0.0))


def _inproj_common(x_ref, mod_ref, g_ref, w_ref, bn_ref, sn_ref, bt_ref, st_ref, qk_ref, va_ref, oa_ref, u_ref, gc_ref):
    h = _norm_mod(x_ref[...], g_ref[...], mod_ref[1, 0], mod_ref[0, 0])
    hb = h.astype(BF16)

    def seg(off, width):
        return (_dot_nt(hb, w_ref[off:off + width, :]) + bn_ref[:, off:off + width]) * sn_ref[:, off:off + width]

    def tseg(off, rows):
        return (_dot_nt(w_ref[N_TOKEN_MAJOR + off:N_TOKEN_MAJOR + off + rows, :], hb) + bt_ref[off:off + rows, :]) * st_ref[off:off + rows, :]

    qk_ref[...] = seg(NC_QA, 2 * QK_PAD).astype(BF16)
    va_ref[...] = seg(NC_VA, W_A).astype(BF16)
    oa_ref[...] = seg(NC_OA, W_A)
    u_ref[...] = seg(NC_U, W_B)
    gc_ref[...] = _gate_cols(seg(NC_G, LANES))
    return seg, tseg


def _inproj_prompt_kernel(x_ref, mod_ref, g_ref, w_ref, bn_ref, sn_ref, bt_ref, st_ref, tri_ref, *rest, tiles_per_seq, n_alias):
    (qk_ref, va_ref, oa_ref, u_ref, gc_ref, kaug_ref, qta_ref, vta_ref, kt_ref, vt_ref, gt_ref, carry_ref) = rest[n_alias:]
    i = pl.program_id(0)
    tm = x_ref.shape[0]
    seg, tseg = _inproj_common(x_ref, mod_ref, g_ref, w_ref, bn_ref, sn_ref, bt_ref, st_ref, qk_ref, va_ref, oa_ref, u_ref, gc_ref)

    kt = tseg(TR_K, W_C)
    vt = tseg(TR_V, W_C)
    gt = _gate_rows(tseg(TR_G, 16))
    qt = tseg(TR_Q, W_C)
    kt_ref[0, 0] = kt
    vt_ref[0, 0] = vt
    gt_ref[0] = gt

    first = (i % tiles_per_seq) == 0
    carry = jnp.where(first, 0.0, carry_ref[:, 0:1])
    ft = _dot3_lhs(gt[0:8] * LOG2E, tri_ref[...]) + carry
    carry_ref[...] = jnp.broadcast_to(ft[:, tm - 1:tm], carry_ref.shape)
    f1, f2, f3 = [p.astype(F32) for p in _split3(ft)]

    r8 = _iota((8, tm), 0)
    ktail = jnp.concatenate([jnp.where(r8 < 3, 1.0, 0.0), -f1, -f2, -f3,
                             jnp.zeros((LANES - AUG_F0 - 24, tm), F32)], axis=0)
    r64 = _iota((LANES - DH_C, tm), 0)
    for hh in range(H_C):
        kaug_ref[hh] = jnp.concatenate([kt[hh * DH_C:(hh + 1) * DH_C], ktail], axis=0).T.astype(BF16)
        qa = jnp.where(r64 == 0, f1[hh:hh + 1], jnp.where(r64 == 1, f2[hh:hh + 1], jnp.where(r64 == 2, f3[hh:hh + 1], 0.0)))
        qa = jnp.where((r64 == 8 + hh) | (r64 == 16 + hh) | (r64 == 24 + hh), 1.0, qa)
        qta_ref[0, hh] = jnp.concatenate([qt[hh * DH_C:(hh + 1) * DH_C].astype(BF16), qa.astype(BF16)], axis=0)
        vta_ref[0, hh] = jnp.concatenate([vt[hh * DH_C:(hh + 1) * DH_C].astype(BF16),
                                          jnp.where(r64 == 0, 1.0, 0.0).astype(BF16)], axis=0)


def _inproj_sample_kernel(x_ref, mod_ref, g_ref, w_ref, bn_ref, sn_ref, bt_ref, st_ref,
                          qk_ref, va_ref, oa_ref, u_ref, gc_ref, qc_ref, kc_ref, vc_ref, kt_ref, vt_ref, gt_ref):
    _, tseg = _inproj_common(x_ref, mod_ref, g_ref, w_ref, bn_ref, sn_ref, bt_ref, st_ref, qk_ref, va_ref, oa_ref, u_ref, gc_ref)
    kt = tseg(TR_K, W_C)
    vt = tseg(TR_V, W_C)
    kt_ref[...] = kt
    vt_ref[...] = vt
    gt_ref[...] = _gate_rows(tseg(TR_G, 16))
    qc_ref[...] = tseg(TR_Q, W_C).T
    kc_ref[...] = kt.T
    vc_ref[...] = vt.T


def _inproj_weights(w_in, b_in):
    o = np.cumsum((0, H_A * DK_A, H_A * DK_A, W_A, H_A, H_A, W_A, W_B, W_C, W_C, W_C, H_C))
    wt_full = w_in.T
    d = w_in.shape[0]
    rows = lambda k: (wt_full[o[k]:o[k + 1]], b_in[o[k]:o[k + 1]])
    (wqa, bqa), (wka, bka), (wva, bva), (wia, bia), (wfa, bfa), (woa, boa), (wu, bu), (wqc, bqc), (wkc, bkc), (wvc, bvc), (wfc, bfc) = [rows(k) for k in range(11)]
    zw = lambda n: jnp.zeros((n, d), F32)
    zb = lambda n: jnp.zeros((n,), F32)
    ws = [wqa, zw(QK_PAD - 192), wka, zw(QK_PAD - 192), wva, woa, wu, wfa, wia, zw(LANES - 8)]
    bs = [bqa, zb(QK_PAD - 192), bka, zb(QK_PAD - 192), bva, boa, bu, bfa, bia, zb(LANES - 8)]
    sn =jnp.concatenate([jnp.ones((QK_PAD,), F32), jnp.full((QK_PAD,), DK_A ** -0.5, F32),
                          jnp.ones((N_TOKEN_MAJOR - 2 * QK_PAD,), F32)])[None, :]
    bn = jnp.concatenate(bs)[None, :]
    ws += [wkc, wvc, wfc, zw(2), wia, wfa, wqc]
    bt = jnp.concatenate([bkc, bvc, bfc, zb(2), bia, bfa, bqc])[:, None]
    st = jnp.concatenate([jnp.ones((TR_Q,), F32), jnp.full((W_C,), DH_C ** -0.5 * LOG2E, F32)])[:, None]
    return jnp.concatenate(ws, axis=0).astype(BF16), bn, sn, bt, st


def _inproj_prompt(x2, mod4, g1, weights, batch, seq, layer, depth, kv_all):
    w, bn, sn, bt, st = weights
    rows, d = x2.shape
    tm = min(512, seq)
    tps = seq // tm
    tri = jnp.asarray(np.triu(np.ones((tm, tm), np.float32)), BF16)
    full = lambda a: pl.BlockSpec(a.shape, lambda i: (0,) * a.ndim)
    rowblk = lambda wd: pl.BlockSpec((tm, wd), lambda i: (i, 0))
    tblk = lambda r: pl.BlockSpec((1, r, tm), lambda i: (i // tps, 0, i % tps))
    tblk4 = pl.BlockSpec((1, H_C, LANES, tm), lambda i: (i // tps, 0, 0, i % tps))
    kvblk = pl.BlockSpec((1, 1, W_C, tm), lambda i: (layer, i // tps, 0, i % tps))
    n_in = 9
    alias_in = [] if kv_all is None else list(kv_all)
    alias_specs = [pl.BlockSpec(memory_space=pl.ANY)] * len(alias_in)
    aliases = {} if kv_all is None else {n_in: 8, n_in + 1: 9}
    outs = pl.pallas_call(
        functools.partial(_inproj_prompt_kernel, tiles_per_seq=tps, n_alias=len(alias_in)),
        grid=(rows // tm,),
        in_specs=[rowblk(d), pl.BlockSpec((6, 1, 1, d), lambda i: (0, i // tps, 0, 0)), full(g1),
                  full(w), full(bn), full(sn), full(bt), full(st), full(tri)] + alias_specs,
        out_specs=[rowblk(2 * QK_PAD), rowblk(W_A), rowblk(W_A), rowblk(W_B), rowblk(LANES),
                   pl.BlockSpec((H_C, tm, LANES), lambda i: (0, i, 0)), tblk4, tblk4, kvblk, kvblk, tblk(16)],
        out_shape=[jax.ShapeDtypeStruct((rows, 2 * QK_PAD), BF16), jax.ShapeDtypeStruct((rows, W_A), BF16),
                   jax.ShapeDtypeStruct((rows, W_A), F32), jax.ShapeDtypeStruct((rows, W_B), F32),
                   jax.ShapeDtypeStruct((rows, LANES), F32),
                   jax.ShapeDtypeStruct((H_C, rows, LANES), BF16),
                   jax.ShapeDtypeStruct((batch, H_C, LANES, seq), BF16), jax.ShapeDtypeStruct((batch, H_C, LANES, seq), BF16),
                   jax.ShapeDtypeStruct((depth, batch, W_C, seq), F32), jax.ShapeDtypeStruct((depth, batch, W_C, seq), F32),
                   jax.ShapeDtypeStruct((batch, 16, seq), F32)],
        scratch_shapes=[pltpu.VMEM((8, LANES), F32)],
        input_output_aliases=aliases,
        compiler_params=pltpu.CompilerParams(dimension_semantics=("arbitrary",), vmem_limit_bytes=VMEM_LIMIT),
        name="inproj_prompt",
    )(x2, mod4, g1, w, bn, sn, bt, st, tri, *alias_in)
    return outs


def _inproj_sample(x2, mod4, g1, weights):
    w, bn, sn, bt, st = weights
    rows, d = x2.shape
    full = lambda a: pl.BlockSpec(a.shape, lambda i: (0,) * a.ndim)
    o2 = lambda r, c: pl.BlockSpec((r, c), lambda i: (0, 0))
    sds = lambda r, c, t: jax.ShapeDtypeStruct((r, c), t)
    return pl.pallas_call(
        _inproj_sample_kernel,
        grid=(1,),
        in_specs=[full(x2), full(mod4), full(g1), full(w), full(bn), full(sn), full(bt), full(st)],
        out_specs=[o2(rows, 2 * QK_PAD), o2(rows, W_A), o2(rows, W_A), o2(rows, W_B), o2(rows, LANES),
                   o2(rows, W_C), o2(rows, W_C), o2(rows, W_C), o2(W_C, rows), o2(W_C, rows), o2(16, rows)],
        out_shape=[sds(rows, 2 * QK_PAD, BF16), sds(rows, W_A, BF16), sds(rows, W_A, F32), sds(rows, W_B, F32), sds(rows, LANES, F32),
                   sds(rows, W_C, F32), sds(rows, W_C, F32), sds(rows, W_C, F32), sds(W_C, rows, F32), sds(W_C, rows, F32), sds(16, rows, F32)],
        compiler_params=pltpu.CompilerParams(dimension_semantics=("arbitrary",), vmem_limit_bytes=VMEM_LIMIT),
        name="inproj_sample",
    )(x2, mod4, g1, w, bn, sn, bt, st)


def _state_mask():
    r = np.arange(QK_PAD)[:, None]
    c = np.arange(ST_COLS)[None, :]
    m = np.zeros((QK_PAD, ST_COLS), np.float32)
    for h in range(H_A):
        rows = (r >= h * DK_A) & (r < (h + 1) * DK_A)
        cols = ((c >= h * DV_A) & (c < (h + 1) * DV_A)) | (c == W_A + h)
        m[rows & cols] = 1.0
    return m


def _seq_kernel(*refs, pos0, group, interleave):
    chains = [_seq_one(g, *refs, pos0=pos0) for g in range(group)]
    if not interleave:
        for ch in chains:
            for _ in ch:
                pass
        return
    for lead in range(group - 1):
        for ch in chains[:group - 1 - lead]:
            next(ch)
    while chains:
        chains = [ch for ch in chains if next(ch, _DONE) is not _DONE]


_DONE = object()


def _seq_one(g, qk_ref, v_ref, o_ref, u_ref, gc_ref, gt_ref, st0_ref, m0_ref, hist0_ref, gh_ref, wp_ref, sp_ref, mask_ref,
             mix_ref, st_out_ref, m_out_ref, hist_out_ref, st_scr, m_scr, z_scr, *, pos0):
    c = pl.program_id(1)
    L = CHUNK

    @pl.when(c == 0)
    def _():
        st_scr[g] = st0_ref[g]
        m_scr[g] = m0_ref[g]
        z_scr[g, 0:16, :] = hist0_ref[g]

    q = qk_ref[g, :, 0:QK_PAD]
    k = qk_ref[g, :, QK_PAD:2 * QK_PAD]
    v = v_ref[g]
    gc = gc_ref[g]
    gt = gt_ref[g]
    st = st_scr[g]
    m_row = m_scr[g, 0:1, :]

    row = _iota((L, L), 0)
    colm = _iota((L, L), 1)
    tri_l = jnp.where(colm <= row, 1.0, 0.0).astype(BF16)
    tri_u = jnp.where(row <= colm, 1.0, 0.0).astype(BF16)
    lane128 = _iota((1, LANES), 1)
    lane_q = _iota((1, QK_PAD), 1)
    lane_v = _iota((1, W_A), 1)

    u = u_ref[g]
    z_scr[g, 16:16 + L, :] = u
    z = z_scr[g]
    s2 = z + pltpu.roll(z, 1, axis=0)
    s4 = s2 + pltpu.roll(s2, 2, axis=0)
    s8 = s4 + pltpu.roll(s4, 4, axis=0)
    s16 = s8 + pltpu.roll(s8, 8, axis=0)
    navail = (pos0 + c * L + 1 + _iota((L, 1), 0)).astype(F32)
    lane_u = _iota((1, W_B), 1)
    y = None
    for gi, (w, sw) in enumerate(zip(POOL_WINDOWS, (s2, s4, s8, s16))):
        yg = sw[16:16 + L, :] * (1.0 / jnp.minimum(float(w), navail))
        y = yg if y is None else jnp.where(lane_u >= gi * CG_B, yg, y)
    y = y - u
    hb = _dot(y.astype(BF16), wp_ref[...]) * sp_ref[...]
    z_scr[g, 0:16, :] = u[L - 16:L, :]

    bc_col = _dot3_rhs(tri_l, gc)
    bc_row = _dot3_lhs(gt[8:16], tri_u)
    qk_h = [_dot_nt(jnp.where((lane_q >= h * DK_A) & (lane_q < (h + 1) * DK_A), q, jnp.zeros_like(q)), k) for h in range(H_A)]
    qs = _dot(q, st.astype(BF16))
    yield
    inter = bc_col + m_row
    ia_col = pltpu.roll(gc, LANES - H_A, axis=1)

    causal = colm <= row
    mt_all = jnp.zeros((L, LANES), F32)
    s_list = []
    vblk = []
    for h in range(H_A):
        d = bc_col[:, h:h + 1] - bc_row[4 + h:5 + h, :] + gt[8 + h:9 + h, :]
        d = jnp.where(causal, d, NEG)
        mt = jnp.maximum(inter[:, h:h + 1], jnp.max(d, axis=-1, keepdims=True))
        s = qk_h[h] * jnp.exp(d - mt)
        s_list.append(s.astype(BF16))
        mt_all = jnp.where(lane128 == h, mt, mt_all)
        vm = jnp.where((lane_v >= h * DV_A) & (lane_v < (h + 1) * DV_A), v, jnp.zeros_like(v))
        vblk.append(jnp.concatenate([vm, jnp.broadcast_to(jnp.where(lane128 == h, 1.0, 0.0).astype(BF16), (L, LANES))], axis=1))
    s_cat = jnp.concatenate(s_list, axis=1)
    v_blk = jnp.concatenate(vblk, axis=0)
    pv = _dot(s_cat, v_blk)
    yield

    valid = lane128 < H_A
    m_new = mt_all[L - 1:L, :]
    g_col = jnp.where(valid, jnp.exp(bc_col[L - 1:L, :] - bc_col + ia_col - m_new), 0.0)
    a_last = jnp.where(valid, jnp.exp(inter[L - 1:L, :] - m_new), 0.0)
    gv = (v.astype(F32) * _expand_heads(g_col, W_A, DV_A)).astype(BF16)
    upd = _dot_tn(k, jnp.concatenate([gv, g_col.astype(BF16)], axis=1))
    yield
    a512 = jnp.concatenate([_expand_heads(a_last, W_A, DV_A), a_last], axis=1)
    st_new = a512 * st + mask_ref[...] * upd
    st_scr[g] = st_new
    m_rows = jnp.broadcast_to(jnp.where(valid, m_new, 0.0), (8, LANES))
    m_scr[g] = m_rows

    a_all = jnp.where(valid, jnp.exp(inter - mt_all), 0.0)
    num = _expand_heads(a_all, W_A, DV_A) * qs[:, 0:W_A] + pv[:, 0:W_A]
    den = a_all * qs[:, W_A:] + pv[:, W_A:]
    inv = 1.0 / jnp.maximum(jnp.abs(den), jnp.exp(-mt_all))
    hs = num * _expand_heads(jnp.where(valid, inv, 0.0), W_A, DV_A)

    sq = hs * hs
    ss_all = jnp.zeros((L, LANES), F32)
    for h in range(H_A):
        ssum = jnp.sum(jnp.where((lane_v >= h * DV_A) & (lane_v < (h + 1) * DV_A), sq, 0.0), axis=-1, keepdims=True)
        ss_all = jnp.where(lane128 == h, ssum, ss_all)
    r_all = lax.rsqrt(ss_all * (1.0 / DV_A) + EPS)
    ha = _sigmoid(o_ref[g]) * hs * _expand_heads(r_all, W_A, DV_A) * gh_ref[...]
    mix_ref[g] = jnp.concatenate([ha, hb], axis=1).astype(BF16)

    @pl.when(c == pl.num_programs(1) - 1)
    def _():
        st_out_ref[g] = st_new
        m_out_ref[g] = m_rows
        hist_out_ref[g] = u[L - 16:L, :]


def _sequence(qk, va, oa, u, gc, gt, st0, m0, hist0, gh, wp, sp, nb, nchunk, pos0):
    L = CHUNK
    G = SEQ_GROUP
    assert nb % G == 0
    mask = jnp.asarray(_state_mask())
    seq3 = lambda a: a.reshape(nb, nchunk * L, a.shape[-1])
    rowblk = lambda w: pl.BlockSpec((G, L, w), lambda b, c: (b, c, 0))
    perb = lambda a: pl.BlockSpec((G,) + a.shape[1:], lambda b, c: (b,) + (0,) * (a.ndim - 1))
    full = lambda a: pl.BlockSpec(a.shape, lambda b, c: (0,) * a.ndim)
    outs = pl.pallas_call(
        functools.partial(_seq_kernel, pos0=pos0, group=G, interleave=nchunk == 1),
        grid=(nb // G, nchunk),
        in_specs=[rowblk(2 * QK_PAD), rowblk(W_A), rowblk(W_A), rowblk(W_B), rowblk(LANES),
                  pl.BlockSpec((G, 16, L), lambda b, c: (b, 0, c)),
                  perb(st0), perb(m0), perb(hist0), full(gh), full(wp), full(sp), full(mask)],
        out_specs=[rowblk(MIX_AB), pl.BlockSpec((G, QK_PAD, ST_COLS), lambda b, c: (b, 0, 0)),
                   pl.BlockSpec((G, 8, LANES), lambda b, c: (b, 0, 0)), pl.BlockSpec((G, 16, W_B), lambda b, c: (b, 0, 0))],
        out_shape=[jax.ShapeDtypeStruct((nb, nchunk * L, MIX_AB), BF16), jax.ShapeDtypeStruct((nb, QK_PAD, ST_COLS), F32),
                   jax.ShapeDtypeStruct((nb, 8, LANES), F32), jax.ShapeDtypeStruct((nb, 16, W_B), F32)],
        scratch_shapes=[pltpu.VMEM((G, QK_PAD, ST_COLS), F32), pltpu.VMEM((G, 8, LANES), F32), pltpu.VMEM((G, 16 + L, W_B), F32)],
        compiler_params=pltpu.CompilerParams(dimension_semantics=("arbitrary", "arbitrary"), vmem_limit_bytes=VMEM_LIMIT),
        name="sequence",
    )(seq3(qk), seq3(va), seq3(oa), seq3(u), seq3(gc), gt, st0, m0, hist0, gh, wp, sp, mask)
    return (outs[0].reshape(nb * nchunk * L, MIX_AB),) + tuple(outs[1:])


def _pack_state(C, n, m):
    nb = C.shape[0]
    z = lambda r, c: jnp.zeros((nb, r, c), F32)
    blocks = []
    for h in range(H_A):
        blocks.append(jnp.concatenate([z(DK_A, h * DV_A), jnp.swapaxes(C[:, h], 1, 2), z(DK_A, W_A - (h + 1) * DV_A),
                                       z(DK_A, h), n[:, h][:, :, None], z(DK_A, LANES - h - 1)], axis=2))
    st = jnp.concatenate(blocks + [z(QK_PAD - H_A * DK_A, ST_COLS)], axis=1)
    mm = jnp.concatenate([jnp.broadcast_to(m[:, None, :], (nb, 8, H_A)), z(8, LANES - H_A)], axis=2)
    return st, mm


def _unpack_state(st, mm):
    C = jnp.stack([jnp.swapaxes(st[:, h * DK_A:(h + 1) * DK_A, h * DV_A:(h + 1) * DV_A], 1, 2) for h in range(H_A)], axis=1)
    n = jnp.stack([st[:, h * DK_A:(h + 1) * DK_A, W_A + h] for h in range(H_A)], axis=1)
    return C, n, mm[:, 0, 0:H_A]


def _attn_prompt_kernel(qt_ref, k_ref, vt_ref, o_ref, s_scr, m_scr, acc_scr, *, tq, tk, sw):
    qi = pl.program_id(2)
    m_scr[...] = jnp.full(m_scr.shape, NEG, F32)
    acc_scr[...] = jnp.zeros(acc_scr.shape, F32)
    units = [(hh, st) for hh in range(2) for st in range(tq // sw)]

    def qk_block(j, slot, skip=()):
        start = pl.multiple_of(j * tk, tk)
        for hh, st in units:
            if st not in skip:
                s_scr[slot, hh, :, st * sw:(st + 1) * sw] = _dot(k_ref[hh, pl.ds(start, tk), :], qt_ref[0, hh, :, st * sw:(st + 1) * sw])

    def softmax_pv_block(j, slot, masked=(), skip=()):
        start = pl.multiple_of(j * tk, tk)
        for hh, st in units:
            if st in skip:
                continue
            strip = slice(st * sw, (st + 1) * sw)
            s = s_scr[slot, hh, :, strip]
            if st in masked:
                keyg = j * tk + _iota((tk, sw), 0)
                qryg = qi * tq + st * sw + _iota((tk, sw), 1)
                s = jnp.where(keyg <= qryg, s, NEG)
            m = m_scr[hh, 0:1, strip]
            m_new = jnp.maximum(m, jnp.max(s, axis=0, keepdims=True))
            p = jnp.exp2(s - m_new)
            pv = _dot(vt_ref[0, hh, :, pl.ds(start, tk)], p.astype(BF16))
            acc_scr[hh, :, strip] = jnp.exp2(m - m_new) * acc_scr[hh, :, strip] + pv
            m_scr[hh, 0:1, strip] = m_new

    n_full = (qi * tq) // tk
    qk_block(0, 0)

    def body(i, carry):
        j = 2 * i
        qk_block(j + 1, 1)
        softmax_pv_block(j, 0)
        qk_block(j + 2, 0)
        softmax_pv_block(j + 1, 1)
        return carry

    lax.fori_loop(0, n_full // 2, body, 0)
    qk_block(n_full + 1, 1, skip=(0,))
    softmax_pv_block(n_full, 0, masked=(0,))
    softmax_pv_block(n_full + 1, 1, masked=(1,), skip=(0,))
    outs = []
    for hh in range(2):
        acc = acc_scr[hh]
        outs.append((acc * (1.0 / acc[DH_C:DH_C + 1, :])).T)
    lane = _iota((1, LANES), 1)
    o_ref[...] = jnp.where(lane < DH_C, outs[0], pltpu.roll(outs[1], DH_C, axis=1)).astype(BF16)


def _attn_prompt(qta, k_aug, vta, batch, seq):
    rows = batch * seq
    tq, tk, sw = 2 * ATTN_STRIP, ATTN_STRIP, ATTN_STRIP
    assert seq % tq == 0
    nq = seq // tq
    return pl.pallas_call(
        functools.partial(_attn_prompt_kernel, tq=tq, tk=tk, sw=sw),
        grid=(batch, H_C // 2, nq),
        in_specs=[pl.BlockSpec((1, 2, LANES, tq), lambda b, p, i: (b, p, 0, i)),
                  pl.BlockSpec((2, seq, LANES), lambda b, p, i: (p, b, 0)),
                  pl.BlockSpec((1, 2, LANES, seq), lambda b, p, i: (b, p, 0, 0))],
        out_specs=pl.BlockSpec((tq, LANES), lambda b, p, i: (b * nq + i, p)),
        out_shape=jax.ShapeDtypeStruct((rows, W_C), BF16),
        scratch_shapes=[pltpu.VMEM((2, 2, tk, tq), F32), pltpu.VMEM((2, 8, tq), F32), pltpu.VMEM((2, LANES, tq), F32)],
        compiler_params=pltpu.CompilerParams(dimension_semantics=("arbitrary", "arbitrary", "arbitrary"), vmem_limit_bytes=VMEM_LIMIT),
        name="attn_prompt",
    )(qta, k_aug, vta)


def _attn_sample_kernel(pt_ref, q_ref, knt_ref, vnt_ref, lfn_ref, *refs, n_pages_step):
    P = n_pages_step
    k_refs, v_refs, lf_refs = refs[0:P], refs[P:2 * P], refs[2 * P:3 * P]
    o_ref = refs[3 * P]
    m_scr, acc_scr, carry_scr, cq_scr = refs[3 * P + 1:]
    j = pl.program_id(1)
    R = 8 * SAMPLE_Q
    head_of_lane = lax.shift_right_logical(_iota((R, W_C), 1), 6)
    headmask = head_of_lane == (_iota((R, W_C), 0) & 7)
    lane = _iota((R, LANES), 1)
    rq = lax.shift_right_logical(_iota((R, LANES), 0), 3)

    def tile_q(x8):
        return jnp.concatenate([x8] * SAMPLE_Q, axis=0)

    q8 = q_ref[0]
    qrep = jnp.concatenate([jnp.broadcast_to(q8[qq:qq + 1], (8, W_C)) for qq in range(SAMPLE_Q)], axis=0)
    qbd = jnp.where(headmask, qrep, 0.0).astype(BF16)

    @pl.when(j == 0)
    def _():
        a = tile_q(lfn_ref[0] * LOG2E)
        cq = jnp.sum(jnp.where(lane <= rq, a, 0.0), axis=-1, keepdims=True)
        incl = jnp.where(_iota((LANES, LANES), 0) <= _iota((LANES, LANES), 1), 1.0, 0.0).astype(BF16)
        crow = _dot3_lhs(a, incl)
        s = _dot(qbd, knt_ref[0].astype(BF16)) + cq - crow
        s = jnp.where((lane <= rq) & (lane < SAMPLE_Q), s, NEG)
        m = jnp.max(s, axis=-1, keepdims=True)
        p = jnp.exp2(s - m)
        l = jnp.sum(p, axis=-1, keepdims=True)
        acc_scr[:, 0:W_C] = _dot_nt(p.astype(BF16), vnt_ref[0].astype(BF16))
        acc_scr[:, W_C:] = jnp.broadcast_to(l, (R, LANES))
        m_scr[...] = jnp.broadcast_to(m, (R, LANES))
        cq_scr[...] = jnp.broadcast_to(cq, (R, LANES))
        carry_scr[...] = jnp.zeros_like(carry_scr)

    cq = cq_scr[:, 0:1]
    strict = jnp.where(_iota((LANES, LANES), 0) > _iota((LANES, LANES), 1), 1.0, 0.0).astype(BF16)
    lf_all = jnp.concatenate([lf_refs[i][0] for i in range(P)], axis=0) * LOG2E
    suf = _dot3_lhs(lf_all, strict)
    carry = carry_scr[:, 0:1]
    bias = [None] * P
    for i in reversed(range(P)):
        bias[i] = tile_q(suf[8 * i:8 * i + 8] + carry)
        carry = carry + suf[8 * i:8 * i + 8, 0:1] + lf_all[8 * i:8 * i + 8, 0:1]
    carry_scr[...] = jnp.broadcast_to(carry, carry_scr.shape)

    kcat = jnp.concatenate([k_refs[i][0].astype(BF16) for i in range(P)], axis=1)
    vcat = jnp.concatenate([v_refs[i][0].astype(BF16) for i in range(P)], axis=1)
    s = _dot(qbd, kcat) + jnp.concatenate(bias, axis=1) + cq
    m_old = m_scr[:, 0:1]
    m_new = jnp.maximum(m_old, jnp.max(s, axis=-1, keepdims=True))
    alpha = jnp.exp2(m_old - m_new)
    p = jnp.exp2(s - m_new)
    l = alpha * acc_scr[:, W_C:W_C + 1] + jnp.sum(p, axis=-1, keepdims=True)
    acc = alpha * acc_scr[:, 0:W_C] + _dot_nt(p.astype(BF16), vcat)
    acc_scr[:, 0:W_C] = acc
    acc_scr[:, W_C:] = jnp.broadcast_to(l, (R, LANES))
    m_scr[...] = jnp.broadcast_to(m_new, (R, LANES))

    @pl.when(j == pl.num_programs(1) - 1)
    def _():
        o = jnp.where(headmask, acc * (1.0 / l), 0.0)
        rows = [jnp.sum(o[8 * qq:8 * qq + 8], axis=0, keepdims=True) for qq in range(SAMPLE_Q)]
        o_ref[0] = jnp.concatenate(rows + [jnp.zeros((8 - SAMPLE_Q, W_C), F32)], axis=0)


def _attn_sample(page_table, q8, knt, vnt, lfn, ckt, cvt, clf, layer, n_phys):
    nb, n_pages = page_table.shape
    P = min(SAMPLE_PAGES_PER_STEP, n_pages)
    nstep = n_pages // P
    base = layer * n_phys

    def page_map(i):
        return lambda b, j, pt: (base + pt[b * n_pages + (nstep - 1 - j) * P + i], 0, 0)

    perb = lambda a: pl.BlockSpec((1,) + a.shape[1:], lambda b, j, pt: (b, 0, 0))
    in_specs = [perb(q8), perb(knt), perb(vnt), perb(lfn)]
    in_specs += [pl.BlockSpec((1, W_C, LANES), page_map(i)) for i in range(P)]
    in_specs += [pl.BlockSpec((1, W_C, LANES), page_map(i)) for i in range(P)]
    in_specs += [pl.BlockSpec((1, 8, LANES), page_map(i)) for i in range(P)]
    R = 8 * SAMPLE_Q
    grid_spec = pltpu.PrefetchScalarGridSpec(
        num_scalar_prefetch=1, grid=(nb, nstep), in_specs=in_specs,
        out_specs=pl.BlockSpec((1, 8, W_C), lambda b, j, pt: (b, 0, 0)),
        scratch_shapes=[pltpu.VMEM((R, LANES), F32), pltpu.VMEM((R, W_C + LANES), F32),
                        pltpu.VMEM((8, LANES), F32), pltpu.VMEM((R, LANES), F32)])
    return pl.pallas_call(
        functools.partial(_attn_sample_kernel, n_pages_step=P),
        grid_spec=grid_spec,
        out_shape=jax.ShapeDtypeStruct((nb, 8, W_C), F32),
        compiler_params=pltpu.CompilerParams(dimension_semantics=("arbitrary", "arbitrary"), vmem_limit_bytes=VMEM_LIMIT),
        name="attn_sample",
    )(page_table.reshape(-1), q8, knt, vnt, lfn, *([ckt] * P), *([cvt] * P), *([clf] * P))


def _mlp_kernel(x_ref, mab_ref, hc_ref, mod_ref, g2_ref, gf_ref, wo_ref, wu_ref, wd_ref, *out_refs, final):
    x = x_ref[...]
    mix = _dot(mab_ref[...], wo_ref[0, 0:MIX_AB, :]) + _dot(hc_ref[...], wo_ref[0, MIX_AB:, :])
    x1 = x + mod_ref[2, 0] * mix
    h2 = _norm_mod(x1, g2_ref[...], mod_ref[4, 0], mod_ref[3, 0]).astype(BF16)
    ff = jnp.zeros_like(x)
    fc = 1024
    for f in range(D_FF // fc):
        a = jnp.maximum(_dot(h2, wu_ref[0, :, f * fc:(f + 1) * fc]), 0.0)
        ff = ff + _dot((a * a).astype(BF16), wd_ref[0, f * fc:(f + 1) * fc, :])
    x2 = x1 + mod_ref[5, 0] * ff
    out_refs[0][...] = x2
    if final:
        ms = jnp.mean(x2 * x2, axis=-1, keepdims=True)
        out_refs[1][...] = x2 * lax.rsqrt(ms + EPS) * gf_ref[...]


def _mlp(x2, mab, hc, mod4, g2, gf, wo, wu, wd, layer, tm, rows_per_mod, final):
    rows, d = x2.shape
    s = mod4.shape[2]
    const = lambda a: pl.BlockSpec(a.shape, lambda i: (0,) * a.ndim, pipeline_mode=pl.Buffered(1))
    layerw = lambda a: pl.BlockSpec((1,) + a.shape[1:], lambda i: (layer, 0, 0), pipeline_mode=pl.Buffered(1))
    rowblk = lambda w: pl.BlockSpec((tm, w), lambda i: (i, 0))
    n_out = 2 if final else 1
    outs = pl.pallas_call(
        functools.partial(_mlp_kernel, final=final),
        grid=(rows // tm,),
        in_specs=[rowblk(d), rowblk(MIX_AB), rowblk(W_C),
                  pl.BlockSpec((6, 1, s, d), lambda i: (0, (i * tm) // rows_per_mod, 0, 0)),
                  const(g2), const(gf), layerw(wo), layerw(wu), layerw(wd)],
        out_specs=[rowblk(d)] * n_out,
        out_shape=[jax.ShapeDtypeStruct((rows, d), F32)] * n_out,
        compiler_params=pltpu.CompilerParams(dimension_semantics=("arbitrary",), vmem_limit_bytes=VMEM_LIMIT),
        name="mlp",
    )(x2, mab, hc, mod4, g2, gf, wo, wu, wd)
    return outs


def kernel(x_prompt, x_sample, c_prompt, c_sample, cache_k, cache_v, cache_logf, page_table, state_C, state_n, state_m, state_pool, w_ada, b_ada, g_norm1, g_norm2, w_in, b_in, g_head_a, w_pool, s_pool, w_out, w_up, w_down, g_final):
    depth = w_ada.shape[0]
    B, T, D = x_prompt.shape
    SB, ST, _ = x_sample.shape
    n_phys, page = cache_k.shape[1], cache_k.shape[2]
    n_pages = page_table.shape[1]
    rows_p, rows_s = B * T, SB * ST
    pos0_s = n_pages * page
    assert T % CHUNK == 0 and ST == SAMPLE_Q and page == LANES

    nc = B + SB
    c_all = jnp.concatenate([c_prompt, c_sample, jnp.zeros((-nc % 8, D), F32)], axis=0)
    mod = _modulation(c_all, w_ada, b_ada)

    ckt = jnp.transpose(cache_k, (0, 1, 3, 4, 2)).reshape(depth * n_phys, W_C, page)
    cvt = jnp.transpose(cache_v, (0, 1, 3, 4, 2)).reshape(depth * n_phys, W_C, page)
    clf = jnp.pad(jnp.transpose(cache_logf, (0, 1, 3, 2)), ((0, 0), (0, 0), (0, 8 - H_C), (0, 0))).reshape(depth * n_phys, 8, page)
    wo_all, wu_all, wd_all = w_out.astype(BF16), w_up.astype(BF16), w_down.astype(BF16)

    xp = x_prompt.reshape(rows_p, D)
    xs = x_sample.reshape(rows_s, D)
    zeros_state = (jnp.zeros((B, QK_PAD, ST_COLS), F32), jnp.zeros((B, 8, LANES), F32))
    outs_p, outs_s = [], []
    yp = ys = None
    kv_all = None
    for l in range(depth):
        final = l == depth - 1
        g1, g2, gf = g_norm1[l][None, :], g_norm2[l][None, :], g_final[None, :]
        gh = g_head_a[l].reshape(1, W_A)
        wp = jax.scipy.linalg.block_diag(*[w_pool[l, g] for g in range(len(POOL_WINDOWS))]).astype(BF16)
        sp = s_pool[l][None, :]
        weights = _inproj_weights(w_in[l], b_in[l])
        modl = mod[l].reshape(-1, 6, D)
        mod_p = jnp.transpose(modl[0:B], (1, 0, 2))[:, :, None, :]
        mod_s = jnp.transpose(jnp.repeat(modl[B:B + SB], ST, axis=0), (1, 0, 2))[:, None]

        (qk, va, oa, u, gc, k_aug, qta, vta, kt_all, vt_all, gt) = _inproj_prompt(xp, mod_p, g1, weights, B, T, l, depth, kv_all)
        kv_all = (kt_all, vt_all)
        mab, st1, m1, hist1 = _sequence(qk, va, oa, u, gc, gt, zeros_state[0], zeros_state[1], jnp.zeros((B, 16, W_B), F32),
                                        gh, wp, sp, B, T // CHUNK, 0)
        hc = _attn_prompt(qta, k_aug, vta, B, T)
        res = _mlp(xp, mab, hc, mod_p, g2, gf, wo_all, wu_all, wd_all, l, min(512, T), T, final)
        xp = res[0]
        if final:
            yp = res[1]
        C1, n1, mm1 = _unpack_state(st1, m1)
        outs_p.append((jnp.transpose(gt[:, 0:H_C, :], (0, 2, 1)), C1, n1, mm1, hist1[:, 1:]))

        (qk, va, oa, u, gc, qc, kc, vc, kt, vt, gt) = _inproj_sample(xs, mod_s, g1, weights)

        def pad_rows(a, fill=0.0):
            a3 = a.reshape(SB, ST, a.shape[-1])
            return jnp.pad(a3, ((0, 0), (0, CHUNK - ST), (0, 0)), constant_values=fill).reshape(SB * CHUNK, a.shape[-1])

        lane = jnp.arange(LANES)[None, :]
        gc_pad = jnp.where((lane >= H_A) & (lane < 2 * H_A), pad_rows(gc, NEG), pad_rows(gc))
        gt3 = jnp.transpose(gt.reshape(16, SB, ST), (1, 0, 2))
        rr = jnp.arange(16)[None, :, None]
        gt_pad = jnp.where((rr >= 8) & (rr < 12), jnp.pad(gt3, ((0, 0), (0, 0), (0, CHUNK - ST)), constant_values=NEG),
                           jnp.pad(gt3, ((0, 0), (0, 0), (0, CHUNK - ST))))
        st0, m0 = _pack_state(state_C[l], state_n[l], state_m[l])
        hist0 = jnp.pad(state_pool[l], ((0, 0), (1, 0), (0, 0)))
        mab, st1, m1, _ = _sequence(pad_rows(qk), pad_rows(va), pad_rows(oa), pad_rows(u), gc_pad, gt_pad, st0, m0, hist0,
                                    gh, wp, sp, SB, 1, pos0_s)
        mab = mab.reshape(SB, CHUNK, MIX_AB)[:, 0:ST].reshape(rows_s, MIX_AB)
        q8 = jnp.pad(qc.reshape(SB, ST, W_C), ((0, 0), (0, 8 - ST), (0, 0)))
        tpad = lambda a: jnp.pad(jnp.transpose(a.reshape(a.shape[0], SB, ST), (1, 0, 2)), ((0, 0), (0, 0), (0, LANES - ST)))
        knt, vnt = tpad(kt), tpad(vt)
        lfn = jnp.where(rr < H_C, jnp.pad(gt3, ((0, 0), (0, 0), (0, LANES - ST))), 0.0)[:, 0:8]
        hc8 = _attn_sample(page_table, q8, knt, vnt, lfn, ckt, cvt, clf, l, n_phys)
        hc = hc8[:, 0:ST].reshape(rows_s, W_C).astype(BF16)
        res = _mlp(xs, mab, hc, mod_s, g2, gf, wo_all, wu_all, wd_all, l, rows_s, rows_s, final)
        xs = res[0]
        if final:
            ys = res[1]
        C1, n1, mm1 = _unpack_state(st1, m1)
        lf_s = jnp.transpose(gt3[:, 0:H_C, :], (0, 2, 1))
        pool_s = jnp.concatenate([state_pool[l], u.reshape(SB, ST, W_B)], axis=1)[:, -POOL_HIST:]
        outs_s.append((kc.reshape(SB, ST, H_C, DH_C), vc.reshape(SB, ST, H_C, DH_C), lf_s, C1, n1, mm1, pool_s))

    to_bthd = lambda a: jnp.transpose(a.reshape(depth, B, H_C, DH_C, T), (0, 1, 4, 2, 3))
    sp_ = [jnp.stack(a) for a in zip(*outs_p)]
    ss_ = [jnp.stack(a) for a in zip(*outs_s)]
    return (yp.reshape(B, T, D), ys.reshape(SB, ST, D), to_bthd(kv_all[0]), to_bthd(kv_all[1]), *sp_, *ss_)
```

```python
import functools

import numpy as np
import jax
import jax.numpy as jnp
from jax import lax
from jax.experimental import pallas as pl
from jax.experimental.pallas import tpu as pltpu

F32 = jnp.float32
BF16 = jnp.bfloat16

D_MODEL = 1024
H_A = 4
W_A = 384
DV_A = 96
DK_A = 48
W_B = 256
CG_B = 64
POOL_HIST = 15
POOL_WINDOWS = (2, 4, 8, 16)
DH_C = 64
W_C = 384
H_C = 6
D_FF = 4096
EPS = 1e-6
CHUNK = 128
LOG2E = 1.4426950408889634
NEG = -1e30

LANES = 128
QK_PAD = 256
ST_COLS = W_A + LANES
MIX_AB = W_A + W_B
AUG_ONE0 = DH_C
AUG_F0 = DH_C + 8
VMEM_LIMIT = 56 * 1024 * 1024
SEQ_GROUP = 2
ATTN_Q_TILE = 1024
ATTN_STRIP = 256
SAMPLE_Q = 4
SAMPLE_PAGES_PER_STEP = 32


def _dot(a, b):
    return jnp.dot(a, b, preferred_element_type=F32)


def _dot_nt(a, b):
    return lax.dot_general(a, b, (((1,), (1,)), ((), ())), preferred_element_type=F32)


def _dot_tn(a, b):
    return lax.dot_general(a, b, (((0,), (0,)), ((), ())), preferred_element_type=F32)


def _split3(x):
    a = x.astype(BF16)
    r = x - a.astype(F32)
    b = r.astype(BF16)
    c = (r - b.astype(F32)).astype(BF16)
    return a, b, c


def _dot3_rhs(m01, x):
    a, b, c = _split3(x)
    return _dot(m01, a) + _dot(m01, b) + _dot(m01, c)


def _dot3_lhs(x, m01):
    a, b, c = _split3(x)
    return _dot(a, m01) + _dot(b, m01) + _dot(c, m01)


def _log_sigmoid(x):
    return jnp.minimum(x, 0.0) - jnp.log1p(jnp.exp(-jnp.abs(x)))


def _sigmoid(x):
    return 1.0 / (1.0 + jnp.exp(-x))


def _iota(shape, dim):
    return lax.broadcasted_iota(jnp.int32, shape, dim)


def _expand_heads(src, width, per_head):
    lane = _iota((1, width), 1)
    out = jnp.zeros(src.shape[:-1] + (width,), F32)
    for h in range(H_A):
        out = jnp.where((lane >= h * per_head) & (lane < (h + 1) * per_head), src[:, h:h + 1], out)
    return out


def _mod_kernel(c_ref, w_ref, b_ref, o_ref):
    c = c_ref[...]
    s = (c * _sigmoid(c)).astype(BF16)
    o_ref[0] = _dot(s, w_ref[0].astype(BF16)) + b_ref[0]


def _modulation(c_all, w_ada, b_ada):
    depth, d, n6 = w_ada.shape
    rows = c_all.shape[0]
    tn = 1536
    return pl.pallas_call(
        _mod_kernel,
        grid=(depth, n6 // tn),
        in_specs=[
            pl.BlockSpec((rows, d), lambda l, j: (0, 0)),
            pl.BlockSpec((1, d, tn), lambda l, j: (l, 0, j)),
            pl.BlockSpec((1, 1, tn), lambda l, j: (l, 0, j)),
        ],
        out_specs=pl.BlockSpec((1, rows, tn), lambda l, j: (l, 0, j)),
        out_shape=jax.ShapeDtypeStruct((depth, rows, n6), F32),
        compiler_params=pltpu.CompilerParams(dimension_semantics=("arbitrary", "arbitrary"), vmem_limit_bytes=VMEM_LIMIT),
        name="modulation",
    )(c_all, w_ada, b_ada.reshape(depth, 1, n6))


NC_QA, NC_KA, NC_VA, NC_OA, NC_U, NC_G = 0, 256, 512, 896, 1280, 1536
N_TOKEN_MAJOR = 1664
TR_K, TR_V, TR_G, TR_Q = 0, 384, 768, 784
N_FEATURE_MAJOR = TR_Q + W_C
N_W_ROWS = N_TOKEN_MAJOR + N_FEATURE_MAJOR


def _norm_mod(x, g, scale, shift):
    ms = jnp.mean(x * x, axis=-1, keepdims=True)
    return (x * lax.rsqrt(ms + EPS) * g) * (1.0 + scale) + shift


def _gate_rows(graw):
    r = _iota(graw.shape, 0)
    ls = _log_sigmoid(graw)
    return jnp.where((r < H_C) | (r >= 12), ls, jnp.where(r < 8, 0.0, graw))


def _gate_cols(graw):
    ln = _iota(graw.shape, 1)
    return jnp.where(ln < H_A, _log_sigmoid(graw), jnp.where(ln < 2 * H_A, graw, 0.0))


def _inproj_common(x_ref, mod_ref, g_ref, w_ref, bn_ref, sn_ref, bt_ref, st_ref, qk_ref, va_ref, oa_ref, u_ref, gc_ref):
    h = _norm_mod(x_ref[...], g_ref[...], mod_ref[1, 0], mod_ref[0, 0])
    hb = h.astype(BF16)

    def seg(off, width):
        return (_dot_nt(hb, w_ref[off:off + width, :]) + bn_ref[:, off:off + width]) * sn_ref[:, off:off + width]

    def tseg(off, rows):
        return (_dot_nt(w_ref[N_TOKEN_MAJOR + off:N_TOKEN_MAJOR + off + rows, :], hb) + bt_ref[off:off + rows, :]) * st_ref[off:off + rows, :]

    qk_ref[...] = seg(NC_QA, 2 * QK_PAD).astype(BF16)
    va_ref[...] = seg(NC_VA, W_A).astype(BF16)
    oa_ref[...] = seg(NC_OA, W_A)
    u_ref[...] = seg(NC_U, W_B)
    gc_ref[...] = _gate_cols(seg(NC_G, LANES))
    return seg, tseg


def _inproj_prompt_kernel(x_ref, mod_ref, g_ref, w_ref, bn_ref, sn_ref, bt_ref, st_ref, tri_ref, *rest, tiles_per_seq, n_prev):
    prev = rest[0:2 * min(n_prev, 1)]
    (qk_ref, va_ref, oa_ref, u_ref, gc_ref, kaug_ref, qta_ref, vta_ref, kt_ref, vt_ref, gt_ref, carry_ref) = rest[len(prev):]
    i = pl.program_id(0)
    tm = x_ref.shape[0]
    seg, tseg = _inproj_common(x_ref, mod_ref, g_ref, w_ref, bn_ref, sn_ref, bt_ref, st_ref, qk_ref, va_ref, oa_ref, u_ref, gc_ref)

    kt = tseg(TR_K, W_C)
    vt = tseg(TR_V, W_C)
    gt = _gate_rows(tseg(TR_G, 16))
    qt = tseg(TR_Q, W_C)
    if n_prev:
        kt_ref[0:n_prev, 0] = prev[0][:, 0]
        vt_ref[0:n_prev, 0] = prev[1][:, 0]
    kt_ref[n_prev, 0] = kt
    vt_ref[n_prev, 0] = vt
    gt_ref[0] = gt

    first = (i % tiles_per_seq) == 0
    carry = jnp.where(first, 0.0, carry_ref[:, 0:1])
    ft = _dot3_lhs(gt[0:8] * LOG2E, tri_ref[...]) + carry
    carry_ref[...] = jnp.broadcast_to(ft[:, tm - 1:tm], carry_ref.shape)
    f1, f2, f3 = [p.astype(F32) for p in _split3(ft)]

    r8 = _iota((8, tm), 0)
    ktail = jnp.concatenate([jnp.where(r8 < 3, 1.0, 0.0), -f1, -f2, -f3,
                             jnp.zeros((LANES - AUG_F0 - 24, tm), F32)], axis=0)
    r64 = _iota((LANES - DH_C, tm), 0)
    for hh in range(H_C):
        kaug_ref[hh] = jnp.concatenate([kt[hh * DH_C:(hh + 1) * DH_C], ktail], axis=0).T.astype(BF16)
        qa = jnp.where(r64 == 0, f1[hh:hh + 1], jnp.where(r64 == 1, f2[hh:hh + 1], jnp.where(r64 == 2, f3[hh:hh + 1], 0.0)))
        qa = jnp.where((r64 == 8 + hh) | (r64 == 16 + hh) | (r64 == 24 + hh), 1.0, qa)
        qta_ref[0, hh] = jnp.concatenate([qt[hh * DH_C:(hh + 1) * DH_C].astype(BF16), qa.astype(BF16)], axis=0)
        vta_ref[0, hh] = jnp.concatenate([vt[hh * DH_C:(hh + 1) * DH_C].astype(BF16),
                                          jnp.where(r64 == 0, 1.0, 0.0).astype(BF16)], axis=0)


def _inproj_sample_kernel(x_ref, mod_ref, g_ref, w_ref, bn_ref, sn_ref, bt_ref, st_ref,
                          qk_ref, va_ref, oa_ref, u_ref, gc_ref, qc_ref, kc_ref, vc_ref, kt_ref, vt_ref, gt_ref):
    _, tseg = _inproj_common(x_ref, mod_ref, g_ref, w_ref, bn_ref, sn_ref, bt_ref, st_ref, qk_ref, va_ref, oa_ref, u_ref, gc_ref)
    kt = tseg(TR_K, W_C)
    vt = tseg(TR_V, W_C)
    kt_ref[...] = kt
    vt_ref[...] = vt
    gt_ref[...] = _gate_rows(tseg(TR_G, 16))
    qc_ref[...] = tseg(TR_Q, W_C).T
    kc_ref[...] = kt.T
    vc_ref[...] = vt.T


def _inproj_weights(w_in, b_in):
    o = np.cumsum((0, H_A * DK_A, H_A * DK_A, W_A, H_A, H_A, W_A, W_B, W_C, W_C, W_C, H_C))
    wt_full = w_in.T
    d = w_in.shape[0]
    rows = lambda k: (wt_full[o[k]:o[k + 1]], b_in[o[k]:o[k + 1]])
    (wqa, bqa), (wka, bka), (wva, bva), (wia, bia), (wfa, bfa), (woa, boa), (wu, bu), (wqc, bqc), (wkc, bkc), (wvc, bvc), (wfc, bfc) = [rows(k) for k in range(11)]
    zw = lambda n: jnp.zeros((n, d), F32)
    zb = lambda n: jnp.zeros((n,), F32)
    ws = [wqa, zw(QK_PAD - 192), wka, zw(QK_PAD - 192), wva, woa, wu, wfa, wia, zw(LANES - 8)]
    bs = [bqa, zb(QK_PAD - 192), bka, zb(QK_PAD - 192), bva, boa, bu, bfa, bia, zb(LANES - 8)]
    sn =jnp.concatenate([jnp.ones((QK_PAD,), F32), jnp.full((QK_PAD,), DK_A ** -0.5, F32),
                          jnp.ones((N_TOKEN_MAJOR - 2 * QK_PAD,), F32)])[None, :]
    bn = jnp.concatenate(bs)[None, :]
    ws += [wkc, wvc, wfc, zw(2), wia, wfa, wqc]
    bt = jnp.concatenate([bkc, bvc, bfc, zb(2), bia, bfa, bqc])[:, None]
    st = jnp.concatenate([jnp.ones((TR_Q,), F32), jnp.full((W_C,), DH_C ** -0.5 * LOG2E, F32)])[:, None]
    return jnp.concatenate(ws, axis=0).astype(BF16), bn, sn, bt, st


def _inproj_prompt(x2, mod4, g1, weights, batch, seq, layer, kv_prev):
    w, bn, sn, bt, st = weights
    rows, d = x2.shape
    tm = min(512, seq)
    tps = seq // tm
    tri = jnp.asarray(np.triu(np.ones((tm, tm), np.float32)), BF16)
    full = lambda a: pl.BlockSpec(a.shape, lambda i: (0,) * a.ndim)
    rowblk = lambda wd: pl.BlockSpec((tm, wd), lambda i: (i, 0))
    tblk = lambda r: pl.BlockSpec((1, r, tm), lambda i: (i // tps, 0, i % tps))
    tblk4 = pl.BlockSpec((1, H_C, LANES, tm), lambda i: (i // tps, 0, 0, i % tps))
    kvblk = lambda n: pl.BlockSpec((n, 1, W_C, tm), lambda i: (0, i // tps, 0, i % tps))
    prev_in = [] if kv_prev is None else list(kv_prev)
    outs = pl.pallas_call(
        functools.partial(_inproj_prompt_kernel, tiles_per_seq=tps, n_prev=layer),
        grid=(rows // tm,),
        in_specs=[rowblk(d), pl.BlockSpec((6, 1, 1, d), lambda i: (0, i // tps, 0, 0)), full(g1),
                  full(w), full(bn), full(sn), full(bt), full(st), full(tri)] + [kvblk(layer)] * len(prev_in),
        out_specs=[rowblk(2 * QK_PAD), rowblk(W_A), rowblk(W_A), rowblk(W_B), rowblk(LANES),
                   pl.BlockSpec((H_C, tm, LANES), lambda i: (0, i, 0)), tblk4, tblk4, kvblk(layer + 1), kvblk(layer + 1), tblk(16)],
        out_shape=[jax.ShapeDtypeStruct((rows, 2 * QK_PAD), BF16), jax.ShapeDtypeStruct((rows, W_A), BF16),
                   jax.ShapeDtypeStruct((rows, W_A), F32), jax.ShapeDtypeStruct((rows, W_B), F32),
                   jax.ShapeDtypeStruct((rows, LANES), F32),
                   jax.ShapeDtypeStruct((H_C, rows, LANES), BF16),
                   jax.ShapeDtypeStruct((batch, H_C, LANES, seq), BF16), jax.ShapeDtypeStruct((batch, H_C, LANES, seq), BF16),
                   jax.ShapeDtypeStruct((layer + 1, batch, W_C, seq), F32), jax.ShapeDtypeStruct((layer + 1, batch, W_C, seq), F32),
                   jax.ShapeDtypeStruct((batch, 16, seq), F32)],
        scratch_shapes=[pltpu.VMEM((8, LANES), F32)],
        compiler_params=pltpu.CompilerParams(dimension_semantics=("arbitrary",), vmem_limit_bytes=VMEM_LIMIT),
        name="inproj_prompt",
    )(x2, mod4, g1, w, bn, sn, bt, st, tri, *prev_in)
    return outs


def _inproj_sample(x2, mod4, g1, weights):
    w, bn, sn, bt, st = weights
    rows, d = x2.shape
    full = lambda a: pl.BlockSpec(a.shape, lambda i: (0,) * a.ndim)
    o2 = lambda r, c: pl.BlockSpec((r, c), lambda i: (0, 0))
    sds = lambda r, c, t: jax.ShapeDtypeStruct((r, c), t)
    return pl.pallas_call(
        _inproj_sample_kernel,
        grid=(1,),
        in_specs=[full(x2), full(mod4), full(g1), full(w), full(bn), full(sn), full(bt), full(st)],
        out_specs=[o2(rows, 2 * QK_PAD), o2(rows, W_A), o2(rows, W_A), o2(rows, W_B), o2(rows, LANES),
                   o2(rows, W_C), o2(rows, W_C), o2(rows, W_C), o2(W_C, rows), o2(W_C, rows), o2(16, rows)],
        out_shape=[sds(rows, 2 * QK_PAD, BF16), sds(rows, W_A, BF16), sds(rows, W_A, F32), sds(rows, W_B, F32), sds(rows, LANES, F32),
                   sds(rows, W_C, F32), sds(rows, W_C, F32), sds(rows, W_C, F32), sds(W_C, rows, F32), sds(W_C, rows, F32), sds(16, rows, F32)],
        compiler_params=pltpu.CompilerParams(dimension_semantics=("arbitrary",), vmem_limit_bytes=VMEM_LIMIT),
        name="inproj_sample",
    )(x2, mod4, g1, w, bn, sn, bt, st)


def _state_mask():
    r = np.arange(QK_PAD)[:, None]
    c = np.arange(ST_COLS)[None, :]
    m = np.zeros((QK_PAD, ST_COLS), np.float32)
    for h in range(H_A):
        rows = (r >= h * DK_A) & (r < (h + 1) * DK_A)
        cols = ((c >= h * DV_A) & (c < (h + 1) * DV_A)) | (c == W_A + h)
        m[rows & cols] = 1.0
    return m


def _seq_kernel(*refs, pos0, group, interleave):
    chains = [_seq_one(g, *refs, pos0=pos0) for g in range(group)]
    if not interleave:
        for ch in chains:
            for _ in ch:
                pass
        return
    for lead in range(group - 1):
        for ch in chains[:group - 1 - lead]:
            next(ch)
    while chains:
        chains = [ch for ch in chains if next(ch, _DONE) is not _DONE]


_DONE = object()


def _seq_one(g, qk_ref, v_ref, o_ref, u_ref, gc_ref, gt_ref, st0_ref, m0_ref, hist0_ref, gh_ref, wp_ref, sp_ref, mask_ref,
             mix_ref, st_out_ref, m_out_ref, hist_out_ref, st_scr, m_scr, z_scr, *, pos0):
    c = pl.program_id(1)
    L = CHUNK

    @pl.when(c == 0)
    def _():
        st_scr[g] = st0_ref[g]
        m_scr[g] = m0_ref[g]
        z_scr[g, 0:16, :] = hist0_ref[g]

    q = qk_ref[g, :, 0:QK_PAD]
    k = qk_ref[g, :, QK_PAD:2 * QK_PAD]
    v = v_ref[g]
    gc = gc_ref[g]
    gt = gt_ref[g]
    st = st_scr[g]
    m_row = m_scr[g, 0:1, :]

    row = _iota((L, L), 0)
    colm = _iota((L, L), 1)
    tri_l = jnp.where(colm <= row, 1.0, 0.0).astype(BF16)
    tri_u = jnp.where(row <= colm, 1.0, 0.0).astype(BF16)
    lane128 = _iota((1, LANES), 1)
    lane_q = _iota((1, QK_PAD), 1)
    lane_v = _iota((1, W_A), 1)

    u = u_ref[g]
    z_scr[g, 16:16 + L, :] = u
    z = z_scr[g]
    s2 = z + pltpu.roll(z, 1, axis=0)
    s4 = s2 + pltpu.roll(s2, 2, axis=0)
    s8 = s4 + pltpu.roll(s4, 4, axis=0)
    s16 = s8 + pltpu.roll(s8, 8, axis=0)
    navail = (pos0 + c * L + 1 + _iota((L, 1), 0)).astype(F32)
    lane_u = _iota((1, W_B), 1)
    y = None
    for gi, (w, sw) in enumerate(zip(POOL_WINDOWS, (s2, s4, s8, s16))):
        yg = sw[16:16 + L, :] * (1.0 / jnp.minimum(float(w), navail))
        y = yg if y is None else jnp.where(lane_u >= gi * CG_B, yg, y)
    y = y - u
    hb = _dot(y.astype(BF16), wp_ref[...]) * sp_ref[...]
    z_scr[g, 0:16, :] = u[L - 16:L, :]

    bc_col = _dot3_rhs(tri_l, gc)
    bc_row = _dot3_lhs(gt[8:16], tri_u)
    qk_h = [_dot_nt(jnp.where((lane_q >= h * DK_A) & (lane_q < (h + 1) * DK_A), q, jnp.zeros_like(q)), k) for h in range(H_A)]
    qs = _dot(q, st.astype(BF16))
    yield
    inter = bc_col + m_row
    ia_col = pltpu.roll(gc, LANES - H_A, axis=1)

    causal = colm <= row
    mt_all = jnp.zeros((L, LANES), F32)
    s_list = []
    vblk = []
    for h in range(H_A):
        d = bc_col[:, h:h + 1] - bc_row[4 + h:5 + h, :] + gt[8 + h:9 + h, :]
        d = jnp.where(causal, d, NEG)
        mt = jnp.maximum(inter[:, h:h + 1], jnp.max(d, axis=-1, keepdims=True))
        s = qk_h[h] * jnp.exp(d - mt)
        s_list.append(s.astype(BF16))
        mt_all = jnp.where(lane128 == h, mt, mt_all)
        vm = jnp.where((lane_v >= h * DV_A) & (lane_v < (h + 1) * DV_A), v, jnp.zeros_like(v))
        vblk.append(jnp.concatenate([vm, jnp.broadcast_to(jnp.where(lane128 == h, 1.0, 0.0).astype(BF16), (L, LANES))], axis=1))
    s_cat = jnp.concatenate(s_list, axis=1)
    v_blk = jnp.concatenate(vblk, axis=0)
    pv = _dot(s_cat, v_blk)
    yield

    valid = lane128 < H_A
    m_new = mt_all[L - 1:L, :]
    g_col = jnp.where(valid, jnp.exp(bc_col[L - 1:L, :] - bc_col + ia_col - m_new), 0.0)
    a_last = jnp.where(valid, jnp.exp(inter[L - 1:L, :] - m_new), 0.0)
    gv = (v.astype(F32) * _expand_heads(g_col, W_A, DV_A)).astype(BF16)
    upd = _dot_tn(k, jnp.concatenate([gv, g_col.astype(BF16)], axis=1))
    yield
    a512 = jnp.concatenate([_expand_heads(a_last, W_A, DV_A), a_last], axis=1)
    st_new = a512 * st + mask_ref[...] * upd
    st_scr[g] = st_new
    m_rows = jnp.broadcast_to(jnp.where(valid, m_new, 0.0), (8, LANES))
    m_scr[g] = m_rows

    a_all = jnp.where(valid, jnp.exp(inter - mt_all), 0.0)
    num = _expand_heads(a_all, W_A, DV_A) * qs[:, 0:W_A] + pv[:, 0:W_A]
    den = a_all * qs[:, W_A:] + pv[:, W_A:]
    inv = 1.0 / jnp.maximum(jnp.abs(den), jnp.exp(-mt_all))
    hs = num * _expand_heads(jnp.where(valid, inv, 0.0), W_A, DV_A)

    sq = hs * hs
    ss_all = jnp.zeros((L, LANES), F32)
    for h in range(H_A):
        ssum = jnp.sum(jnp.where((lane_v >= h * DV_A) & (lane_v < (h + 1) * DV_A), sq, 0.0), axis=-1, keepdims=True)
        ss_all = jnp.where(lane128 == h, ssum, ss_all)
    r_all = lax.rsqrt(ss_all * (1.0 / DV_A) + EPS)
    ha = _sigmoid(o_ref[g]) * hs * _expand_heads(r_all, W_A, DV_A) * gh_ref[...]
    mix_ref[g] = jnp.concatenate([ha, hb], axis=1).astype(BF16)

    @pl.when(c == pl.num_programs(1) - 1)
    def _():
        st_out_ref[g] = st_new
        m_out_ref[g] = m_rows
        hist_out_ref[g] = u[L - 16:L, :]


def _sequence(qk, va, oa, u, gc, gt, st0, m0, hist0, gh, wp, sp, nb, nchunk, pos0):
    L = CHUNK
    G = SEQ_GROUP
    assert nb % G == 0
    mask = jnp.asarray(_state_mask())
    seq3 = lambda a: a.reshape(nb, nchunk * L, a.shape[-1])
    rowblk = lambda w: pl.BlockSpec((G, L, w), lambda b, c: (b, c, 0))
    perb = lambda a: pl.BlockSpec((G,) + a.shape[1:], lambda b, c: (b,) + (0,) * (a.ndim - 1))
    full = lambda a: pl.BlockSpec(a.shape, lambda b, c: (0,) * a.ndim)
    outs = pl.pallas_call(
        functools.partial(_seq_kernel, pos0=pos0, group=G, interleave=nchunk == 1),
        grid=(nb // G, nchunk),
        in_specs=[rowblk(2 * QK_PAD), rowblk(W_A), rowblk(W_A), rowblk(W_B), rowblk(LANES),
                  pl.BlockSpec((G, 16, L), lambda b, c: (b, 0, c)),
                  perb(st0), perb(m0), perb(hist0), full(gh), full(wp), full(sp), full(mask)],
        out_specs=[rowblk(MIX_AB), pl.BlockSpec((G, QK_PAD, ST_COLS), lambda b, c: (b, 0, 0)),
                   pl.BlockSpec((G, 8, LANES), lambda b, c: (b, 0, 0)), pl.BlockSpec((G, 16, W_B), lambda b, c: (b, 0, 0))],
        out_shape=[jax.ShapeDtypeStruct((nb, nchunk * L, MIX_AB), BF16), jax.ShapeDtypeStruct((nb, QK_PAD, ST_COLS), F32),
                   jax.ShapeDtypeStruct((nb, 8, LANES), F32), jax.ShapeDtypeStruct((nb, 16, W_B), F32)],
        scratch_shapes=[pltpu.VMEM((G, QK_PAD, ST_COLS), F32), pltpu.VMEM((G, 8, LANES), F32), pltpu.VMEM((G, 16 + L, W_B), F32)],
        compiler_params=pltpu.CompilerParams(dimension_semantics=("arbitrary", "arbitrary"), vmem_limit_bytes=VMEM_LIMIT),
        name="sequence",
    )(seq3(qk), seq3(va), seq3(oa), seq3(u), seq3(gc), gt, st0, m0, hist0, gh, wp, sp, mask)
    return (outs[0].reshape(nb * nchunk * L, MIX_AB),) + tuple(outs[1:])


def _pack_state(C, n, m):
    nb = C.shape[0]
    z = lambda r, c: jnp.zeros((nb, r, c), F32)
    blocks = []
    for h in range(H_A):
        blocks.append(jnp.concatenate([z(DK_A, h * DV_A), jnp.swapaxes(C[:, h], 1, 2), z(DK_A, W_A - (h + 1) * DV_A),
                                       z(DK_A, h), n[:, h][:, :, None], z(DK_A, LANES - h - 1)], axis=2))
    st = jnp.concatenate(blocks + [z(QK_PAD - H_A * DK_A, ST_COLS)], axis=1)
    mm = jnp.concatenate([jnp.broadcast_to(m[:, None, :], (nb, 8, H_A)), z(8, LANES - H_A)], axis=2)
    return st, mm


def _unpack_state(st, mm):
    C = jnp.stack([jnp.swapaxes(st[:, h * DK_A:(h + 1) * DK_A, h * DV_A:(h + 1) * DV_A], 1, 2) for h in range(H_A)], axis=1)
    n = jnp.stack([st[:, h * DK_A:(h + 1) * DK_A, W_A + h] for h in range(H_A)], axis=1)
    return C, n, mm[:, 0, 0:H_A]


def _attn_prompt_kernel(qt_ref, k_ref, vt_ref, o_ref, s_scr, m_scr, acc_scr, *, tq, tk, sw):
    qi = pl.program_id(2)
    m_scr[...] = jnp.full(m_scr.shape, NEG, F32)
    acc_scr[...] = jnp.zeros(acc_scr.shape, F32)
    units = [(hh, st) for hh in range(2) for st in range(tq // sw)]

    def qk_block(j, slot, skip=()):
        start = pl.multiple_of(j * tk, tk)
        for hh, st in units:
            if st not in skip:
                s_scr[slot, hh, :, st * sw:(st + 1) * sw] = _dot(k_ref[hh, pl.ds(start, tk), :], qt_ref[0, hh, :, st * sw:(st + 1) * sw])

    def softmax_pv_block(j, slot, masked=(), skip=()):
        start = pl.multiple_of(j * tk, tk)
        for hh, st in units:
            if st in skip:
                continue
            strip = slice(st * sw, (st + 1) * sw)
            s = s_scr[slot, hh, :, strip]
            if st in masked:
                keyg = j * tk + _iota((tk, sw), 0)
                qryg = qi * tq + st * sw + _iota((tk, sw), 1)
                s = jnp.where(keyg <= qryg, s, NEG)
            m = m_scr[hh, 0:1, strip]
            m_new = jnp.maximum(m, jnp.max(s, axis=0, keepdims=True))
            p = jnp.exp2(s - m_new)
            pv = _dot(vt_ref[0, hh, :, pl.ds(start, tk)], p.astype(BF16))
            acc_scr[hh, :, strip] = jnp.exp2(m - m_new) * acc_scr[hh, :, strip] + pv
            m_scr[hh, 0:1, strip] = m_new

    n_full = (qi * tq) // tk
    qk_block(0, 0)

    def body(i, carry):
        j = 2 * i
        qk_block(j + 1, 1)
        softmax_pv_block(j, 0)
        qk_block(j + 2, 0)
        softmax_pv_block(j + 1, 1)
        return carry

    lax.fori_loop(0, n_full // 2, body, 0)
    n_diag = tq // tk
    for d in range(n_diag):
        if d + 1 < n_diag:
            qk_block(n_full + d + 1, (d + 1) % 2, skip=tuple(range(d + 1)))
        softmax_pv_block(n_full + d, d % 2, masked=(d,), skip=tuple(range(d)))
    outs = []
    for hh in range(2):
        acc = acc_scr[hh]
        outs.append((acc * (1.0 / acc[DH_C:DH_C + 1, :])).T)
    lane = _iota((1, LANES), 1)
    o_ref[...] = jnp.where(lane < DH_C, outs[0], pltpu.roll(outs[1], DH_C, axis=1)).astype(BF16)


def _attn_prompt(qta, k_aug, vta, batch, seq):
    rows = batch * seq
    tk = sw = ATTN_STRIP
    tq = min(ATTN_Q_TILE, seq)
    assert seq % tq == 0 and (tq // tk) % 2 == 0
    nq = seq // tq
    return pl.pallas_call(
        functools.partial(_attn_prompt_kernel, tq=tq, tk=tk, sw=sw),
        grid=(batch, H_C // 2, nq),
        in_specs=[pl.BlockSpec((1, 2, LANES, tq), lambda b, p, i: (b, p, 0, i)),
                  pl.BlockSpec((2, seq, LANES), lambda b, p, i: (p, b, 0)),
                  pl.BlockSpec((1, 2, LANES, seq), lambda b, p, i: (b, p, 0, 0))],
        out_specs=pl.BlockSpec((tq, LANES), lambda b, p, i: (b * nq + i, p)),
        out_shape=jax.ShapeDtypeStruct((rows, W_C), BF16),
        scratch_shapes=[pltpu.VMEM((2, 2, tk, tq), F32), pltpu.VMEM((2, 8, tq), F32), pltpu.VMEM((2, LANES, tq), F32)],
        compiler_params=pltpu.CompilerParams(dimension_semantics=("arbitrary", "arbitrary", "arbitrary"), vmem_limit_bytes=VMEM_LIMIT),
        name="attn_prompt",
    )(qta, k_aug, vta)


def _attn_sample_kernel(pt_ref, q_ref, knt_ref, vnt_ref, lfn_ref, ck_hbm, cv_hbm, clf_hbm, o_ref,
                        kbuf, vbuf, lbuf, sems, m_scr, acc_scr, carry_scr, cq_scr, *, n_pages_step):
    P = n_pages_step
    j = pl.program_id(1)
    nstep = pl.num_programs(1)
    t = pl.program_id(0) * nstep + j
    n_total = pl.num_programs(0) * nstep
    slot = t % 2

    def page_copies(step, sl):
        out = []
        for i in range(P):
            pg = pt_ref[step * P + i]
            out += [pltpu.make_async_copy(ck_hbm.at[pg], kbuf.at[sl, i], sems.at[sl, 0]),
                    pltpu.make_async_copy(cv_hbm.at[pg], vbuf.at[sl, i], sems.at[sl, 1]),
                    pltpu.make_async_copy(clf_hbm.at[pg], lbuf.at[sl, i], sems.at[sl, 2])]
        return out

    @pl.when(t == 0)
    def _():
        for cp in page_copies(0, 0):
            cp.start()

    @pl.when(t + 1 < n_total)
    def _():
        for cp in page_copies(t + 1, 1 - slot):
            cp.start()

    for cp in page_copies(t, slot):
        cp.wait()

    R = 8 * SAMPLE_Q
    head_of_lane = lax.shift_right_logical(_iota((R, W_C), 1), 6)
    headmask = head_of_lane == (_iota((R, W_C), 0) & 7)
    lane = _iota((R, LANES), 1)
    rq = lax.shift_right_logical(_iota((R, LANES), 0), 3)

    def tile_q(x8):
        return jnp.concatenate([x8] * SAMPLE_Q, axis=0)

    q8 = q_ref[0]
    qrep = jnp.concatenate([jnp.broadcast_to(q8[qq:qq + 1], (8, W_C)) for qq in range(SAMPLE_Q)], axis=0)
    qbd = jnp.where(headmask, qrep, 0.0).astype(BF16)

    @pl.when(j == 0)
    def _():
        a = tile_q(lfn_ref[0] * LOG2E)
        cq = jnp.sum(jnp.where(lane <= rq, a, 0.0), axis=-1, keepdims=True)
        incl = jnp.where(_iota((LANES, LANES), 0) <= _iota((LANES, LANES), 1), 1.0, 0.0).astype(BF16)
        crow = _dot3_lhs(a, incl)
        s = _dot(qbd, knt_ref[0].astype(BF16)) + cq - crow
        s = jnp.where((lane <= rq) & (lane < SAMPLE_Q), s, NEG)
        m = jnp.max(s, axis=-1, keepdims=True)
        p = jnp.exp2(s - m)
        l = jnp.sum(p, axis=-1, keepdims=True)
        acc_scr[:, 0:W_C] = _dot_nt(p.astype(BF16), vnt_ref[0].astype(BF16))
        acc_scr[:, W_C:] = jnp.broadcast_to(l, (R, LANES))
        m_scr[...] = jnp.broadcast_to(m, (R, LANES))
        cq_scr[...] = jnp.broadcast_to(cq, (R, LANES))
        carry_scr[...] = jnp.zeros_like(carry_scr)

    cq = cq_scr[:, 0:1]
    strict = jnp.where(_iota((LANES, LANES), 0) > _iota((LANES, LANES), 1), 1.0, 0.0).astype(BF16)
    lf_all = lbuf[slot].reshape(8 * P, LANES) * LOG2E
    suf = _dot3_lhs(lf_all, strict)
    carry = carry_scr[:, 0:1]
    bias = [None] * P
    for i in range(P):
        bias[i] = tile_q(suf[8 * i:8 * i + 8] + carry)
        carry = carry + suf[8 * i:8 * i + 8, 0:1] + lf_all[8 * i:8 * i + 8, 0:1]
    carry_scr[...] = jnp.broadcast_to(carry, carry_scr.shape)

    kcat = jnp.concatenate([kbuf[slot, i].astype(BF16) for i in range(P)], axis=1)
    vcat = jnp.concatenate([vbuf[slot, i].astype(BF16) for i in range(P)], axis=1)
    s = _dot(qbd, kcat) + jnp.concatenate(bias, axis=1) + cq
    m_old = m_scr[:, 0:1]
    m_new = jnp.maximum(m_old, jnp.max(s, axis=-1, keepdims=True))
    alpha = jnp.exp2(m_old - m_new)
    p = jnp.exp2(s - m_new)
    l = alpha * acc_scr[:, W_C:W_C + 1] + jnp.sum(p, axis=-1, keepdims=True)
    acc = alpha * acc_scr[:, 0:W_C] + _dot_nt(p.astype(BF16), vcat)
    acc_scr[:, 0:W_C] = acc
    acc_scr[:, W_C:] = jnp.broadcast_to(l, (R, LANES))
    m_scr[...] = jnp.broadcast_to(m_new, (R, LANES))

    @pl.when(j == pl.num_programs(1) - 1)
    def _():
        o = jnp.where(headmask, acc * (1.0 / l), 0.0)
        rows = [jnp.sum(o[8 * qq:8 * qq + 8], axis=0, keepdims=True) for qq in range(SAMPLE_Q)]
        o_ref[0] = jnp.concatenate(rows + [jnp.zeros((8 - SAMPLE_Q, W_C), F32)], axis=0)


def _attn_sample(page_table, q8, knt, vnt, lfn, ckt, cvt, clf, layer, n_phys):
    nb, n_pages = page_table.shape
    P = min(SAMPLE_PAGES_PER_STEP, n_pages)
    nstep = n_pages // P
    page_ids = (layer * n_phys + page_table[:, ::-1]).reshape(-1)

    perb = lambda a: pl.BlockSpec((1,) + a.shape[1:], lambda b, j, pt: (b, 0, 0))
    hbm = pl.BlockSpec(memory_space=pl.ANY)
    R = 8 * SAMPLE_Q
    grid_spec = pltpu.PrefetchScalarGridSpec(
        num_scalar_prefetch=1, grid=(nb, nstep),
        in_specs=[perb(q8), perb(knt), perb(vnt), perb(lfn), hbm, hbm, hbm],
        out_specs=pl.BlockSpec((1, 8, W_C), lambda b, j, pt: (b, 0, 0)),
        scratch_shapes=[pltpu.VMEM((2, P, W_C, LANES), F32), pltpu.VMEM((2, P, W_C, LANES), F32), pltpu.VMEM((2, P, 8, LANES), F32),
                        pltpu.SemaphoreType.DMA((2, 3)),
                        pltpu.VMEM((R, LANES), F32), pltpu.VMEM((R, W_C + LANES), F32),
                        pltpu.VMEM((8, LANES), F32), pltpu.VMEM((R, LANES), F32)])
    return pl.pallas_call(
        functools.partial(_attn_sample_kernel, n_pages_step=P),
        grid_spec=grid_spec,
        out_shape=jax.ShapeDtypeStruct((nb, 8, W_C), F32),
        compiler_params=pltpu.CompilerParams(dimension_semantics=("arbitrary", "arbitrary"), vmem_limit_bytes=VMEM_LIMIT),
        name="attn_sample",
    )(page_ids, q8, knt, vnt, lfn, ckt, cvt, clf)


def _mlp_kernel(x_ref, mab_ref, hc_ref, mod_ref, g2_ref, gf_ref, wo_ref, wu_ref, wd_ref, *out_refs, final):
    x = x_ref[...]
    mix = _dot(mab_ref[...], wo_ref[0, 0:MIX_AB, :]) + _dot(hc_ref[...], wo_ref[0, MIX_AB:, :])
    x1 = x + mod_ref[2, 0] * mix
    h2 = _norm_mod(x1, g2_ref[...], mod_ref[4, 0], mod_ref[3, 0]).astype(BF16)
    ff = jnp.zeros_like(x)
    fc = 1024
    for f in range(D_FF // fc):
        a = jnp.maximum(_dot(h2, wu_ref[0, :, f * fc:(f + 1) * fc]), 0.0)
        ff = ff + _dot((a * a).astype(BF16), wd_ref[0, f * fc:(f + 1) * fc, :])
    x2 = x1 + mod_ref[5, 0] * ff
    out_refs[0][...] = x2
    if final:
        ms = jnp.mean(x2 * x2, axis=-1, keepdims=True)
        out_refs[1][...] = x2 * lax.rsqrt(ms + EPS) * gf_ref[...]


def _mlp(x2, mab, hc, mod4, g2, gf, wo, wu, wd, layer, tm, rows_per_mod, final):
    rows, d = x2.shape
    s = mod4.shape[2]
    const = lambda a: pl.BlockSpec(a.shape, lambda i: (0,) * a.ndim, pipeline_mode=pl.Buffered(1))
    layerw = lambda a: pl.BlockSpec((1,) + a.shape[1:], lambda i: (layer, 0, 0), pipeline_mode=pl.Buffered(1))
    rowblk = lambda w: pl.BlockSpec((tm, w), lambda i: (i, 0))
    n_out = 2 if final else 1
    outs = pl.pallas_call(
        functools.partial(_mlp_kernel, final=final),
        grid=(rows // tm,),
        in_specs=[rowblk(d), rowblk(MIX_AB), rowblk(W_C),
                  pl.BlockSpec((6, 1, s, d), lambda i: (0, (i * tm) // rows_per_mod, 0, 0)),
                  const(g2), const(gf), layerw(wo), layerw(wu), layerw(wd)],
        out_specs=[rowblk(d)] * n_out,
        out_shape=[jax.ShapeDtypeStruct((rows, d), F32)] * n_out,
        compiler_params=pltpu.CompilerParams(dimension_semantics=("arbitrary",), vmem_limit_bytes=VMEM_LIMIT),
        name="mlp",
    )(x2, mab, hc, mod4, g2, gf, wo, wu, wd)
    return outs


def kernel(x_prompt, x_sample, c_prompt, c_sample, cache_k, cache_v, cache_logf, page_table, state_C, state_n, state_m, state_pool, w_ada, b_ada, g_norm1, g_norm2, w_in, b_in, g_head_a, w_pool, s_pool, w_out, w_up, w_down, g_final):
    depth = w_ada.shape[0]
    B, T, D = x_prompt.shape
    SB, ST, _ = x_sample.shape
    n_phys, page = cache_k.shape[1], cache_k.shape[2]
    n_pages = page_table.shape[1]
    rows_p, rows_s = B * T, SB * ST
    pos0_s = n_pages * page
    assert T % CHUNK == 0 and ST == SAMPLE_Q and page == LANES

    nc = B + SB
    c_all = jnp.concatenate([c_prompt, c_sample, jnp.zeros((-nc % 8, D), F32)], axis=0)
    mod = _modulation(c_all, w_ada, b_ada)

    ckt = jnp.transpose(cache_k, (0, 1, 3, 4, 2)).reshape(depth * n_phys, W_C, page)
    cvt = jnp.transpose(cache_v, (0, 1, 3, 4, 2)).reshape(depth * n_phys, W_C, page)
    clf = jnp.pad(jnp.transpose(cache_logf, (0, 1, 3, 2)), ((0, 0), (0, 0), (0, 8 - H_C), (0, 0))).reshape(depth * n_phys, 8, page)
    wo_all, wu_all, wd_all = w_out.astype(BF16), w_up.astype(BF16), w_down.astype(BF16)

    xp = x_prompt.reshape(rows_p, D)
    xs = x_sample.reshape(rows_s, D)
    zeros_state = (jnp.zeros((B, QK_PAD, ST_COLS), F32), jnp.zeros((B, 8, LANES), F32))
    outs_p, outs_s = [], []
    yp = ys = None
    kv_all = None
    for l in range(depth):
        final = l == depth - 1
        g1, g2, gf = g_norm1[l][None, :], g_norm2[l][None, :], g_final[None, :]
        gh = g_head_a[l].reshape(1, W_A)
        wp = jax.scipy.linalg.block_diag(*[w_pool[l, g] for g in range(len(POOL_WINDOWS))]).astype(BF16)
        sp = s_pool[l][None, :]
        weights = _inproj_weights(w_in[l], b_in[l])
        modl = mod[l].reshape(-1, 6, D)
        mod_p = jnp.transpose(modl[0:B], (1, 0, 2))[:, :, None, :]
        mod_s = jnp.transpose(jnp.repeat(modl[B:B + SB], ST, axis=0), (1, 0, 2))[:, None]

        (qk, va, oa, u, gc, k_aug, qta, vta, kt_all, vt_all, gt) = _inproj_prompt(xp, mod_p, g1, weights, B, T, l, kv_all)
        kv_all = (kt_all, vt_all)
        mab, st1, m1, hist1 = _sequence(qk, va, oa, u, gc, gt, zeros_state[0], zeros_state[1], jnp.zeros((B, 16, W_B), F32),
                                        gh, wp, sp, B, T // CHUNK, 0)
        hc = _attn_prompt(qta, k_aug, vta, B, T)
        res = _mlp(xp, mab, hc, mod_p, g2, gf, wo_all, wu_all, wd_all, l, min(512, T), T, final)
        xp = res[0]
        if final:
            yp = res[1]
        C1, n1, mm1 = _unpack_state(st1, m1)
        outs_p.append((jnp.transpose(gt[:, 0:H_C, :], (0, 2, 1)), C1, n1, mm1, hist1[:, 1:]))

        (qk, va, oa, u, gc, qc, kc, vc, kt, vt, gt) = _inproj_sample(xs, mod_s, g1, weights)

        def pad_rows(a, fill=0.0):
            a3 = a.reshape(SB, ST, a.shape[-1])
            return jnp.pad(a3, ((0, 0), (0, CHUNK - ST), (0, 0)), constant_values=fill).reshape(SB * CHUNK, a.shape[-1])

        lane = jnp.arange(LANES)[None, :]
        gc_pad = jnp.where((lane >= H_A) & (lane < 2 * H_A), pad_rows(gc, NEG), pad_rows(gc))
        gt3 = jnp.transpose(gt.reshape(16, SB, ST), (1, 0, 2))
        rr = jnp.arange(16)[None, :, None]
        gt_pad = jnp.where((rr >= 8) & (rr < 12), jnp.pad(gt3, ((0, 0), (0, 0), (0, CHUNK - ST)), constant_values=NEG),
                           jnp.pad(gt3, ((0, 0), (0, 0), (0, CHUNK - ST))))
        st0, m0 = _pack_state(state_C[l], state_n[l], state_m[l])
        hist0 = jnp.pad(state_pool[l], ((0, 0), (1, 0), (0, 0)))
        mab, st1, m1, _ = _sequence(pad_rows(qk), pad_rows(va), pad_rows(oa), pad_rows(u), gc_pad, gt_pad, st0, m0, hist0,
                                    gh, wp, sp, SB, 1, pos0_s)
        mab = mab.reshape(SB, CHUNK, MIX_AB)[:, 0:ST].reshape(rows_s, MIX_AB)
        q8 = jnp.pad(qc.reshape(SB, ST, W_C), ((0, 0), (0, 8 - ST), (0, 0)))
        tpad = lambda a: jnp.pad(jnp.transpose(a.reshape(a.shape[0], SB, ST), (1, 0, 2)), ((0, 0), (0, 0), (0, LANES - ST)))
        knt, vnt = tpad(kt), tpad(vt)
        lfn = jnp.where(rr < H_C, jnp.pad(gt3, ((0, 0), (0, 0), (0, LANES - ST))), 0.0)[:, 0:8]
        hc8 = _attn_sample(page_table, q8, knt, vnt, lfn, ckt, cvt, clf, l, n_phys)
        hc = hc8[:, 0:ST].reshape(rows_s, W_C).astype(BF16)
        res = _mlp(xs, mab, hc, mod_s, g2, gf, wo_all, wu_all, wd_all, l, rows_s, rows_s, final)
        xs = res[0]
        if final:
            ys = res[1]
        C1, n1, mm1 = _unpack_state(st1, m1)
        lf_s = jnp.transpose(gt3[:, 0:H_C, :], (0, 2, 1))
        pool_s = jnp.concatenate([state_pool[l], u.reshape(SB, ST, W_B)], axis=1)[:, -POOL_HIST:]
        outs_s.append((kc.reshape(SB, ST, H_C, DH_C), vc.reshape(SB, ST, H_C, DH_C), lf_s, C1, n1, mm1, pool_s))

    to_bthd = lambda a: jnp.transpose(a.reshape(depth, B, H_C, DH_C, T), (0, 1, 4, 2, 3))
    sp_ = [jnp.stack(a) for a in zip(*outs_p)]
    ss_ = [jnp.stack(a) for a in zip(*outs_s)]
    return (yp.reshape(B, T, D), ys.reshape(SB, ST, D), to_bthd(kv_all[0]), to_bthd(kv_all[1]), *sp_, *ss_)
```

```python
import functools

import numpy as np
import jax
import jax.numpy as jnp
from jax import lax
from jax.experimental import pallas as pl
from jax.experimental.pallas import tpu as pltpu

F32 = jnp.float32
BF16 = jnp.bfloat16

D_MODEL = 1024
H_A = 4
W_A = 384
DV_A = 96
DK_A = 48
W_B = 256
CG_B = 64
POOL_HIST = 15
POOL_WINDOWS = (2, 4, 8, 16)
DH_C = 64
W_C = 384
H_C = 6
D_FF = 4096
EPS = 1e-6
CHUNK = 128
LOG2E = 1.4426950408889634
NEG = -1e30

LANES = 128
QK_PAD = 256
ST_COLS = W_A + LANES
MIX_AB = W_A + W_B
AUG_ONE0 = DH_C
AUG_F0 = DH_C + 8
VMEM_LIMIT = 56 * 1024 * 1024
SEQ_GROUP = 2
ATTN_Q_TILE = 1024
ATTN_STRIP = 256
MLP_CHUNKS = 4
SAMPLE_Q = 4
SAMPLE_PAGES_PER_STEP = 16


def _dot(a, b):
    return jnp.dot(a, b, preferred_element_type=F32)


def _dot_nt(a, b):
    return lax.dot_general(a, b, (((1,), (1,)), ((), ())), preferred_element_type=F32)


def _dot_tn(a, b):
    return lax.dot_general(a, b, (((0,), (0,)), ((), ())), preferred_element_type=F32)


def _split3(x):
    a = x.astype(BF16)
    r = x - a.astype(F32)
    b = r.astype(BF16)
    c = (r - b.astype(F32)).astype(BF16)
    return a, b, c


def _dot3_rhs(m01, x):
    a, b, c = _split3(x)
    return _dot(m01, a) + _dot(m01, b) + _dot(m01, c)


def _dot3_lhs(x, m01):
    a, b, c = _split3(x)
    return _dot(a, m01) + _dot(b, m01) + _dot(c, m01)


def _log_sigmoid(x):
    return jnp.minimum(x, 0.0) - jnp.log1p(jnp.exp(-jnp.abs(x)))


def _sigmoid(x):
    return 1.0 / (1.0 + jnp.exp(-x))


def _iota(shape, dim):
    return lax.broadcasted_iota(jnp.int32, shape, dim)


def _expand_heads(src, width, per_head):
    lane = _iota((1, width), 1)
    out = jnp.zeros(src.shape[:-1] + (width,), F32)
    for h in range(H_A):
        out = jnp.where((lane >= h * per_head) & (lane < (h + 1) * per_head), src[:, h:h + 1], out)
    return out


def _mod_kernel(c_ref, w_ref, b_ref, o_ref):
    c = c_ref[...]
    s = (c * _sigmoid(c)).astype(BF16)
    o_ref[0] = _dot(s, w_ref[0].astype(BF16)) + b_ref[0]


def _modulation(c_all, w_ada, b_ada):
    depth, d, n6 = w_ada.shape
    rows = c_all.shape[0]
    tn = 1536
    return pl.pallas_call(
        _mod_kernel,
        grid=(depth, n6 // tn),
        in_specs=[
            pl.BlockSpec((rows, d), lambda l, j: (0, 0)),
            pl.BlockSpec((1, d, tn), lambda l, j: (l, 0, j)),
            pl.BlockSpec((1, 1, tn), lambda l, j: (l, 0, j)),
        ],
        out_specs=pl.BlockSpec((1, rows, tn), lambda l, j: (l, 0, j)),
        out_shape=jax.ShapeDtypeStruct((depth, rows, n6), F32),
        compiler_params=pltpu.CompilerParams(dimension_semantics=("arbitrary", "arbitrary"), vmem_limit_bytes=VMEM_LIMIT),
        name="modulation",
    )(c_all, w_ada, b_ada.reshape(depth, 1, n6))


NC_QA, NC_KA, NC_VA, NC_OA, NC_U, NC_G = 0, 256, 512, 896, 1280, 1536
N_TOKEN_MAJOR = 1664
TR_K, TR_V, TR_G, TR_Q = 0, 384, 768, 784
N_FEATURE_MAJOR = TR_Q + W_C
N_W_ROWS = N_TOKEN_MAJOR + N_FEATURE_MAJOR


def _norm_mod(x, g, scale, shift):
    ms = jnp.mean(x * x, axis=-1, keepdims=True)
    return (x * lax.rsqrt(ms + EPS) * g) * (1.0 + scale) + shift


def _gate_rows(graw):
    r = _iota(graw.shape, 0)
    ls = _log_sigmoid(graw)
    return jnp.where((r < H_C) | (r >= 12), ls, jnp.where(r < 8, 0.0, graw))


def _gate_cols(graw):
    ln = _iota(graw.shape, 1)
    return jnp.where(ln < H_A, _log_sigmoid(graw), jnp.where(ln < 2 * H_A, graw, 0.0))


def _inproj_common(x_ref, mod_ref, g_ref, w_ref, bn_ref, sn_ref, bt_ref, st_ref, qk_ref, va_ref, oa_ref, u_ref, gc_ref):
    h = _norm_mod(x_ref[...], g_ref[...], mod_ref[1, 0], mod_ref[0, 0])
    hb = h.astype(BF16)

    def seg(off, width):
        return (_dot_nt(hb, w_ref[off:off + width, :]) + bn_ref[:, off:off + width]) * sn_ref[:, off:off + width]

    def tseg(off, rows):
        return (_dot_nt(w_ref[N_TOKEN_MAJOR + off:N_TOKEN_MAJOR + off + rows, :], hb) + bt_ref[off:off + rows, :]) * st_ref[off:off + rows, :]

    qk_ref[...] = seg(NC_QA, 2 * QK_PAD).astype(BF16)
    va_ref[...] = seg(NC_VA, W_A).astype(BF16)
    oa_ref[...] = seg(NC_OA, W_A)
    u_ref[...] = seg(NC_U, W_B)
    gc_ref[...] = _gate_cols(seg(NC_G, LANES))
    return seg, tseg


def _inproj_prompt_kernel(x_ref, mod_ref, g_ref, w_ref, bn_ref, sn_ref, bt_ref, st_ref, tri_ref, *rest, tiles_per_seq, n_prev):
    prev = rest[0:2 * min(n_prev, 1)]
    (qk_ref, va_ref, oa_ref, u_ref, gc_ref, kaug_ref, qta_ref, vta_ref, kt_ref, vt_ref, gt_ref, carry_ref) = rest[len(prev):]
    i = pl.program_id(0)
    tm = x_ref.shape[0]
    seg, tseg = _inproj_common(x_ref, mod_ref, g_ref, w_ref, bn_ref, sn_ref, bt_ref, st_ref, qk_ref, va_ref, oa_ref, u_ref, gc_ref)

    kt = tseg(TR_K, W_C)
    vt = tseg(TR_V, W_C)
    gt = _gate_rows(tseg(TR_G, 16))
    qt = tseg(TR_Q, W_C)
    if n_prev:
        kt_ref[0:n_prev, 0] = prev[0][:, 0]
        vt_ref[0:n_prev, 0] = prev[1][:, 0]
    kt_ref[n_prev, 0] = kt
    vt_ref[n_prev, 0] = vt
    gt_ref[0] = gt

    first = (i % tiles_per_seq) == 0
    carry = jnp.where(first, 0.0, carry_ref[:, 0:1])
    ft = _dot3_lhs(gt[0:8] * LOG2E, tri_ref[...]) + carry
    carry_ref[...] = jnp.broadcast_to(ft[:, tm - 1:tm], carry_ref.shape)
    f1, f2, f3 = [p.astype(F32) for p in _split3(ft)]

    r8 = _iota((8, tm), 0)
    ktail = jnp.concatenate([jnp.where(r8 < 3, 1.0, 0.0), -f1, -f2, -f3,
                             jnp.zeros((LANES - AUG_F0 - 24, tm), F32)], axis=0)
    r64 = _iota((LANES - DH_C, tm), 0)
    for hh in range(H_C):
        kaug_ref[hh] = jnp.concatenate([kt[hh * DH_C:(hh + 1) * DH_C], ktail], axis=0).T.astype(BF16)
        qa = jnp.where(r64 == 0, f1[hh:hh + 1], jnp.where(r64 == 1, f2[hh:hh + 1], jnp.where(r64 == 2, f3[hh:hh + 1], 0.0)))
        qa = jnp.where((r64 == 8 + hh) | (r64 == 16 + hh) | (r64 == 24 + hh), 1.0, qa)
        qta_ref[0, hh] = jnp.concatenate([qt[hh * DH_C:(hh + 1) * DH_C].astype(BF16), qa.astype(BF16)], axis=0)
        vta_ref[0, hh] = jnp.concatenate([vt[hh * DH_C:(hh + 1) * DH_C].astype(BF16),
                                          jnp.where(r64 == 0, 1.0, 0.0).astype(BF16)], axis=0)


def _inproj_sample_kernel(x_ref, mod_ref, g_ref, w_ref, bn_ref, sn_ref, bt_ref, st_ref,
                          qk_ref, va_ref, oa_ref, u_ref, gc_ref, qc_ref, kc_ref, vc_ref, kt_ref, vt_ref, gt_ref):
    _, tseg = _inproj_common(x_ref, mod_ref, g_ref, w_ref, bn_ref, sn_ref, bt_ref, st_ref, qk_ref, va_ref, oa_ref, u_ref, gc_ref)
    kt = tseg(TR_K, W_C)
    vt = tseg(TR_V, W_C)
    kt_ref[...] = kt
    vt_ref[...] = vt
    gt_ref[...] = _gate_rows(tseg(TR_G, 16))
    qc_ref[...] = tseg(TR_Q, W_C).T
    kc_ref[...] = kt.T
    vc_ref[...] = vt.T


def _inproj_weights(w_in, b_in):
    o = np.cumsum((0, H_A * DK_A, H_A * DK_A, W_A, H_A, H_A, W_A, W_B, W_C, W_C, W_C, H_C))
    wt_full = w_in.T
    d = w_in.shape[0]
    rows = lambda k: (wt_full[o[k]:o[k + 1]], b_in[o[k]:o[k + 1]])
    (wqa, bqa), (wka, bka), (wva, bva), (wia, bia), (wfa, bfa), (woa, boa), (wu, bu), (wqc, bqc), (wkc, bkc), (wvc, bvc), (wfc, bfc) = [rows(k) for k in range(11)]
    zw = lambda n: jnp.zeros((n, d), F32)
    zb = lambda n: jnp.zeros((n,), F32)
    ws = [wqa, zw(QK_PAD - 192), wka, zw(QK_PAD - 192), wva, woa, wu, wfa, wia, zw(LANES - 8)]
    bs = [bqa, zb(QK_PAD - 192), bka, zb(QK_PAD - 192), bva, boa, bu, bfa, bia, zb(LANES - 8)]
    sn =jnp.concatenate([jnp.ones((QK_PAD,), F32), jnp.full((QK_PAD,), DK_A ** -0.5, F32),
                          jnp.ones((N_TOKEN_MAJOR - 2 * QK_PAD,), F32)])[None, :]
    bn = jnp.concatenate(bs)[None, :]
    ws += [wkc, wvc, wfc, zw(2), wia, wfa, wqc]
    bt = jnp.concatenate([bkc, bvc, bfc, zb(2), bia, bfa, bqc])[:, None]
    st = jnp.concatenate([jnp.ones((TR_Q,), F32), jnp.full((W_C,), DH_C ** -0.5 * LOG2E, F32)])[:, None]
    return jnp.concatenate(ws, axis=0).astype(BF16), bn, sn, bt, st


def _inproj_prompt(x2, mod4, g1, weights, batch, seq, layer, kv_prev):
    w, bn, sn, bt, st = weights
    rows, d = x2.shape
    tm = min(512, seq)
    tps = seq // tm
    tri = jnp.asarray(np.triu(np.ones((tm, tm), np.float32)), BF16)
    full = lambda a: pl.BlockSpec(a.shape, lambda i: (0,) * a.ndim)
    rowblk = lambda wd: pl.BlockSpec((tm, wd), lambda i: (i, 0))
    tblk = lambda r: pl.BlockSpec((1, r, tm), lambda i: (i // tps, 0, i % tps))
    tblk4 = pl.BlockSpec((1, H_C, LANES, tm), lambda i: (i // tps, 0, 0, i % tps))
    kvblk = lambda n: pl.BlockSpec((n, 1, W_C, tm), lambda i: (0, i // tps, 0, i % tps))
    prev_in = [] if kv_prev is None else list(kv_prev)
    outs = pl.pallas_call(
        functools.partial(_inproj_prompt_kernel, tiles_per_seq=tps, n_prev=layer),
        grid=(rows // tm,),
        in_specs=[rowblk(d), pl.BlockSpec((6, 1, 1, d), lambda i: (0, i // tps, 0, 0)), full(g1),
                  full(w), full(bn), full(sn), full(bt), full(st), full(tri)] + [kvblk(layer)] * len(prev_in),
        out_specs=[rowblk(2 * QK_PAD), rowblk(W_A), rowblk(W_A), rowblk(W_B), rowblk(LANES),
                   pl.BlockSpec((H_C, tm, LANES), lambda i: (0, i, 0)), tblk4, tblk4, kvblk(layer + 1), kvblk(layer + 1), tblk(16)],
        out_shape=[jax.ShapeDtypeStruct((rows, 2 * QK_PAD), BF16), jax.ShapeDtypeStruct((rows, W_A), BF16),
                   jax.ShapeDtypeStruct((rows, W_A), F32), jax.ShapeDtypeStruct((rows, W_B), F32),
                   jax.ShapeDtypeStruct((rows, LANES), F32),
                   jax.ShapeDtypeStruct((H_C, rows, LANES), BF16),
                   jax.ShapeDtypeStruct((batch, H_C, LANES, seq), BF16), jax.ShapeDtypeStruct((batch, H_C, LANES, seq), BF16),
                   jax.ShapeDtypeStruct((layer + 1, batch, W_C, seq), F32), jax.ShapeDtypeStruct((layer + 1, batch, W_C, seq), F32),
                   jax.ShapeDtypeStruct((batch, 16, seq), F32)],
        scratch_shapes=[pltpu.VMEM((8, LANES), F32)],
        compiler_params=pltpu.CompilerParams(dimension_semantics=("arbitrary",), vmem_limit_bytes=VMEM_LIMIT),
        name="inproj_prompt",
    )(x2, mod4, g1, w, bn, sn, bt, st, tri, *prev_in)
    return outs


def _inproj_sample(x2, mod4, g1, weights):
    w, bn, sn, bt, st = weights
    rows, d = x2.shape
    full = lambda a: pl.BlockSpec(a.shape, lambda i: (0,) * a.ndim)
    o2 = lambda r, c: pl.BlockSpec((r, c), lambda i: (0, 0))
    sds = lambda r, c, t: jax.ShapeDtypeStruct((r, c), t)
    return pl.pallas_call(
        _inproj_sample_kernel,
        grid=(1,),
        in_specs=[full(x2), full(mod4), full(g1), full(w), full(bn), full(sn), full(bt), full(st)],
        out_specs=[o2(rows, 2 * QK_PAD), o2(rows, W_A), o2(rows, W_A), o2(rows, W_B), o2(rows, LANES),
                   o2(rows, W_C), o2(rows, W_C), o2(rows, W_C), o2(W_C, rows), o2(W_C, rows), o2(16, rows)],
        out_shape=[sds(rows, 2 * QK_PAD, BF16), sds(rows, W_A, BF16), sds(rows, W_A, F32), sds(rows, W_B, F32), sds(rows, LANES, F32),
                   sds(rows, W_C, F32), sds(rows, W_C, F32), sds(rows, W_C, F32), sds(W_C, rows, F32), sds(W_C, rows, F32), sds(16, rows, F32)],
        compiler_params=pltpu.CompilerParams(dimension_semantics=("arbitrary",), vmem_limit_bytes=VMEM_LIMIT),
        name="inproj_sample",
    )(x2, mod4, g1, w, bn, sn, bt, st)


def _state_mask():
    r = np.arange(QK_PAD)[:, None]
    c = np.arange(ST_COLS)[None, :]
    m = np.zeros((QK_PAD, ST_COLS), np.float32)
    for h in range(H_A):
        rows = (r >= h * DK_A) & (r < (h + 1) * DK_A)
        cols = ((c >= h * DV_A) & (c < (h + 1) * DV_A)) | (c == W_A + h)
        m[rows & cols] = 1.0
    return m


def _seq_kernel(*refs, pos0, group, interleave):
    chains = [_seq_one(g, *refs, pos0=pos0) for g in range(group)]
    if not interleave:
        for ch in chains:
            for _ in ch:
                pass
        return
    for lead in range(group - 1):
        for ch in chains[:group - 1 - lead]:
            next(ch)
    while chains:
        chains = [ch for ch in chains if next(ch, _DONE) is not _DONE]


_DONE = object()


def _seq_one(g, qk_ref, v_ref, o_ref, u_ref, gc_ref, gt_ref, st0_ref, m0_ref, hist0_ref, gh_ref, wp_ref, sp_ref, mask_ref,
             mix_ref, st_out_ref, m_out_ref, hist_out_ref, st_scr, m_scr, z_scr, *, pos0):
    c = pl.program_id(1)
    L = CHUNK

    @pl.when(c == 0)
    def _():
        st_scr[g] = st0_ref[g]
        m_scr[g] = m0_ref[g]
        z_scr[g, 0:16, :] = hist0_ref[g]

    q = qk_ref[g, :, 0:QK_PAD]
    k = qk_ref[g, :, QK_PAD:2 * QK_PAD]
    v = v_ref[g]
    gc = gc_ref[g]
    gt = gt_ref[g]
    st = st_scr[g]
    m_row = m_scr[g, 0:1, :]

    row = _iota((L, L), 0)
    colm = _iota((L, L), 1)
    tri_l = jnp.where(colm <= row, 1.0, 0.0).astype(BF16)
    tri_u = jnp.where(row <= colm, 1.0, 0.0).astype(BF16)
    lane128 = _iota((1, LANES), 1)
    lane_q = _iota((1, QK_PAD), 1)
    lane_v = _iota((1, W_A), 1)

    u = u_ref[g]
    z_scr[g, 16:16 + L, :] = u
    z = z_scr[g]
    s2 = z + pltpu.roll(z, 1, axis=0)
    s4 = s2 + pltpu.roll(s2, 2, axis=0)
    s8 = s4 + pltpu.roll(s4, 4, axis=0)
    s16 = s8 + pltpu.roll(s8, 8, axis=0)
    navail = (pos0 + c * L + 1 + _iota((L, 1), 0)).astype(F32)
    lane_u = _iota((1, W_B), 1)
    y = None
    for gi, (w, sw) in enumerate(zip(POOL_WINDOWS, (s2, s4, s8, s16))):
        yg = sw[16:16 + L, :] * (1.0 / jnp.minimum(float(w), navail))
        y = yg if y is None else jnp.where(lane_u >= gi * CG_B, yg, y)
    y = y - u
    hb = _dot(y.astype(BF16), wp_ref[...]) * sp_ref[...]
    z_scr[g, 0:16, :] = u[L - 16:L, :]

    bc_col = _dot3_rhs(tri_l, gc)
    bc_row = _dot3_lhs(gt[8:16], tri_u)
    qk_h = [_dot_nt(jnp.where((lane_q >= h * DK_A) & (lane_q < (h + 1) * DK_A), q, jnp.zeros_like(q)), k) for h in range(H_A)]
    qs = _dot(q, st.astype(BF16))
    yield
    inter = bc_col + m_row
    ia_col = pltpu.roll(gc, LANES - H_A, axis=1)

    causal = colm <= row
    mt_all = jnp.zeros((L, LANES), F32)
    s_list = []
    vblk = []
    for h in range(H_A):
        d = bc_col[:, h:h + 1] - bc_row[4 + h:5 + h, :] + gt[8 + h:9 + h, :]
        d = jnp.where(causal, d, NEG)
        mt = jnp.maximum(inter[:, h:h + 1], jnp.max(d, axis=-1, keepdims=True))
        s = qk_h[h] * jnp.exp(d - mt)
        s_list.append(s.astype(BF16))
        mt_all = jnp.where(lane128 == h, mt, mt_all)
        vm = jnp.where((lane_v >= h * DV_A) & (lane_v < (h + 1) * DV_A), v, jnp.zeros_like(v))
        vblk.append(jnp.concatenate([vm, jnp.broadcast_to(jnp.where(lane128 == h, 1.0, 0.0).astype(BF16), (L, LANES))], axis=1))
    s_cat = jnp.concatenate(s_list, axis=1)
    v_blk = jnp.concatenate(vblk, axis=0)
    pv = _dot(s_cat, v_blk)
    yield

    valid = lane128 < H_A
    m_new = mt_all[L - 1:L, :]
    g_col = jnp.where(valid, jnp.exp(bc_col[L - 1:L, :] - bc_col + ia_col - m_new), 0.0)
    a_last = jnp.where(valid, jnp.exp(inter[L - 1:L, :] - m_new), 0.0)
    gv = (v.astype(F32) * _expand_heads(g_col, W_A, DV_A)).astype(BF16)
    upd = _dot_tn(k, jnp.concatenate([gv, g_col.astype(BF16)], axis=1))
    yield
    a512 = jnp.concatenate([_expand_heads(a_last, W_A, DV_A), a_last], axis=1)
    st_new = a512 * st + mask_ref[...] * upd
    st_scr[g] = st_new
    m_rows = jnp.broadcast_to(jnp.where(valid, m_new, 0.0), (8, LANES))
    m_scr[g] = m_rows

    a_all = jnp.where(valid, jnp.exp(inter - mt_all), 0.0)
    num = _expand_heads(a_all, W_A, DV_A) * qs[:, 0:W_A] + pv[:, 0:W_A]
    den = a_all * qs[:, W_A:] + pv[:, W_A:]
    inv = 1.0 / jnp.maximum(jnp.abs(den), jnp.exp(-mt_all))
    hs = num * _expand_heads(jnp.where(valid, inv, 0.0), W_A, DV_A)

    sq = hs * hs
    ss_all = jnp.zeros((L, LANES), F32)
    for h in range(H_A):
        ssum = jnp.sum(jnp.where((lane_v >= h * DV_A) & (lane_v < (h + 1) * DV_A), sq, 0.0), axis=-1, keepdims=True)
        ss_all = jnp.where(lane128 == h, ssum, ss_all)
    r_all = lax.rsqrt(ss_all * (1.0 / DV_A) + EPS)
    ha = _sigmoid(o_ref[g]) * hs * _expand_heads(r_all, W_A, DV_A) * gh_ref[...]
    mix_ref[g] = jnp.concatenate([ha, hb], axis=1).astype(BF16)

    @pl.when(c == pl.num_programs(1) - 1)
    def _():
        st_out_ref[g] = st_new
        m_out_ref[g] = m_rows
        hist_out_ref[g] = u[L - 16:L, :]


def _sequence(qk, va, oa, u, gc, gt, st0, m0, hist0, gh, wp, sp, nb, nchunk, pos0):
    L = CHUNK
    G = SEQ_GROUP
    assert nb % G == 0
    mask = jnp.asarray(_state_mask())
    seq3 = lambda a: a.reshape(nb, nchunk * L, a.shape[-1])
    rowblk = lambda w: pl.BlockSpec((G, L, w), lambda b, c: (b, c, 0))
    perb = lambda a: pl.BlockSpec((G,) + a.shape[1:], lambda b, c: (b,) + (0,) * (a.ndim - 1))
    full = lambda a: pl.BlockSpec(a.shape, lambda b, c: (0,) * a.ndim)
    outs = pl.pallas_call(
        functools.partial(_seq_kernel, pos0=pos0, group=G, interleave=nchunk == 1),
        grid=(nb // G, nchunk),
        in_specs=[rowblk(2 * QK_PAD), rowblk(W_A), rowblk(W_A), rowblk(W_B), rowblk(LANES),
                  pl.BlockSpec((G, 16, L), lambda b, c: (b, 0, c)),
                  perb(st0), perb(m0), perb(hist0), full(gh), full(wp), full(sp), full(mask)],
        out_specs=[rowblk(MIX_AB), pl.BlockSpec((G, QK_PAD, ST_COLS), lambda b, c: (b, 0, 0)),
                   pl.BlockSpec((G, 8, LANES), lambda b, c: (b, 0, 0)), pl.BlockSpec((G, 16, W_B), lambda b, c: (b, 0, 0))],
        out_shape=[jax.ShapeDtypeStruct((nb, nchunk * L, MIX_AB), BF16), jax.ShapeDtypeStruct((nb, QK_PAD, ST_COLS), F32),
                   jax.ShapeDtypeStruct((nb, 8, LANES), F32), jax.ShapeDtypeStruct((nb, 16, W_B), F32)],
        scratch_shapes=[pltpu.VMEM((G, QK_PAD, ST_COLS), F32), pltpu.VMEM((G, 8, LANES), F32), pltpu.VMEM((G, 16 + L, W_B), F32)],
        compiler_params=pltpu.CompilerParams(dimension_semantics=("arbitrary", "arbitrary"), vmem_limit_bytes=VMEM_LIMIT),
        name="sequence",
    )(seq3(qk), seq3(va), seq3(oa), seq3(u), seq3(gc), gt, st0, m0, hist0, gh, wp, sp, mask)
    return (outs[0].reshape(nb * nchunk * L, MIX_AB),) + tuple(outs[1:])


def _pack_state(C, n, m):
    nb = C.shape[0]
    z = lambda r, c: jnp.zeros((nb, r, c), F32)
    blocks = []
    for h in range(H_A):
        blocks.append(jnp.concatenate([z(DK_A, h * DV_A), jnp.swapaxes(C[:, h], 1, 2), z(DK_A, W_A - (h + 1) * DV_A),
                                       z(DK_A, h), n[:, h][:, :, None], z(DK_A, LANES - h - 1)], axis=2))
    st = jnp.concatenate(blocks + [z(QK_PAD - H_A * DK_A, ST_COLS)], axis=1)
    mm = jnp.concatenate([jnp.broadcast_to(m[:, None, :], (nb, 8, H_A)), z(8, LANES - H_A)], axis=2)
    return st, mm


def _unpack_state(st, mm):
    C = jnp.stack([jnp.swapaxes(st[:, h * DK_A:(h + 1) * DK_A, h * DV_A:(h + 1) * DV_A], 1, 2) for h in range(H_A)], axis=1)
    n = jnp.stack([st[:, h * DK_A:(h + 1) * DK_A, W_A + h] for h in range(H_A)], axis=1)
    return C, n, mm[:, 0, 0:H_A]


def _attn_prompt_kernel(qt_ref, k_ref, vt_ref, o_ref, s_scr, m_scr, acc_scr, *, tq, tk, sw):
    qi = pl.program_id(2)
    m_scr[...] = jnp.full(m_scr.shape, NEG, F32)
    acc_scr[...] = jnp.zeros(acc_scr.shape, F32)
    units = [(hh, st) for hh in range(2) for st in range(tq // sw)]

    def qk_block(j, slot, skip=()):
        start = pl.multiple_of(j * tk, tk)
        for hh, st in units:
            if st not in skip:
                s_scr[slot, hh, :, st * sw:(st + 1) * sw] = _dot(k_ref[hh, pl.ds(start, tk), :], qt_ref[0, hh, :, st * sw:(st + 1) * sw])

    def softmax_pv_block(j, slot, masked=(), skip=()):
        start = pl.multiple_of(j * tk, tk)
        for hh, st in units:
            if st in skip:
                continue
            strip = slice(st * sw, (st + 1) * sw)
            s = s_scr[slot, hh, :, strip]
            if st in masked:
                keyg = j * tk + _iota((tk, sw), 0)
                qryg = qi * tq + st * sw + _iota((tk, sw), 1)
                s = jnp.where(keyg <= qryg, s, NEG)
            m = m_scr[hh, 0:1, strip]
            m_new = jnp.maximum(m, jnp.max(s, axis=0, keepdims=True))
            p = jnp.exp2(s - m_new)
            pv = _dot(vt_ref[0, hh, :, pl.ds(start, tk)], p.astype(BF16))
            acc_scr[hh, :, strip] = jnp.exp2(m - m_new) * acc_scr[hh, :, strip] + pv
            m_scr[hh, 0:1, strip] = m_new

    n_full = (qi * tq) // tk
    qk_block(0, 0)

    def body(i, carry):
        j = 2 * i
        qk_block(j + 1, 1)
        softmax_pv_block(j, 0)
        qk_block(j + 2, 0)
        softmax_pv_block(j + 1, 1)
        return carry

    lax.fori_loop(0, n_full // 2, body, 0)
    n_diag = tq // tk
    for d in range(n_diag):
        if d + 1 < n_diag:
            qk_block(n_full + d + 1, (d + 1) % 2, skip=tuple(range(d + 1)))
        softmax_pv_block(n_full + d, d % 2, masked=(d,), skip=tuple(range(d)))
    outs = []
    for hh in range(2):
        acc = acc_scr[hh]
        outs.append((acc * (1.0 / acc[DH_C:DH_C + 1, :])).T)
    lane = _iota((1, LANES), 1)
    o_ref[...] = jnp.where(lane < DH_C, outs[0], pltpu.roll(outs[1], DH_C, axis=1)).astype(BF16)


def _attn_prompt(qta, k_aug, vta, batch, seq):
    rows = batch * seq
    tk = sw = ATTN_STRIP
    tq = min(ATTN_Q_TILE, seq)
    assert seq % tq == 0 and (tq // tk) % 2 == 0
    nq = seq // tq
    return pl.pallas_call(
        functools.partial(_attn_prompt_kernel, tq=tq, tk=tk, sw=sw),
        grid=(batch, H_C // 2, nq),
        in_specs=[pl.BlockSpec((1, 2, LANES, tq), lambda b, p, i: (b, p, 0, i)),
                  pl.BlockSpec((2, seq, LANES), lambda b, p, i: (p, b, 0)),
                  pl.BlockSpec((1, 2, LANES, seq), lambda b, p, i: (b, p, 0, 0))],
        out_specs=pl.BlockSpec((tq, LANES), lambda b, p, i: (b * nq + i, p)),
        out_shape=jax.ShapeDtypeStruct((rows, W_C), BF16),
        scratch_shapes=[pltpu.VMEM((2, 2, tk, tq), F32), pltpu.VMEM((2, 8, tq), F32), pltpu.VMEM((2, LANES, tq), F32)],
        compiler_params=pltpu.CompilerParams(dimension_semantics=("arbitrary", "arbitrary", "arbitrary"), vmem_limit_bytes=VMEM_LIMIT),
        name="attn_prompt",
    )(qta, k_aug, vta)


def _when(cond):
    if isinstance(cond, bool):
        return (lambda f: f()) if cond else (lambda f: None)
    return pl.when(cond)


def _sample_attn_step(t, n_total, first, last, P, pt_ref, q_ref, knt_ref, vnt_ref, lfn_ref, ck_hbm, cv_hbm, clf_hbm, o_ref,
                      kbuf, vbuf, lbuf, sems, m_scr, acc_scr, carry_scr, cq_scr):
    slot = t % 2

    def page_copies(step, sl):
        out = []
        for i in range(P):
            pg = pt_ref[step * P + i]
            out += [pltpu.make_async_copy(ck_hbm.at[pg], kbuf.at[sl, i], sems.at[sl, 0]),
                    pltpu.make_async_copy(cv_hbm.at[pg], vbuf.at[sl, i], sems.at[sl, 1]),
                    pltpu.make_async_copy(clf_hbm.at[pg], lbuf.at[sl, i], sems.at[sl, 2])]
        return out

    @_when(first and (t == 0))
    def _():
        for cp in page_copies(0, 0):
            cp.start()

    @pl.when(t + 1 < n_total)
    def _():
        for cp in page_copies(t + 1, 1 - slot):
            cp.start()

    for cp in page_copies(t, slot):
        cp.wait()

    R = 8 * SAMPLE_Q
    head_of_lane = lax.shift_right_logical(_iota((R, W_C), 1), 6)
    headmask = head_of_lane == (_iota((R, W_C), 0) & 7)
    lane = _iota((R, LANES), 1)
    rq = lax.shift_right_logical(_iota((R, LANES), 0), 3)

    def tile_q(x8):
        return jnp.concatenate([x8] * SAMPLE_Q, axis=0)

    q8 = q_ref[0]
    qrep = jnp.concatenate([jnp.broadcast_to(q8[qq:qq + 1], (8, W_C)) for qq in range(SAMPLE_Q)], axis=0)
    qbd = jnp.where(headmask, qrep, 0.0).astype(BF16)

    @_when(first)
    def _():
        a = tile_q(lfn_ref[0] * LOG2E)
        cq = jnp.sum(jnp.where(lane <= rq, a, 0.0), axis=-1, keepdims=True)
        incl = jnp.where(_iota((LANES, LANES), 0) <= _iota((LANES, LANES), 1), 1.0, 0.0).astype(BF16)
        crow = _dot3_lhs(a, incl)
        s = _dot(qbd, knt_ref[0].astype(BF16)) + cq - crow
        s = jnp.where((lane <= rq) & (lane < SAMPLE_Q), s, NEG)
        m = jnp.max(s, axis=-1, keepdims=True)
        p = jnp.exp2(s - m)
        l = jnp.sum(p, axis=-1, keepdims=True)
        acc_scr[:, 0:W_C] = _dot_nt(p.astype(BF16), vnt_ref[0].astype(BF16))
        acc_scr[:, W_C:] = jnp.broadcast_to(l, (R, LANES))
        m_scr[...] = jnp.broadcast_to(m, (R, LANES))
        cq_scr[...] = jnp.broadcast_to(cq, (R, LANES))
        carry_scr[...] = jnp.zeros_like(carry_scr)

    cq = cq_scr[:, 0:1]
    strict = jnp.where(_iota((LANES, LANES), 0) > _iota((LANES, LANES), 1), 1.0, 0.0).astype(BF16)
    lf_all = lbuf[slot].reshape(8 * P, LANES) * LOG2E
    suf = _dot3_lhs(lf_all, strict)
    carry = carry_scr[:, 0:1]
    bias = [None] * P
    for i in range(P):
        bias[i] = tile_q(suf[8 * i:8 * i + 8] + carry)
        carry = carry + suf[8 * i:8 * i + 8, 0:1] + lf_all[8 * i:8 * i + 8, 0:1]
    carry_scr[...] = jnp.broadcast_to(carry, carry_scr.shape)

    kcat = jnp.concatenate([kbuf[slot, i].astype(BF16) for i in range(P)], axis=1)
    vcat = jnp.concatenate([vbuf[slot, i].astype(BF16) for i in range(P)], axis=1)
    s = _dot(qbd, kcat) + jnp.concatenate(bias, axis=1) + cq
    yield
    m_old = m_scr[:, 0:1]
    m_new = jnp.maximum(m_old, jnp.max(s, axis=-1, keepdims=True))
    alpha = jnp.exp2(m_old - m_new)
    p = jnp.exp2(s - m_new)
    l = alpha * acc_scr[:, W_C:W_C + 1] + jnp.sum(p, axis=-1, keepdims=True)
    acc = alpha * acc_scr[:, 0:W_C] + _dot_nt(p.astype(BF16), vcat)
    acc_scr[:, 0:W_C] = acc
    acc_scr[:, W_C:] = jnp.broadcast_to(l, (R, LANES))
    m_scr[...] = jnp.broadcast_to(m_new, (R, LANES))

    @_when(last)
    def _():
        o = jnp.where(headmask, acc * (1.0 / l), 0.0)
        rows = [jnp.sum(o[8 * qq:8 * qq + 8], axis=0, keepdims=True) for qq in range(SAMPLE_Q)]
        o_ref[0] = jnp.concatenate(rows + [jnp.zeros((8 - SAMPLE_Q, W_C), F32)], axis=0)


def _sample_attn_scratch(P):
    R = 8 * SAMPLE_Q
    return [pltpu.VMEM((2, P, W_C, LANES), F32), pltpu.VMEM((2, P, W_C, LANES), F32), pltpu.VMEM((2, P, 8, LANES), F32),
            pltpu.SemaphoreType.DMA((2, 3)),
            pltpu.VMEM((R, LANES), F32), pltpu.VMEM((R, W_C + LANES), F32), pltpu.VMEM((8, LANES), F32), pltpu.VMEM((R, LANES), F32)]


def _mlp_kernel(*refs, final, n_groups, pages_per_group):
    fused = n_groups > 0
    if fused:
        pt_ref, refs = refs[0], refs[1:]
    x_ref, mab_ref, hc_ref, mod_ref, g2_ref, gf_ref, wo_ref, wu_ref, wd_ref = refs[0:9]
    rest = refs[9:]
    if fused:
        sample_in, rest = rest[0:7], rest[7:]
    n_out = (2 if final else 1) + (1 if fused else 0)
    out_refs, scratch = rest[0:n_out], rest[n_out:]

    x = x_ref[...]
    mix = _dot(mab_ref[...], wo_ref[0, 0:MIX_AB, :]) + _dot(hc_ref[...], wo_ref[0, MIX_AB:, :])
    x1 = x + mod_ref[2, 0] * mix
    h2 = _norm_mod(x1, g2_ref[...], mod_ref[4, 0], mod_ref[3, 0]).astype(BF16)
    ff = jnp.zeros_like(x)
    n_chunks = n_groups if fused else MLP_CHUNKS
    fc = D_FF // n_chunks
    for f in range(n_chunks):
        a = jnp.maximum(_dot(h2, wu_ref[0, :, f * fc:(f + 1) * fc]), 0.0)
        if fused:
            step = _sample_attn_step(pl.program_id(0) * n_groups + f, pl.num_programs(0) * n_groups, f == 0, f == n_groups - 1,
                                     pages_per_group, pt_ref, *sample_in, out_refs[n_out - 1], *scratch)
            next(step)
        ff = ff + _dot((a * a).astype(BF16), wd_ref[0, f * fc:(f + 1) * fc, :])
        if fused:
            next(step, None)
    x2 = x1 + mod_ref[5, 0] * ff
    out_refs[0][...] = x2
    if final:
        ms = jnp.mean(x2 * x2, axis=-1, keepdims=True)
        out_refs[1][...] = x2 * lax.rsqrt(ms + EPS) * gf_ref[...]


def _mlp(x2, mab, hc, mod4, g2, gf, wo, wu, wd, layer, tm, rows_per_mod, final, sample=None):
    rows, d = x2.shape
    s = mod4.shape[2]
    fused = sample is not None
    im = (lambda f: (lambda i, pt: f(i))) if fused else (lambda f: f)
    const = lambda a: pl.BlockSpec(a.shape, im(lambda i: (0,) * a.ndim), pipeline_mode=pl.Buffered(1))
    layerw = lambda a: pl.BlockSpec((1,) + a.shape[1:], im(lambda i: (layer, 0, 0)), pipeline_mode=pl.Buffered(1))
    rowblk = lambda w: pl.BlockSpec((tm, w), im(lambda i: (i, 0)))
    n_out = 2 if final else 1
    in_specs = [rowblk(d), rowblk(MIX_AB), rowblk(W_C),
                pl.BlockSpec((6, 1, s, d), im(lambda i: (0, (i * tm) // rows_per_mod, 0, 0))),
                const(g2), const(gf), layerw(wo), layerw(wu), layerw(wd)]
    out_specs = [rowblk(d)] * n_out
    out_shape = [jax.ShapeDtypeStruct((rows, d), F32)] * n_out
    args = [x2, mab, hc, mod4, g2, gf, wo, wu, wd]
    scratch, n_groups, P, n_prefetch = [], 0, 0, 0
    if fused:
        page_ids, q8, knt, vnt, lfn = sample[0:5]
        nb, n_pages = q8.shape[0], page_ids.shape[0] // q8.shape[0]
        assert nb == rows // tm
        P = min(SAMPLE_PAGES_PER_STEP, n_pages)
        n_groups = n_pages // P
        assert D_FF % (n_groups * LANES) == 0
        perb = lambda a: pl.BlockSpec((1,) + a.shape[1:], lambda i, pt: (i, 0, 0))
        in_specs += [perb(q8), perb(knt), perb(vnt), perb(lfn)] + [pl.BlockSpec(memory_space=pl.ANY)] * 3
        out_specs += [pl.BlockSpec((1, 8, W_C), lambda i, pt: (i, 0, 0))]
        out_shape += [jax.ShapeDtypeStruct((nb, 8, W_C), F32)]
        args = [page_ids] + args + list(sample[1:])
        scratch, n_prefetch = _sample_attn_scratch(P), 1
    grid_spec = pltpu.PrefetchScalarGridSpec(num_scalar_prefetch=n_prefetch, grid=(rows // tm,), in_specs=in_specs,
                                             out_specs=out_specs, scratch_shapes=scratch)
    return pl.pallas_call(
        functools.partial(_mlp_kernel, final=final, n_groups=n_groups, pages_per_group=P),
        grid_spec=grid_spec,
        out_shape=out_shape,
        compiler_params=pltpu.CompilerParams(dimension_semantics=("arbitrary",), vmem_limit_bytes=VMEM_LIMIT),
        name="mlp_attn_sample" if fused else "mlp",
    )(*args)


def kernel(x_prompt, x_sample, c_prompt, c_sample, cache_k, cache_v, cache_logf, page_table, state_C, state_n, state_m, state_pool, w_ada, b_ada, g_norm1, g_norm2, w_in, b_in, g_head_a, w_pool, s_pool, w_out, w_up, w_down, g_final):
    depth = w_ada.shape[0]
    B, T, D = x_prompt.shape
    SB, ST, _ = x_sample.shape
    n_phys, page = cache_k.shape[1], cache_k.shape[2]
    n_pages = page_table.shape[1]
    rows_p, rows_s = B * T, SB * ST
    pos0_s = n_pages * page
    assert T % CHUNK == 0 and ST == SAMPLE_Q and page == LANES

    nc = B + SB
    c_all = jnp.concatenate([c_prompt, c_sample, jnp.zeros((-nc % 8, D), F32)], axis=0)
    mod = _modulation(c_all, w_ada, b_ada)

    ckt = jnp.transpose(cache_k, (0, 1, 3, 4, 2)).reshape(depth * n_phys, W_C, page)
    cvt = jnp.transpose(cache_v, (0, 1, 3, 4, 2)).reshape(depth * n_phys, W_C, page)
    clf = jnp.pad(jnp.transpose(cache_logf, (0, 1, 3, 2)), ((0, 0), (0, 0), (0, 8 - H_C), (0, 0))).reshape(depth * n_phys, 8, page)
    wo_all, wu_all, wd_all = w_out.astype(BF16), w_up.astype(BF16), w_down.astype(BF16)

    xp = x_prompt.reshape(rows_p, D)
    xs = x_sample.reshape(rows_s, D)
    zeros_state = (jnp.zeros((B, QK_PAD, ST_COLS), F32), jnp.zeros((B, 8, LANES), F32))
    outs_p, outs_s = [], []
    yp = ys = None
    kv_all = None
    for l in range(depth):
        final = l == depth - 1
        g1, g2, gf = g_norm1[l][None, :], g_norm2[l][None, :], g_final[None, :]
        gh = g_head_a[l].reshape(1, W_A)
        wp = jax.scipy.linalg.block_diag(*[w_pool[l, g] for g in range(len(POOL_WINDOWS))]).astype(BF16)
        sp = s_pool[l][None, :]
        weights = _inproj_weights(w_in[l], b_in[l])
        modl = mod[l].reshape(-1, 6, D)
        mod_p = jnp.transpose(modl[0:B], (1, 0, 2))[:, :, None, :]
        mod_s = jnp.transpose(jnp.repeat(modl[B:B + SB], ST, axis=0), (1, 0, 2))[:, None]

        (qk, va, oa, u, gc, k_aug, qta, vta, kt_all, vt_all, gt) = _inproj_prompt(xp, mod_p, g1, weights, B, T, l, kv_all)
        kv_all = (kt_all, vt_all)
        mab, st1, m1, hist1 = _sequence(qk, va, oa, u, gc, gt, zeros_state[0], zeros_state[1], jnp.zeros((B, 16, W_B), F32),
                                        gh, wp, sp, B, T // CHUNK, 0)
        hc = _attn_prompt(qta, k_aug, vta, B, T)
        mab_p, hc_p = mab, hc
        C1, n1, mm1 = _unpack_state(st1, m1)
        outs_p.append((jnp.transpose(gt[:, 0:H_C, :], (0, 2, 1)), C1, n1, mm1, hist1[:, 1:]))

        (qk, va, oa, u, gc, qc, kc, vc, kt, vt, gt) = _inproj_sample(xs, mod_s, g1, weights)

        def pad_rows(a, fill=0.0):
            a3 = a.reshape(SB, ST, a.shape[-1])
            return jnp.pad(a3, ((0, 0), (0, CHUNK - ST), (0, 0)), constant_values=fill).reshape(SB * CHUNK, a.shape[-1])

        lane = jnp.arange(LANES)[None, :]
        gc_pad = jnp.where((lane >= H_A) & (lane < 2 * H_A), pad_rows(gc, NEG), pad_rows(gc))
        gt3 = jnp.transpose(gt.reshape(16, SB, ST), (1, 0, 2))
        rr = jnp.arange(16)[None, :, None]
        gt_pad = jnp.where((rr >= 8) & (rr < 12), jnp.pad(gt3, ((0, 0), (0, 0), (0, CHUNK - ST)), constant_values=NEG),
                           jnp.pad(gt3, ((0, 0), (0, 0), (0, CHUNK - ST))))
        st0, m0 = _pack_state(state_C[l], state_n[l], state_m[l])
        hist0 = jnp.pad(state_pool[l], ((0, 0), (1, 0), (0, 0)))
        mab, st1, m1, _ = _sequence(pad_rows(qk), pad_rows(va), pad_rows(oa), pad_rows(u), gc_pad, gt_pad, st0, m0, hist0,
                                    gh, wp, sp, SB, 1, pos0_s)
        mab = mab.reshape(SB, CHUNK, MIX_AB)[:, 0:ST].reshape(rows_s, MIX_AB)
        q8 = jnp.pad(qc.reshape(SB, ST, W_C), ((0, 0), (0, 8 - ST), (0, 0)))
        tpad = lambda a: jnp.pad(jnp.transpose(a.reshape(a.shape[0], SB, ST), (1, 0, 2)), ((0, 0), (0, 0), (0, LANES - ST)))
        knt, vnt = tpad(kt), tpad(vt)
        lfn = jnp.where(rr < H_C, jnp.pad(gt3, ((0, 0), (0, 0), (0, LANES - ST))), 0.0)[:, 0:8]
        page_ids = (l * n_phys + page_table[:, ::-1]).reshape(-1)
        res = _mlp(xp, mab_p, hc_p, mod_p, g2, gf, wo_all, wu_all, wd_all, l, rows_p // SB, T, final,
                   sample=(page_ids, q8, knt, vnt, lfn, ckt, cvt, clf))
        xp = res[0]
        if final:
            yp = res[1]
        hc8 = res[-1]
        hc = hc8[:, 0:ST].reshape(rows_s, W_C).astype(BF16)
        res = _mlp(xs, mab, hc, mod_s, g2, gf, wo_all, wu_all, wd_all, l, rows_s, rows_s, final)
        xs = res[0]
        if final:
            ys = res[1]
        C1, n1, mm1 = _unpack_state(st1, m1)
        lf_s = jnp.transpose(gt3[:, 0:H_C, :], (0, 2, 1))
        pool_s = jnp.concatenate([state_pool[l], u.reshape(SB, ST, W_B)], axis=1)[:, -POOL_HIST:]
        outs_s.append((kc.reshape(SB, ST, H_C, DH_C), vc.reshape(SB, ST, H_C, DH_C), lf_s, C1, n1, mm1, pool_s))

    to_bthd = lambda a: jnp.transpose(a.reshape(depth, B, H_C, DH_C, T), (0, 1, 4, 2, 3))
    sp_ = [jnp.stack(a) for a in zip(*outs_p)]
    ss_ = [jnp.stack(a) for a in zip(*outs_s)]
    return (yp.reshape(B, T, D), ys.reshape(SB, ST, D), to_bthd(kv_all[0]), to_bthd(kv_all[1]), *sp_, *ss_)
```

```python
import functools

import numpy as np
import jax
import jax.numpy as jnp
from jax import lax
from jax.experimental import pallas as pl
from jax.experimental.pallas import tpu as pltpu

F32 = jnp.float32
BF16 = jnp.bfloat16

D_MODEL = 1024
H_A = 4
W_A = 384
DV_A = 96
DK_A = 48
W_B = 256
CG_B = 64
POOL_HIST = 15
POOL_WINDOWS = (2, 4, 8, 16)
DH_C = 64
W_C = 384
H_C = 6
D_FF = 4096
EPS = 1e-6
CHUNK = 128
LOG2E = 1.4426950408889634
NEG = -1e30

LANES = 128
QK_PAD = 256
ST_COLS = W_A + LANES
MIX_AB = W_A + W_B
AUG_ONE0 = DH_C
AUG_F0 = DH_C + 8
VMEM_LIMIT = 56 * 1024 * 1024
SEQ_GROUP = 2
ATTN_Q_TILE = 1024
ATTN_STRIP = 256
DEAD_LOG2 = 160.0
NORM_SLACK = 1.05
MLP_CHUNKS = 4
SAMPLE_Q = 4
SAMPLE_PAGES_PER_STEP = 16


def _dot(a, b):
    return jnp.dot(a, b, preferred_element_type=F32)


def _dot_nt(a, b):
    return lax.dot_general(a, b, (((1,), (1,)), ((), ())), preferred_element_type=F32)


def _dot_tn(a, b):
    return lax.dot_general(a, b, (((0,), (0,)), ((), ())), preferred_element_type=F32)


def _split3(x):
    a = x.astype(BF16)
    r = x - a.astype(F32)
    b = r.astype(BF16)
    c = (r - b.astype(F32)).astype(BF16)
    return a, b, c


def _dot3_rhs(m01, x):
    a, b, c = _split3(x)
    return _dot(m01, a) + _dot(m01, b) + _dot(m01, c)


def _dot3_lhs(x, m01):
    a, b, c = _split3(x)
    return _dot(a, m01) + _dot(b, m01) + _dot(c, m01)


def _log_sigmoid(x):
    return jnp.minimum(x, 0.0) - jnp.log1p(jnp.exp(-jnp.abs(x)))


def _sigmoid(x):
    return 1.0 / (1.0 + jnp.exp(-x))


def _iota(shape, dim):
    return lax.broadcasted_iota(jnp.int32, shape, dim)


def _expand_heads(src, width, per_head):
    lane = _iota((1, width), 1)
    out = jnp.zeros(src.shape[:-1] + (width,), F32)
    for h in range(H_A):
        out = jnp.where((lane >= h * per_head) & (lane < (h + 1) * per_head), src[:, h:h + 1], out)
    return out


def _mod_kernel(c_ref, w_ref, b_ref, o_ref):
    c = c_ref[...]
    s = (c * _sigmoid(c)).astype(BF16)
    o_ref[0] = _dot(s, w_ref[0].astype(BF16)) + b_ref[0]


def _modulation(c_all, w_ada, b_ada):
    depth, d, n6 = w_ada.shape
    rows = c_all.shape[0]
    tn = 1536
    return pl.pallas_call(
        _mod_kernel,
        grid=(depth, n6 // tn),
        in_specs=[
            pl.BlockSpec((rows, d), lambda l, j: (0, 0)),
            pl.BlockSpec((1, d, tn), lambda l, j: (l, 0, j)),
            pl.BlockSpec((1, 1, tn), lambda l, j: (l, 0, j)),
        ],
        out_specs=pl.BlockSpec((1, rows, tn), lambda l, j: (l, 0, j)),
        out_shape=jax.ShapeDtypeStruct((depth, rows, n6), F32),
        compiler_params=pltpu.CompilerParams(dimension_semantics=("arbitrary", "arbitrary"), vmem_limit_bytes=VMEM_LIMIT),
        name="modulation",
    )(c_all, w_ada, b_ada.reshape(depth, 1, n6))


NC_QA, NC_KA, NC_VA, NC_OA, NC_U, NC_G = 0, 256, 512, 896, 1280, 1536
N_TOKEN_MAJOR = 1664
TR_K, TR_V, TR_G, TR_Q = 0, 384, 768, 784
N_FEATURE_MAJOR = TR_Q + W_C
N_W_ROWS = N_TOKEN_MAJOR + N_FEATURE_MAJOR


def _norm_mod(x, g, scale, shift):
    ms = jnp.mean(x * x, axis=-1, keepdims=True)
    return (x * lax.rsqrt(ms + EPS) * g) * (1.0 + scale) + shift


def _gate_rows(graw):
    r = _iota(graw.shape, 0)
    ls = _log_sigmoid(graw)
    return jnp.where((r < H_C) | (r >= 12), ls, jnp.where(r < 8, 0.0, graw))


def _gate_cols(graw):
    ln = _iota(graw.shape, 1)
    return jnp.where(ln < H_A, _log_sigmoid(graw), jnp.where(ln < 2 * H_A, graw, 0.0))


def _inproj_common(x_ref, mod_ref, g_ref, w_ref, bn_ref, sn_ref, bt_ref, st_ref, qk_ref, va_ref, oa_ref, u_ref, gc_ref):
    h = _norm_mod(x_ref[...], g_ref[...], mod_ref[1, 0], mod_ref[0, 0])
    hb = h.astype(BF16)

    def seg(off, width):
        return (_dot_nt(hb, w_ref[off:off + width, :]) + bn_ref[:, off:off + width]) * sn_ref[:, off:off + width]

    def tseg(off, rows):
        return (_dot_nt(w_ref[N_TOKEN_MAJOR + off:N_TOKEN_MAJOR + off + rows, :], hb) + bt_ref[off:off + rows, :]) * st_ref[off:off + rows, :]

    qk_ref[...] = seg(NC_QA, 2 * QK_PAD).astype(BF16)
    va_ref[...] = seg(NC_VA, W_A).astype(BF16)
    oa_ref[...] = seg(NC_OA, W_A)
    u_ref[...] = seg(NC_U, W_B)
    gc_ref[...] = _gate_cols(seg(NC_G, LANES))
    return seg, tseg


def _inproj_prompt_kernel(x_ref, mod_ref, g_ref, w_ref, bn_ref, sn_ref, bt_ref, st_ref, tri_ref, *rest, tiles_per_seq, n_prev):
    prev = rest[0:2 * min(n_prev, 1)]
    (qk_ref, va_ref, oa_ref, u_ref, gc_ref, kaug_ref, qta_ref, vta_ref, kt_ref, vt_ref, gt_ref, ft_ref, nrm_ref, carry_ref) = rest[len(prev):]
    i = pl.program_id(0)
    tm = x_ref.shape[0]
    seg, tseg = _inproj_common(x_ref, mod_ref, g_ref, w_ref, bn_ref, sn_ref, bt_ref, st_ref, qk_ref, va_ref, oa_ref, u_ref, gc_ref)

    kt = tseg(TR_K, W_C)
    vt = tseg(TR_V, W_C)
    gt = _gate_rows(tseg(TR_G, 16))
    qt = tseg(TR_Q, W_C)
    if n_prev:
        kt_ref[0:n_prev, 0] = prev[0][:, 0]
        vt_ref[0:n_prev, 0] = prev[1][:, 0]
    kt_ref[n_prev, 0] = kt
    vt_ref[n_prev, 0] = vt
    gt_ref[0] = gt

    first = (i % tiles_per_seq) == 0
    carry = jnp.where(first, 0.0, carry_ref[:, 0:1])
    ft = _dot3_lhs(gt[0:8] * LOG2E, tri_ref[...]) + carry
    carry_ref[...] = jnp.broadcast_to(ft[:, tm - 1:tm], carry_ref.shape)
    ft_ref[0] = ft
    f1, f2, f3 = [p.astype(F32) for p in _split3(ft)]

    nr, nl = _iota((8, LANES), 0), _iota((8, LANES), 1)
    nrm = jnp.zeros((8, LANES), F32)
    for hh in range(H_C):
        for col, x in ((0, qt), (1, kt)):
            xh = x[hh * DH_C:(hh + 1) * DH_C]
            big = jnp.sqrt(jnp.max(jnp.sum(xh * xh, axis=0, keepdims=True), axis=1, keepdims=True))
            nrm = jnp.where((nr == hh) & (nl == col), big, nrm)
    nrm_ref[0, 0] = nrm

    r8 = _iota((8, tm), 0)
    ktail = jnp.concatenate([jnp.where(r8 < 3, 1.0, 0.0), -f1, -f2, -f3,
                             jnp.zeros((LANES - AUG_F0 - 24, tm), F32)], axis=0)
    r64 = _iota((LANES - DH_C, tm), 0)
    for hh in range(H_C):
        kaug_ref[hh] = jnp.concatenate([kt[hh * DH_C:(hh + 1) * DH_C], ktail], axis=0).T.astype(BF16)
        qa = jnp.where(r64 == 0, f1[hh:hh + 1], jnp.where(r64 == 1, f2[hh:hh + 1], jnp.where(r64 == 2, f3[hh:hh + 1], 0.0)))
        qa = jnp.where((r64 == 8 + hh) | (r64 == 16 + hh) | (r64 == 24 + hh), 1.0, qa)
        qta_ref[0, hh] = jnp.concatenate([qt[hh * DH_C:(hh + 1) * DH_C].astype(BF16), qa.astype(BF16)], axis=0)
        vta_ref[0, hh] = jnp.concatenate([vt[hh * DH_C:(hh + 1) * DH_C].astype(BF16),
                                          jnp.where(r64 == 0, 1.0, 0.0).astype(BF16)], axis=0)


def _inproj_sample_kernel(x_ref, mod_ref, g_ref, w_ref, bn_ref, sn_ref, bt_ref, st_ref,
                          qk_ref, va_ref, oa_ref, u_ref, gc_ref, qc_ref, kc_ref, vc_ref, kt_ref, vt_ref, gt_ref):
    _, tseg = _inproj_common(x_ref, mod_ref, g_ref, w_ref, bn_ref, sn_ref, bt_ref, st_ref, qk_ref, va_ref, oa_ref, u_ref, gc_ref)
    kt = tseg(TR_K, W_C)
    vt = tseg(TR_V, W_C)
    kt_ref[...] = kt
    vt_ref[...] = vt
    gt_ref[...] = _gate_rows(tseg(TR_G, 16))
    qc_ref[...] = tseg(TR_Q, W_C).T
    kc_ref[...] = kt.T
    vc_ref[...] = vt.T


def _inproj_weights(w_in, b_in):
    o = np.cumsum((0, H_A * DK_A, H_A * DK_A, W_A, H_A, H_A, W_A, W_B, W_C, W_C, W_C, H_C))
    wt_full = w_in.T
    d = w_in.shape[0]
    rows = lambda k: (wt_full[o[k]:o[k + 1]], b_in[o[k]:o[k + 1]])
    (wqa, bqa), (wka, bka), (wva, bva), (wia, bia), (wfa, bfa), (woa, boa), (wu, bu), (wqc, bqc), (wkc, bkc), (wvc, bvc), (wfc, bfc) = [rows(k) for k in range(11)]
    zw = lambda n: jnp.zeros((n, d), F32)
    zb = lambda n: jnp.zeros((n,), F32)
    ws = [wqa, zw(QK_PAD - 192), wka, zw(QK_PAD - 192), wva, woa, wu, wfa, wia, zw(LANES - 8)]
    bs = [bqa, zb(QK_PAD - 192), bka, zb(QK_PAD - 192), bva, boa, bu, bfa, bia, zb(LANES - 8)]
    sn =jnp.concatenate([jnp.ones((QK_PAD,), F32), jnp.full((QK_PAD,), DK_A ** -0.5, F32),
                          jnp.ones((N_TOKEN_MAJOR - 2 * QK_PAD,), F32)])[None, :]
    bn = jnp.concatenate(bs)[None, :]
    ws += [wkc, wvc, wfc, zw(2), wia, wfa, wqc]
    bt = jnp.concatenate([bkc, bvc, bfc, zb(2), bia, bfa, bqc])[:, None]
    st = jnp.concatenate([jnp.ones((TR_Q,), F32), jnp.full((W_C,), DH_C ** -0.5 * LOG2E, F32)])[:, None]
    return jnp.concatenate(ws, axis=0).astype(BF16), bn, sn, bt, st


def _inproj_prompt(x2, mod4, g1, weights, batch, seq, layer, kv_prev):
    w, bn, sn, bt, st = weights
    rows, d = x2.shape
    tm = min(512, seq)
    tps = seq // tm
    tri = jnp.asarray(np.triu(np.ones((tm, tm), np.float32)), BF16)
    full = lambda a: pl.BlockSpec(a.shape, lambda i: (0,) * a.ndim)
    rowblk = lambda wd: pl.BlockSpec((tm, wd), lambda i: (i, 0))
    tblk = lambda r: pl.BlockSpec((1, r, tm), lambda i: (i // tps, 0, i % tps))
    tblk4 = pl.BlockSpec((1, H_C, LANES, tm), lambda i: (i // tps, 0, 0, i % tps))
    kvblk = lambda n: pl.BlockSpec((n, 1, W_C, tm), lambda i: (0, i // tps, 0, i % tps))
    prev_in = [] if kv_prev is None else list(kv_prev)
    outs = pl.pallas_call(
        functools.partial(_inproj_prompt_kernel, tiles_per_seq=tps, n_prev=layer),
        grid=(rows // tm,),
        in_specs=[rowblk(d), pl.BlockSpec((6, 1, 1, d), lambda i: (0, i // tps, 0, 0)), full(g1),
                  full(w), full(bn), full(sn), full(bt), full(st), full(tri)] + [kvblk(layer)] * len(prev_in),
        out_specs=[rowblk(2 * QK_PAD), rowblk(W_A), rowblk(W_A), rowblk(W_B), rowblk(LANES),
                   pl.BlockSpec((H_C, tm, LANES), lambda i: (0, i, 0)), tblk4, tblk4, kvblk(layer + 1), kvblk(layer + 1), tblk(16),
                   tblk(8), pl.BlockSpec((1, 1, 8, LANES), lambda i: (i // tps, i % tps, 0, 0))],
        out_shape=[jax.ShapeDtypeStruct((rows, 2 * QK_PAD), BF16), jax.ShapeDtypeStruct((rows, W_A), BF16),
                   jax.ShapeDtypeStruct((rows, W_A), F32), jax.ShapeDtypeStruct((rows, W_B), F32),
                   jax.ShapeDtypeStruct((rows, LANES), F32),
                   jax.ShapeDtypeStruct((H_C, rows, LANES), BF16),
                   jax.ShapeDtypeStruct((batch, H_C, LANES, seq), BF16), jax.ShapeDtypeStruct((batch, H_C, LANES, seq), BF16),
                   jax.ShapeDtypeStruct((layer + 1, batch, W_C, seq), F32), jax.ShapeDtypeStruct((layer + 1, batch, W_C, seq), F32),
                   jax.ShapeDtypeStruct((batch, 16, seq), F32),
                   jax.ShapeDtypeStruct((batch, 8, seq), F32), jax.ShapeDtypeStruct((batch, tps, 8, LANES), F32)],
        scratch_shapes=[pltpu.VMEM((8, LANES), F32)],
        compiler_params=pltpu.CompilerParams(dimension_semantics=("arbitrary",), vmem_limit_bytes=VMEM_LIMIT),
        name="inproj_prompt",
    )(x2, mod4, g1, w, bn, sn, bt, st, tri, *prev_in)
    return outs


def _inproj_sample(x2, mod4, g1, weights):
    w, bn, sn, bt, st = weights
    rows, d = x2.shape
    full = lambda a: pl.BlockSpec(a.shape, lambda i: (0,) * a.ndim)
    o2 = lambda r, c: pl.BlockSpec((r, c), lambda i: (0, 0))
    sds = lambda r, c, t: jax.ShapeDtypeStruct((r, c), t)
    return pl.pallas_call(
        _inproj_sample_kernel,
        grid=(1,),
        in_specs=[full(x2), full(mod4), full(g1), full(w), full(bn), full(sn), full(bt), full(st)],
        out_specs=[o2(rows, 2 * QK_PAD), o2(rows, W_A), o2(rows, W_A), o2(rows, W_B), o2(rows, LANES),
                   o2(rows, W_C), o2(rows, W_C), o2(rows, W_C), o2(W_C, rows), o2(W_C, rows), o2(16, rows)],
        out_shape=[sds(rows, 2 * QK_PAD, BF16), sds(rows, W_A, BF16), sds(rows, W_A, F32), sds(rows, W_B, F32), sds(rows, LANES, F32),
                   sds(rows, W_C, F32), sds(rows, W_C, F32), sds(rows, W_C, F32), sds(W_C, rows, F32), sds(W_C, rows, F32), sds(16, rows, F32)],
        compiler_params=pltpu.CompilerParams(dimension_semantics=("arbitrary",), vmem_limit_bytes=VMEM_LIMIT),
        name="inproj_sample",
    )(x2, mod4, g1, w, bn, sn, bt, st)


def _state_mask():
    r = np.arange(QK_PAD)[:, None]
    c = np.arange(ST_COLS)[None, :]
    m = np.zeros((QK_PAD, ST_COLS), np.float32)
    for h in range(H_A):
        rows = (r >= h * DK_A) & (r < (h + 1) * DK_A)
        cols = ((c >= h * DV_A) & (c < (h + 1) * DV_A)) | (c == W_A + h)
        m[rows & cols] = 1.0
    return m


def _seq_kernel(*refs, pos0, group, interleave):
    chains = [_seq_one(g, *refs, pos0=pos0) for g in range(group)]
    if not interleave:
        for ch in chains:
            for _ in ch:
                pass
        return
    for lead in range(group - 1):
        for ch in chains[:group - 1 - lead]:
            next(ch)
    while chains:
        chains = [ch for ch in chains if next(ch, _DONE) is not _DONE]


_DONE = object()


def _seq_one(g, qk_ref, v_ref, o_ref, u_ref, gc_ref, gt_ref, st0_ref, m0_ref, hist0_ref, gh_ref, wp_ref, sp_ref, mask_ref,
             mix_ref, st_out_ref, m_out_ref, hist_out_ref, st_scr, m_scr, z_scr, *, pos0):
    c = pl.program_id(1)
    L = CHUNK

    @pl.when(c == 0)
    def _():
        st_scr[g] = st0_ref[g]
        m_scr[g] = m0_ref[g]
        z_scr[g, 0:16, :] = hist0_ref[g]

    q = qk_ref[g, :, 0:QK_PAD]
    k = qk_ref[g, :, QK_PAD:2 * QK_PAD]
    v = v_ref[g]
    gc = gc_ref[g]
    gt = gt_ref[g]
    st = st_scr[g]
    m_row = m_scr[g, 0:1, :]

    row = _iota((L, L), 0)
    colm = _iota((L, L), 1)
    tri_l = jnp.where(colm <= row, 1.0, 0.0).astype(BF16)
    tri_u = jnp.where(row <= colm, 1.0, 0.0).astype(BF16)
    lane128 = _iota((1, LANES), 1)
    lane_q = _iota((1, QK_PAD), 1)
    lane_v = _iota((1, W_A), 1)

    u = u_ref[g]
    z_scr[g, 16:16 + L, :] = u
    z = z_scr[g]
    s2 = z + pltpu.roll(z, 1, axis=0)
    s4 = s2 + pltpu.roll(s2, 2, axis=0)
    s8 = s4 + pltpu.roll(s4, 4, axis=0)
    s16 = s8 + pltpu.roll(s8, 8, axis=0)
    navail = (pos0 + c * L + 1 + _iota((L, 1), 0)).astype(F32)
    lane_u = _iota((1, W_B), 1)
    y = None
    for gi, (w, sw) in enumerate(zip(POOL_WINDOWS, (s2, s4, s8, s16))):
        yg = sw[16:16 + L, :] * (1.0 / jnp.minimum(float(w), navail))
        y = yg if y is None else jnp.where(lane_u >= gi * CG_B, yg, y)
    y = y - u
    hb = _dot(y.astype(BF16), wp_ref[...]) * sp_ref[...]
    z_scr[g, 0:16, :] = u[L - 16:L, :]

    bc_col = _dot3_rhs(tri_l, gc)
    bc_row = _dot3_lhs(gt[8:16], tri_u)
    qk_h = [_dot_nt(jnp.where((lane_q >= h * DK_A) & (lane_q < (h + 1) * DK_A), q, jnp.zeros_like(q)), k) for h in range(H_A)]
    qs = _dot(q, st.astype(BF16))
    yield
    inter = bc_col + m_row
    ia_col = pltpu.roll(gc, LANES - H_A, axis=1)

    causal = colm <= row
    mt_all = jnp.zeros((L, LANES), F32)
    s_list = []
    vblk = []
    for h in range(H_A):
        d = bc_col[:, h:h + 1] - bc_row[4 + h:5 + h, :] + gt[8 + h:9 + h, :]
        d = jnp.where(causal, d, NEG)
        mt = jnp.maximum(inter[:, h:h + 1], jnp.max(d, axis=-1, keepdims=True))
        s = qk_h[h] * jnp.exp(d - mt)
        s_list.append(s.astype(BF16))
        mt_all = jnp.where(lane128 == h, mt, mt_all)
        vm = jnp.where((lane_v >= h * DV_A) & (lane_v < (h + 1) * DV_A), v, jnp.zeros_like(v))
        vblk.append(jnp.concatenate([vm, jnp.broadcast_to(jnp.where(lane128 == h, 1.0, 0.0).astype(BF16), (L, LANES))], axis=1))
    s_cat = jnp.concatenate(s_list, axis=1)
    v_blk = jnp.concatenate(vblk, axis=0)
    pv = _dot(s_cat, v_blk)
    yield

    valid = lane128 < H_A
    m_new = mt_all[L - 1:L, :]
    g_col = jnp.where(valid, jnp.exp(bc_col[L - 1:L, :] - bc_col + ia_col - m_new), 0.0)
    a_last = jnp.where(valid, jnp.exp(inter[L - 1:L, :] - m_new), 0.0)
    gv = (v.astype(F32) * _expand_heads(g_col, W_A, DV_A)).astype(BF16)
    upd = _dot_tn(k, jnp.concatenate([gv, g_col.astype(BF16)], axis=1))
    yield
    a512 = jnp.concatenate([_expand_heads(a_last, W_A, DV_A), a_last], axis=1)
    st_new = a512 * st + mask_ref[...] * upd
    st_scr[g] = st_new
    m_rows = jnp.broadcast_to(jnp.where(valid, m_new, 0.0), (8, LANES))
    m_scr[g] = m_rows

    a_all = jnp.where(valid, jnp.exp(inter - mt_all), 0.0)
    num = _expand_heads(a_all, W_A, DV_A) * qs[:, 0:W_A] + pv[:, 0:W_A]
    den = a_all * qs[:, W_A:] + pv[:, W_A:]
    inv = 1.0 / jnp.maximum(jnp.abs(den), jnp.exp(-mt_all))
    hs = num * _expand_heads(jnp.where(valid, inv, 0.0), W_A, DV_A)

    sq = hs * hs
    ss_all = jnp.zeros((L, LANES), F32)
    for h in range(H_A):
        ssum = jnp.sum(jnp.where((lane_v >= h * DV_A) & (lane_v < (h + 1) * DV_A), sq, 0.0), axis=-1, keepdims=True)
        ss_all = jnp.where(lane128 == h, ssum, ss_all)
    r_all = lax.rsqrt(ss_all * (1.0 / DV_A) + EPS)
    ha = _sigmoid(o_ref[g]) * hs * _expand_heads(r_all, W_A, DV_A) * gh_ref[...]
    mix_ref[g] = jnp.concatenate([ha, hb], axis=1).astype(BF16)

    @pl.when(c == pl.num_programs(1) - 1)
    def _():
        st_out_ref[g] = st_new
        m_out_ref[g] = m_rows
        hist_out_ref[g] = u[L - 16:L, :]


def _sequence(qk, va, oa, u, gc, gt, st0, m0, hist0, gh, wp, sp, nb, nchunk, pos0):
    L = CHUNK
    G = SEQ_GROUP
    assert nb % G == 0
    mask = jnp.asarray(_state_mask())
    seq3 = lambda a: a.reshape(nb, nchunk * L, a.shape[-1])
    rowblk = lambda w: pl.BlockSpec((G, L, w), lambda b, c: (b, c, 0))
    perb = lambda a: pl.BlockSpec((G,) + a.shape[1:], lambda b, c: (b,) + (0,) * (a.ndim - 1))
    full = lambda a: pl.BlockSpec(a.shape, lambda b, c: (0,) * a.ndim)
    outs = pl.pallas_call(
        functools.partial(_seq_kernel, pos0=pos0, group=G, interleave=nchunk == 1),
        grid=(nb // G, nchunk),
        in_specs=[rowblk(2 * QK_PAD), rowblk(W_A), rowblk(W_A), rowblk(W_B), rowblk(LANES),
                  pl.BlockSpec((G, 16, L), lambda b, c: (b, 0, c)),
                  perb(st0), perb(m0), perb(hist0), full(gh), full(wp), full(sp), full(mask)],
        out_specs=[rowblk(MIX_AB), pl.BlockSpec((G, QK_PAD, ST_COLS), lambda b, c: (b, 0, 0)),
                   pl.BlockSpec((G, 8, LANES), lambda b, c: (b, 0, 0)), pl.BlockSpec((G, 16, W_B), lambda b, c: (b, 0, 0))],
        out_shape=[jax.ShapeDtypeStruct((nb, nchunk * L, MIX_AB), BF16), jax.ShapeDtypeStruct((nb, QK_PAD, ST_COLS), F32),
                   jax.ShapeDtypeStruct((nb, 8, LANES), F32), jax.ShapeDtypeStruct((nb, 16, W_B), F32)],
        scratch_shapes=[pltpu.VMEM((G, QK_PAD, ST_COLS), F32), pltpu.VMEM((G, 8, LANES), F32), pltpu.VMEM((G, 16 + L, W_B), F32)],
        compiler_params=pltpu.CompilerParams(dimension_semantics=("arbitrary", "arbitrary"), vmem_limit_bytes=VMEM_LIMIT),
        name="sequence",
    )(seq3(qk), seq3(va), seq3(oa), seq3(u), seq3(gc), gt, st0, m0, hist0, gh, wp, sp, mask)
    return (outs[0].reshape(nb * nchunk * L, MIX_AB),) + tuple(outs[1:])


def _pack_state(C, n, m):
    nb = C.shape[0]
    z = lambda r, c: jnp.zeros((nb, r, c), F32)
    blocks = []
    for h in range(H_A):
        blocks.append(jnp.concatenate([z(DK_A, h * DV_A), jnp.swapaxes(C[:, h], 1, 2), z(DK_A, W_A - (h + 1) * DV_A),
                                       z(DK_A, h), n[:, h][:, :, None], z(DK_A, LANES - h - 1)], axis=2))
    st = jnp.concatenate(blocks + [z(QK_PAD - H_A * DK_A, ST_COLS)], axis=1)
    mm = jnp.concatenate([jnp.broadcast_to(m[:, None, :], (nb, 8, H_A)), z(8, LANES - H_A)], axis=2)
    return st, mm


def _unpack_state(st, mm):
    C = jnp.stack([jnp.swapaxes(st[:, h * DK_A:(h + 1) * DK_A, h * DV_A:(h + 1) * DV_A], 1, 2) for h in range(H_A)], axis=1)
    n = jnp.stack([st[:, h * DK_A:(h + 1) * DK_A, W_A + h] for h in range(H_A)], axis=1)
    return C, n, mm[:, 0, 0:H_A]


def _attn_prompt_kernel(skip_ref, qt_ref, k_ref, vt_ref, o_ref, s_scr, m_scr, acc_scr, *, tq, tk, sw):
    qi = pl.program_id(2)
    m_scr[...] = jnp.full(m_scr.shape, NEG, F32)
    acc_scr[...] = jnp.zeros(acc_scr.shape, F32)
    units = [(hh, st) for hh in range(2) for st in range(tq // sw)]

    def qk_block(j, slot, skip=()):
        start = pl.multiple_of(j * tk, tk)
        for hh, st in units:
            if st not in skip:
                s_scr[slot, hh, :, st * sw:(st + 1) * sw] = _dot(k_ref[hh, pl.ds(start, tk), :], qt_ref[0, hh, :, st * sw:(st + 1) * sw])

    def softmax_pv_block(j, slot, masked=(), skip=()):
        start = pl.multiple_of(j * tk, tk)
        for hh, st in units:
            if st in skip:
                continue
            strip = slice(st * sw, (st + 1) * sw)
            s = s_scr[slot, hh, :, strip]
            if st in masked:
                keyg = j * tk + _iota((tk, sw), 0)
                qryg = qi * tq + st * sw + _iota((tk, sw), 1)
                s = jnp.where(keyg <= qryg, s, NEG)
            m = m_scr[hh, 0:1, strip]
            m_new = jnp.maximum(m, jnp.max(s, axis=0, keepdims=True))
            p = jnp.exp2(s - m_new)
            pv = _dot(vt_ref[0, hh, :, pl.ds(start, tk)], p.astype(BF16))
            acc_scr[hh, :, strip] = jnp.exp2(m - m_new) * acc_scr[hh, :, strip] + pv
            m_scr[hh, 0:1, strip] = m_new

    n_full = (qi * tq) // tk
    first_pair = jnp.minimum(skip_ref[(pl.program_id(0) * pl.num_programs(1) + pl.program_id(1)) * pl.num_programs(2) + qi],
                             n_full // 2)
    qk_block(2 * first_pair, 0)

    def body(i, carry):
        j = 2 * i
        qk_block(j + 1, 1)
        softmax_pv_block(j, 0)
        qk_block(j + 2, 0)
        softmax_pv_block(j + 1, 1)
        return carry

    lax.fori_loop(first_pair, n_full // 2, body, 0)
    n_diag = tq // tk
    for d in range(n_diag):
        if d + 1 < n_diag:
            qk_block(n_full + d + 1, (d + 1) % 2, skip=tuple(range(d + 1)))
        softmax_pv_block(n_full + d, d % 2, masked=(d,), skip=tuple(range(d)))
    outs = []
    for hh in range(2):
        acc = acc_scr[hh]
        outs.append((acc * (1.0 / acc[DH_C:DH_C + 1, :])).T)
    lane = _iota((1, LANES), 1)
    o_ref[...] = jnp.where(lane < DH_C, outs[0], pltpu.roll(outs[1], DH_C, axis=1)).astype(BF16)


def _dead_block_pairs(ft, qk_norm, seq, tq, tk):
    nb = ft.shape[0]
    nq, nk = seq // tq, seq // tk
    tiles = qk_norm.shape[1]
    qmax = jnp.max(qk_norm[:, :, 0:H_C, 0].reshape(nb, nq, tiles // nq, H_C), axis=2)
    kmax = jnp.max(qk_norm[:, :, 0:H_C, 1], axis=1)
    f_start = jnp.transpose(ft[:, 0:H_C, 0::tq], (0, 2, 1))
    f_end = jnp.transpose(ft[:, 0:H_C, tk - 1::tk], (0, 2, 1))
    bound = (NORM_SLACK * 2.0) * (qmax * kmax[:, None, :])[:, :, None, :] + f_start[:, :, None, :] - f_end[:, None, :, :]
    dead = bound < -DEAD_LOG2
    dead = jnp.all(dead.reshape(nb, nq, nk // 2, 2, H_C // 2, 2), axis=(3, 5))
    n_dead = jnp.sum(jnp.cumprod(dead.astype(jnp.int32), axis=2), axis=2)
    return jnp.transpose(n_dead, (0, 2, 1)).reshape(-1).astype(jnp.int32)


def _attn_prompt(qta, k_aug, vta, ft, qk_norm, batch, seq):
    rows = batch * seq
    tk = sw = ATTN_STRIP
    tq = min(ATTN_Q_TILE, seq)
    assert seq % tq == 0 and (tq // tk) % 2 == 0
    nq = seq // tq
    grid_spec = pltpu.PrefetchScalarGridSpec(
        num_scalar_prefetch=1, grid=(batch, H_C // 2, nq),
        in_specs=[pl.BlockSpec((1, 2, LANES, tq), lambda b, p, i, sk: (b, p, 0, i)),
                  pl.BlockSpec((2, seq, LANES), lambda b, p, i, sk: (p, b, 0)),
                  pl.BlockSpec((1, 2, LANES, seq), lambda b, p, i, sk: (b, p, 0, 0))],
        out_specs=pl.BlockSpec((tq, LANES), lambda b, p, i, sk: (b * nq + i, p)),
        scratch_shapes=[pltpu.VMEM((2, 2, tk, tq), F32), pltpu.VMEM((2, 8, tq), F32), pltpu.VMEM((2, LANES, tq), F32)])
    return pl.pallas_call(
        functools.partial(_attn_prompt_kernel, tq=tq, tk=tk, sw=sw),
        grid_spec=grid_spec,
        out_shape=jax.ShapeDtypeStruct((rows, W_C), BF16),
        compiler_params=pltpu.CompilerParams(dimension_semantics=("arbitrary", "arbitrary", "arbitrary"), vmem_limit_bytes=VMEM_LIMIT),
        name="attn_prompt",
    )(_dead_block_pairs(ft, qk_norm, seq, tq, tk), qta, k_aug, vta)


def _when(cond):
    if isinstance(cond, bool):
        return (lambda f: f()) if cond else (lambda f: None)
    return pl.when(cond)


def _sample_attn_step(t, n_total, first, last, P, pt_ref, q_ref, knt_ref, vnt_ref, lfn_ref, ck_hbm, cv_hbm, clf_hbm, o_ref,
                      kbuf, vbuf, lbuf, sems, m_scr, acc_scr, carry_scr, cq_scr):
    slot = t % 2

    def page_copies(step, sl):
        out = []
        for i in range(P):
            pg = pt_ref[step * P + i]
            out += [pltpu.make_async_copy(ck_hbm.at[pg], kbuf.at[sl, i], sems.at[sl, 0]),
                    pltpu.make_async_copy(cv_hbm.at[pg], vbuf.at[sl, i], sems.at[sl, 1]),
                    pltpu.make_async_copy(clf_hbm.at[pg], lbuf.at[sl, i], sems.at[sl, 2])]
        return out

    @_when(first and (t == 0))
    def _():
        for cp in page_copies(0, 0):
            cp.start()

    @pl.when(t + 1 < n_total)
    def _():
        for cp in page_copies(t + 1, 1 - slot):
            cp.start()

    for cp in page_copies(t, slot):
        cp.wait()

    R = 8 * SAMPLE_Q
    head_of_lane = lax.shift_right_logical(_iota((R, W_C), 1), 6)
    headmask = head_of_lane == (_iota((R, W_C), 0) & 7)
    lane = _iota((R, LANES), 1)
    rq = lax.shift_right_logical(_iota((R, LANES), 0), 3)

    def tile_q(x8):
        return jnp.concatenate([x8] * SAMPLE_Q, axis=0)

    q8 = q_ref[0]
    qrep = jnp.concatenate([jnp.broadcast_to(q8[qq:qq + 1], (8, W_C)) for qq in range(SAMPLE_Q)], axis=0)
    qbd = jnp.where(headmask, qrep, 0.0).astype(BF16)

    @_when(first)
    def _():
        a = tile_q(lfn_ref[0] * LOG2E)
        cq = jnp.sum(jnp.where(lane <= rq, a, 0.0), axis=-1, keepdims=True)
        incl = jnp.where(_iota((LANES, LANES), 0) <= _iota((LANES, LANES), 1), 1.0, 0.0).astype(BF16)
        crow = _dot3_lhs(a, incl)
        s = _dot(qbd, knt_ref[0].astype(BF16)) + cq - crow
        s = jnp.where((lane <= rq) & (lane < SAMPLE_Q), s, NEG)
        m = jnp.max(s, axis=-1, keepdims=True)
        p = jnp.exp2(s - m)
        l = jnp.sum(p, axis=-1, keepdims=True)
        acc_scr[:, 0:W_C] = _dot_nt(p.astype(BF16), vnt_ref[0].astype(BF16))
        acc_scr[:, W_C:] = jnp.broadcast_to(l, (R, LANES))
        m_scr[...] = jnp.broadcast_to(m, (R, LANES))
        cq_scr[...] = jnp.broadcast_to(cq, (R, LANES))
        carry_scr[...] = jnp.zeros_like(carry_scr)

    cq = cq_scr[:, 0:1]
    strict = jnp.where(_iota((LANES, LANES), 0) > _iota((LANES, LANES), 1), 1.0, 0.0).astype(BF16)
    lf_all = lbuf[slot].reshape(8 * P, LANES) * LOG2E
    suf = _dot3_lhs(lf_all, strict)
    carry = carry_scr[:, 0:1]
    bias = [None] * P
    for i in range(P):
        bias[i] = tile_q(suf[8 * i:8 * i + 8] + carry)
        carry = carry + suf[8 * i:8 * i + 8, 0:1] + lf_all[8 * i:8 * i + 8, 0:1]
    carry_scr[...] = jnp.broadcast_to(carry, carry_scr.shape)

    kcat = jnp.concatenate([kbuf[slot, i].astype(BF16) for i in range(P)], axis=1)
    vcat = jnp.concatenate([vbuf[slot, i].astype(BF16) for i in range(P)], axis=1)
    s = _dot(qbd, kcat) + jnp.concatenate(bias, axis=1) + cq
    yield
    m_old = m_scr[:, 0:1]
    m_new = jnp.maximum(m_old, jnp.max(s, axis=-1, keepdims=True))
    alpha = jnp.exp2(m_old - m_new)
    p = jnp.exp2(s - m_new)
    l = alpha * acc_scr[:, W_C:W_C + 1] + jnp.sum(p, axis=-1, keepdims=True)
    acc = alpha * acc_scr[:, 0:W_C] + _dot_nt(p.astype(BF16), vcat)
    acc_scr[:, 0:W_C] = acc
    acc_scr[:, W_C:] = jnp.broadcast_to(l, (R, LANES))
    m_scr[...] = jnp.broadcast_to(m_new, (R, LANES))

    @_when(last)
    def _():
        o = jnp.where(headmask, acc * (1.0 / l), 0.0)
        rows = [jnp.sum(o[8 * qq:8 * qq + 8], axis=0, keepdims=True) for qq in range(SAMPLE_Q)]
        o_ref[0] = jnp.concatenate(rows + [jnp.zeros((8 - SAMPLE_Q, W_C), F32)], axis=0)


def _sample_attn_scratch(P):
    R = 8 * SAMPLE_Q
    return [pltpu.VMEM((2, P, W_C, LANES), F32), pltpu.VMEM((2, P, W_C, LANES), F32), pltpu.VMEM((2, P, 8, LANES), F32),
            pltpu.SemaphoreType.DMA((2, 3)),
            pltpu.VMEM((R, LANES), F32), pltpu.VMEM((R, W_C + LANES), F32), pltpu.VMEM((8, LANES), F32), pltpu.VMEM((R, LANES), F32)]


def _mlp_kernel(*refs, final, n_groups, pages_per_group):
    fused = n_groups > 0
    if fused:
        pt_ref, refs = refs[0], refs[1:]
    x_ref, mab_ref, hc_ref, mod_ref, g2_ref, gf_ref, wo_ref, wu_ref, wd_ref = refs[0:9]
    rest = refs[9:]
    if fused:
        sample_in, rest = rest[0:7], rest[7:]
    n_out = (2 if final else 1) + (1 if fused else 0)
    out_refs, scratch = rest[0:n_out], rest[n_out:]

    x = x_ref[...]
    mix = _dot(mab_ref[...], wo_ref[0, 0:MIX_AB, :]) + _dot(hc_ref[...], wo_ref[0, MIX_AB:, :])
    x1 = x + mod_ref[2, 0] * mix
    h2 = _norm_mod(x1, g2_ref[...], mod_ref[4, 0], mod_ref[3, 0]).astype(BF16)
    ff = jnp.zeros_like(x)
    n_chunks = n_groups if fused else MLP_CHUNKS
    fc = D_FF // n_chunks
    for f in range(n_chunks):
        a = jnp.maximum(_dot(h2, wu_ref[0, :, f * fc:(f + 1) * fc]), 0.0)
        if fused:
            step = _sample_attn_step(pl.program_id(0) * n_groups + f, pl.num_programs(0) * n_groups, f == 0, f == n_groups - 1,
                                     pages_per_group, pt_ref, *sample_in, out_refs[n_out - 1], *scratch)
            next(step)
        ff = ff + _dot((a * a).astype(BF16), wd_ref[0, f * fc:(f + 1) * fc, :])
        if fused:
            next(step, None)
    x2 = x1 + mod_ref[5, 0] * ff
    out_refs[0][...] = x2
    if final:
        ms = jnp.mean(x2 * x2, axis=-1, keepdims=True)
        out_refs[1][...] = x2 * lax.rsqrt(ms + EPS) * gf_ref[...]


def _mlp(x2, mab, hc, mod4, g2, gf, wo, wu, wd, layer, tm, rows_per_mod, final, sample=None):
    rows, d = x2.shape
    s = mod4.shape[2]
    fused = sample is not None
    im = (lambda f: (lambda i, pt: f(i))) if fused else (lambda f: f)
    const = lambda a: pl.BlockSpec(a.shape, im(lambda i: (0,) * a.ndim), pipeline_mode=pl.Buffered(1))
    layerw = lambda a: pl.BlockSpec((1,) + a.shape[1:], im(lambda i: (layer, 0, 0)), pipeline_mode=pl.Buffered(1))
    rowblk = lambda w: pl.BlockSpec((tm, w), im(lambda i: (i, 0)))
    n_out = 2 if final else 1
    in_specs = [rowblk(d), rowblk(MIX_AB), rowblk(W_C),
                pl.BlockSpec((6, 1, s, d), im(lambda i: (0, (i * tm) // rows_per_mod, 0, 0))),
                const(g2), const(gf), layerw(wo), layerw(wu), layerw(wd)]
    out_specs = [rowblk(d)] * n_out
    out_shape = [jax.ShapeDtypeStruct((rows, d), F32)] * n_out
    args = [x2, mab, hc, mod4, g2, gf, wo, wu, wd]
    scratch, n_groups, P, n_prefetch = [], 0, 0, 0
    if fused:
        page_ids, q8, knt, vnt, lfn = sample[0:5]
        nb, n_pages = q8.shape[0], page_ids.shape[0] // q8.shape[0]
        assert nb == rows // tm
        P = min(SAMPLE_PAGES_PER_STEP, n_pages)
        n_groups = n_pages // P
        assert D_FF % (n_groups * LANES) == 0
        perb = lambda a: pl.BlockSpec((1,) + a.shape[1:], lambda i, pt: (i, 0, 0))
        in_specs += [perb(q8), perb(knt), perb(vnt), perb(lfn)] + [pl.BlockSpec(memory_space=pl.ANY)] * 3
        out_specs += [pl.BlockSpec((1, 8, W_C), lambda i, pt: (i, 0, 0))]
        out_shape += [jax.ShapeDtypeStruct((nb, 8, W_C), F32)]
        args = [page_ids] + args + list(sample[1:])
        scratch, n_prefetch = _sample_attn_scratch(P), 1
    grid_spec = pltpu.PrefetchScalarGridSpec(num_scalar_prefetch=n_prefetch, grid=(rows // tm,), in_specs=in_specs,
                                             out_specs=out_specs, scratch_shapes=scratch)
    return pl.pallas_call(
        functools.partial(_mlp_kernel, final=final, n_groups=n_groups, pages_per_group=P),
        grid_spec=grid_spec,
        out_shape=out_shape,
        compiler_params=pltpu.CompilerParams(dimension_semantics=("arbitrary",), vmem_limit_bytes=VMEM_LIMIT),
        name="mlp_attn_sample" if fused else "mlp",
    )(*args)


def kernel(x_prompt, x_sample, c_prompt, c_sample, cache_k, cache_v, cache_logf, page_table, state_C, state_n, state_m, state_pool, w_ada, b_ada, g_norm1, g_norm2, w_in, b_in, g_head_a, w_pool, s_pool, w_out, w_up, w_down, g_final):
    depth = w_ada.shape[0]
    B, T, D = x_prompt.shape
    SB, ST, _ = x_sample.shape
    n_phys, page = cache_k.shape[1], cache_k.shape[2]
    n_pages = page_table.shape[1]
    rows_p, rows_s = B * T, SB * ST
    pos0_s = n_pages * page
    assert T % CHUNK == 0 and ST == SAMPLE_Q and page == LANES

    nc = B + SB
    c_all = jnp.concatenate([c_prompt, c_sample, jnp.zeros((-nc % 8, D), F32)], axis=0)
    mod = _modulation(c_all, w_ada, b_ada)

    ckt = jnp.transpose(cache_k, (0, 1, 3, 4, 2)).reshape(depth * n_phys, W_C, page)
    cvt = jnp.transpose(cache_v, (0, 1, 3, 4, 2)).reshape(depth * n_phys, W_C, page)
    clf = jnp.pad(jnp.transpose(cache_logf, (0, 1, 3, 2)), ((0, 0), (0, 0), (0, 8 - H_C), (0, 0))).reshape(depth * n_phys, 8, page)
    wo_all, wu_all, wd_all = w_out.astype(BF16), w_up.astype(BF16), w_down.astype(BF16)

    xp = x_prompt.reshape(rows_p, D)
    xs = x_sample.reshape(rows_s, D)
    zeros_state = (jnp.zeros((B, QK_PAD, ST_COLS), F32), jnp.zeros((B, 8, LANES), F32))
    outs_p, outs_s = [], []
    yp = ys = None
    kv_all = None
    for l in range(depth):
        final = l == depth - 1
        g1, g2, gf = g_norm1[l][None, :], g_norm2[l][None, :], g_final[None, :]
        gh = g_head_a[l].reshape(1, W_A)
        wp = jax.scipy.linalg.block_diag(*[w_pool[l, g] for g in range(len(POOL_WINDOWS))]).astype(BF16)
        sp = s_pool[l][None, :]
        weights = _inproj_weights(w_in[l], b_in[l])
        modl = mod[l].reshape(-1, 6, D)
        mod_p = jnp.transpose(modl[0:B], (1, 0, 2))[:, :, None, :]
        mod_s = jnp.transpose(jnp.repeat(modl[B:B + SB], ST, axis=0), (1, 0, 2))[:, None]

        (qk, va, oa, u, gc, k_aug, qta, vta, kt_all, vt_all, gt, ft, qk_norm) = _inproj_prompt(xp, mod_p, g1, weights, B, T, l, kv_all)
        kv_all = (kt_all, vt_all)
        mab, st1, m1, hist1 = _sequence(qk, va, oa, u, gc, gt, zeros_state[0], zeros_state[1], jnp.zeros((B, 16, W_B), F32),
                                        gh, wp, sp, B, T // CHUNK, 0)
        hc = _attn_prompt(qta, k_aug, vta, ft, qk_norm, B, T)
        mab_p, hc_p = mab, hc
        C1, n1, mm1 = _unpack_state(st1, m1)
        outs_p.append((jnp.transpose(gt[:, 0:H_C, :], (0, 2, 1)), C1, n1, mm1, hist1[:, 1:]))

        (qk, va, oa, u, gc, qc, kc, vc, kt, vt, gt) = _inproj_sample(xs, mod_s, g1, weights)

        def pad_rows(a, fill=0.0):
            a3 = a.reshape(SB, ST, a.shape[-1])
            return jnp.pad(a3, ((0, 0), (0, CHUNK - ST), (0, 0)), constant_values=fill).reshape(SB * CHUNK, a.shape[-1])

        lane = jnp.arange(LANES)[None, :]
        gc_pad = jnp.where((lane >= H_A) & (lane < 2 * H_A), pad_rows(gc, NEG), pad_rows(gc))
        gt3 = jnp.transpose(gt.reshape(16, SB, ST), (1, 0, 2))
        rr = jnp.arange(16)[None, :, None]
        gt_pad = jnp.where((rr >= 8) & (rr < 12), jnp.pad(gt3, ((0, 0), (0, 0), (0, CHUNK - ST)), constant_values=NEG),
                           jnp.pad(gt3, ((0, 0), (0, 0), (0, CHUNK - ST))))
        st0, m0 = _pack_state(state_C[l], state_n[l], state_m[l])
        hist0 = jnp.pad(state_pool[l], ((0, 0), (1, 0), (0, 0)))
        mab, st1, m1, _ = _sequence(pad_rows(qk), pad_rows(va), pad_rows(oa), pad_rows(u), gc_pad, gt_pad, st0, m0, hist0,
                                    gh, wp, sp, SB, 1, pos0_s)
        mab = mab.reshape(SB, CHUNK, MIX_AB)[:, 0:ST].reshape(rows_s, MIX_AB)
        q8 = jnp.pad(qc.reshape(SB, ST, W_C), ((0, 0), (0, 8 - ST), (0, 0)))
        tpad = lambda a: jnp.pad(jnp.transpose(a.reshape(a.shape[0], SB, ST), (1, 0, 2)), ((0, 0), (0, 0), (0, LANES - ST)))
        knt, vnt = tpad(kt), tpad(vt)
        lfn = jnp.where(rr < H_C, jnp.pad(gt3, ((0, 0), (0, 0), (0, LANES - ST))), 0.0)[:, 0:8]
        page_ids = (l * n_phys + page_table[:, ::-1]).reshape(-1)
        res = _mlp(xp, mab_p, hc_p, mod_p, g2, gf, wo_all, wu_all, wd_all, l, rows_p // SB, T, final,
                   sample=(page_ids, q8, knt, vnt, lfn, ckt, cvt, clf))
        xp = res[0]
        if final:
            yp = res[1]
        hc8 = res[-1]
        hc = hc8[:, 0:ST].reshape(rows_s, W_C).astype(BF16)
        res = _mlp(xs, mab, hc, mod_s, g2, gf, wo_all, wu_all, wd_all, l, rows_s, rows_s, final)
        xs = res[0]
        if final:
            ys = res[1]
        C1, n1, mm1 = _unpack_state(st1, m1)
        lf_s = jnp.transpose(gt3[:, 0:H_C, :], (0, 2, 1))
        pool_s = jnp.concatenate([state_pool[l], u.reshape(SB, ST, W_B)], axis=1)[:, -POOL_HIST:]
        outs_s.append((kc.reshape(SB, ST, H_C, DH_C), vc.reshape(SB, ST, H_C, DH_C), lf_s, C1, n1, mm1, pool_s))

    to_bthd = lambda a: jnp.transpose(a.reshape(depth, B, H_C, DH_C, T), (0, 1, 4, 2, 3))
    sp_ = [jnp.stack(a) for a in zip(*outs_p)]
    ss_ = [jnp.stack(a) for a in zip(*outs_s)]
    return (yp.reshape(B, T, D), ys.reshape(SB, ST, D), to_bthd(kv_all[0]), to_bthd(kv_all[1]), *sp_, *ss_)
```

```python
import functools

import numpy as np
import jax
import jax.numpy as jnp
from jax import lax
from jax.experimental import pallas as pl
from jax.experimental.pallas import tpu as pltpu

F32 = jnp.float32
BF16 = jnp.bfloat16

D_MODEL = 1024
H_A = 4
W_A = 384
DV_A = 96
DK_A = 48
W_B = 256
CG_B = 64
POOL_HIST = 15
POOL_WINDOWS = (2, 4, 8, 16)
DH_C = 64
W_C = 384
H_C = 6
D_FF = 4096
EPS = 1e-6
CHUNK = 128
LOG2E = 1.4426950408889634
NEG = -1e30

LANES = 128
QK_PAD = 256
ST_COLS = W_A + LANES
MIX_AB = W_A + W_B
AUG_ONE0 = DH_C
AUG_F0 = DH_C + 8
VMEM_LIMIT = 56 * 1024 * 1024
SEQ_GROUP = 2
ATTN_Q_TILE = 1024
ATTN_STRIP = 256
DEAD_LOG2 = 152.0
NORM_SLACK = 1.05
MLP_CHUNKS = 4
SAMPLE_Q = 4
SAMPLE_PAGES_PER_STEP = 16


def _dot(a, b):
    return jnp.dot(a, b, preferred_element_type=F32)


def _dot_nt(a, b):
    return lax.dot_general(a, b, (((1,), (1,)), ((), ())), preferred_element_type=F32)


def _dot_tn(a, b):
    return lax.dot_general(a, b, (((0,), (0,)), ((), ())), preferred_element_type=F32)


def _split3(x):
    a = x.astype(BF16)
    r = x - a.astype(F32)
    b = r.astype(BF16)
    c = (r - b.astype(F32)).astype(BF16)
    return a, b, c


def _dot3_rhs(m01, x):
    a, b, c = _split3(x)
    return _dot(m01, a) + _dot(m01, b) + _dot(m01, c)


def _dot3_lhs(x, m01):
    a, b, c = _split3(x)
    return _dot(a, m01) + _dot(b, m01) + _dot(c, m01)


def _log_sigmoid(x):
    return jnp.minimum(x, 0.0) - jnp.log1p(jnp.exp(-jnp.abs(x)))


def _sigmoid(x):
    return 1.0 / (1.0 + jnp.exp(-x))


def _iota(shape, dim):
    return lax.broadcasted_iota(jnp.int32, shape, dim)


def _expand_heads(src, width, per_head):
    lane = _iota((1, width), 1)
    out = jnp.zeros(src.shape[:-1] + (width,), F32)
    for h in range(H_A):
        out = jnp.where((lane >= h * per_head) & (lane < (h + 1) * per_head), src[:, h:h + 1], out)
    return out


def _mod_kernel(c_ref, w_ref, b_ref, o_ref):
    c = c_ref[...]
    s = (c * _sigmoid(c)).astype(BF16)
    o_ref[0] = _dot(s, w_ref[0].astype(BF16)) + b_ref[0]


def _modulation(c_all, w_ada, b_ada):
    depth, d, n6 = w_ada.shape
    rows = c_all.shape[0]
    tn = 1536
    return pl.pallas_call(
        _mod_kernel,
        grid=(depth, n6 // tn),
        in_specs=[
            pl.BlockSpec((rows, d), lambda l, j: (0, 0)),
            pl.BlockSpec((1, d, tn), lambda l, j: (l, 0, j)),
            pl.BlockSpec((1, 1, tn), lambda l, j: (l, 0, j)),
        ],
        out_specs=pl.BlockSpec((1, rows, tn), lambda l, j: (l, 0, j)),
        out_shape=jax.ShapeDtypeStruct((depth, rows, n6), F32),
        compiler_params=pltpu.CompilerParams(dimension_semantics=("arbitrary", "arbitrary"), vmem_limit_bytes=VMEM_LIMIT),
        name="modulation",
    )(c_all, w_ada, b_ada.reshape(depth, 1, n6))


NC_QA, NC_KA, NC_VA, NC_OA, NC_U, NC_G = 0, 256, 512, 896, 1280, 1536
N_TOKEN_MAJOR = 1664
TR_K, TR_V, TR_G, TR_Q = 0, 384, 768, 784
N_FEATURE_MAJOR = TR_Q + W_C
N_W_ROWS = N_TOKEN_MAJOR + N_FEATURE_MAJOR


def _norm_mod(x, g, scale, shift):
    ms = jnp.mean(x * x, axis=-1, keepdims=True)
    return (x * lax.rsqrt(ms + EPS) * g) * (1.0 + scale) + shift


def _gate_rows(graw):
    r = _iota(graw.shape, 0)
    ls = _log_sigmoid(graw)
    return jnp.where((r < H_C) | (r >= 12), ls, jnp.where(r < 8, 0.0, graw))


def _gate_cols(graw):
    ln = _iota(graw.shape, 1)
    return jnp.where(ln < H_A, _log_sigmoid(graw), jnp.where(ln < 2 * H_A, graw, 0.0))


def _inproj_common(x_ref, mod_ref, g_ref, w_ref, bn_ref, sn_ref, bt_ref, st_ref, qk_ref, va_ref, oa_ref, u_ref, gc_ref):
    h = _norm_mod(x_ref[...], g_ref[...], mod_ref[1, 0], mod_ref[0, 0])
    hb = h.astype(BF16)

    def seg(off, width):
        return (_dot_nt(hb, w_ref[off:off + width, :]) + bn_ref[:, off:off + width]) * sn_ref[:, off:off + width]

    def tseg(off, rows):
        return (_dot_nt(w_ref[N_TOKEN_MAJOR + off:N_TOKEN_MAJOR + off + rows, :], hb) + bt_ref[off:off + rows, :]) * st_ref[off:off + rows, :]

    qk_ref[...] = seg(NC_QA, 2 * QK_PAD).astype(BF16)
    va_ref[...] = seg(NC_VA, W_A).astype(BF16)
    oa_ref[...] = seg(NC_OA, W_A)
    u_ref[...] = seg(NC_U, W_B)
    gc_ref[...] = _gate_cols(seg(NC_G, LANES))
    return seg, tseg


def _inproj_prompt_kernel(x_ref, mod_ref, g_ref, w_ref, bn_ref, sn_ref, bt_ref, st_ref, tri_ref, *rest, tiles_per_seq, n_prev):
    prev = rest[0:2 * min(n_prev, 1)]
    (qk_ref, va_ref, oa_ref, u_ref, gc_ref, kaug_ref, qta_ref, vta_ref, kt_ref, vt_ref, gt_ref, ft_ref, nrm_ref, carry_ref) = rest[len(prev):]
    i = pl.program_id(0)
    tm = x_ref.shape[0]
    seg, tseg = _inproj_common(x_ref, mod_ref, g_ref, w_ref, bn_ref, sn_ref, bt_ref, st_ref, qk_ref, va_ref, oa_ref, u_ref, gc_ref)

    kt = tseg(TR_K, W_C)
    vt = tseg(TR_V, W_C)
    gt = _gate_rows(tseg(TR_G, 16))
    qt = tseg(TR_Q, W_C)
    if n_prev:
        kt_ref[0:n_prev, 0] = prev[0][:, 0]
        vt_ref[0:n_prev, 0] = prev[1][:, 0]
    kt_ref[n_prev, 0] = kt
    vt_ref[n_prev, 0] = vt
    gt_ref[0] = gt

    first = (i % tiles_per_seq) == 0
    carry = jnp.where(first, 0.0, carry_ref[:, 0:1])
    ft = _dot3_lhs(gt[0:8] * LOG2E, tri_ref[...]) + carry
    carry_ref[...] = jnp.broadcast_to(ft[:, tm - 1:tm], carry_ref.shape)
    ft_ref[0] = ft
    f1, f2, f3 = [p.astype(F32) for p in _split3(ft)]

    nr, nl = _iota((8, LANES), 0), _iota((8, LANES), 1)
    nrm = jnp.zeros((8, LANES), F32)
    for hh in range(H_C):
        for col, x in ((0, qt), (1, kt)):
            xh = x[hh * DH_C:(hh + 1) * DH_C]
            big = jnp.sqrt(jnp.max(jnp.sum(xh * xh, axis=0, keepdims=True), axis=1, keepdims=True))
            nrm = jnp.where((nr == hh) & (nl == col), big, nrm)
    nrm_ref[0, 0] = nrm

    r8 = _iota((8, tm), 0)
    ktail = jnp.concatenate([jnp.where(r8 < 3, 1.0, 0.0), -f1, -f2, -f3,
                             jnp.zeros((LANES - AUG_F0 - 24, tm), F32)], axis=0)
    r64 = _iota((LANES - DH_C, tm), 0)
    for hh in range(H_C):
        kaug_ref[hh] = jnp.concatenate([kt[hh * DH_C:(hh + 1) * DH_C], ktail], axis=0).T.astype(BF16)
        qa = jnp.where(r64 == 0, f1[hh:hh + 1], jnp.where(r64 == 1, f2[hh:hh + 1], jnp.where(r64 == 2, f3[hh:hh + 1], 0.0)))
        qa = jnp.where((r64 == 8 + hh) | (r64 == 16 + hh) | (r64 == 24 + hh), 1.0, qa)
        qta_ref[0, hh] = jnp.concatenate([qt[hh * DH_C:(hh + 1) * DH_C].astype(BF16), qa.astype(BF16)], axis=0)
        vta_ref[0, hh] = jnp.concatenate([vt[hh * DH_C:(hh + 1) * DH_C].astype(BF16),
                                          jnp.where(r64 == 0, 1.0, 0.0).astype(BF16)], axis=0)


def _inproj_sample_kernel(x_ref, mod_ref, g_ref, w_ref, bn_ref, sn_ref, bt_ref, st_ref,
                          qk_ref, va_ref, oa_ref, u_ref, gc_ref, qc_ref, kc_ref, vc_ref, kt_ref, vt_ref, gt_ref):
    _, tseg = _inproj_common(x_ref, mod_ref, g_ref, w_ref, bn_ref, sn_ref, bt_ref, st_ref, qk_ref, va_ref, oa_ref, u_ref, gc_ref)
    kt = tseg(TR_K, W_C)
    vt = tseg(TR_V, W_C)
    kt_ref[...] = kt
    vt_ref[...] = vt
    gt_ref[...] = _gate_rows(tseg(TR_G, 16))
    qc_ref[...] = tseg(TR_Q, W_C).T
    kc_ref[...] = kt.T
    vc_ref[...] = vt.T


def _inproj_weights(w_in, b_in):
    o = np.cumsum((0, H_A * DK_A, H_A * DK_A, W_A, H_A, H_A, W_A, W_B, W_C, W_C, W_C, H_C))
    wt_full = w_in.T
    d = w_in.shape[0]
    rows = lambda k: (wt_full[o[k]:o[k + 1]], b_in[o[k]:o[k + 1]])
    (wqa, bqa), (wka, bka), (wva, bva), (wia, bia), (wfa, bfa), (woa, boa), (wu, bu), (wqc, bqc), (wkc, bkc), (wvc, bvc), (wfc, bfc) = [rows(k) for k in range(11)]
    zw = lambda n: jnp.zeros((n, d), F32)
    zb = lambda n: jnp.zeros((n,), F32)
    ws = [wqa, zw(QK_PAD - 192), wka, zw(QK_PAD - 192), wva, woa, wu, wfa, wia, zw(LANES - 8)]
    bs = [bqa, zb(QK_PAD - 192), bka, zb(QK_PAD - 192), bva, boa, bu, bfa, bia, zb(LANES - 8)]
    sn =jnp.concatenate([jnp.ones((QK_PAD,), F32), jnp.full((QK_PAD,), DK_A ** -0.5, F32),
                          jnp.ones((N_TOKEN_MAJOR - 2 * QK_PAD,), F32)])[None, :]
    bn = jnp.concatenate(bs)[None, :]
    ws += [wkc, wvc, wfc, zw(2), wia, wfa, wqc]
    bt = jnp.concatenate([bkc, bvc, bfc, zb(2), bia, bfa, bqc])[:, None]
    st = jnp.concatenate([jnp.ones((TR_Q,), F32), jnp.full((W_C,), DH_C ** -0.5 * LOG2E, F32)])[:, None]
    return jnp.concatenate(ws, axis=0).astype(BF16), bn, sn, bt, st


def _inproj_prompt(x2, mod4, g1, weights, batch, seq, layer, kv_prev):
    w, bn, sn, bt, st = weights
    rows, d = x2.shape
    tm = min(512, seq)
    tps = seq // tm
    tri = jnp.asarray(np.triu(np.ones((tm, tm), np.float32)), BF16)
    full = lambda a: pl.BlockSpec(a.shape, lambda i: (0,) * a.ndim)
    rowblk = lambda wd: pl.BlockSpec((tm, wd), lambda i: (i, 0))
    tblk = lambda r: pl.BlockSpec((1, r, tm), lambda i: (i // tps, 0, i % tps))
    tblk4 = pl.BlockSpec((1, H_C, LANES, tm), lambda i: (i // tps, 0, 0, i % tps))
    kvblk = lambda n: pl.BlockSpec((n, 1, W_C, tm), lambda i: (0, i // tps, 0, i % tps))
    prev_in = [] if kv_prev is None else list(kv_prev)
    outs = pl.pallas_call(
        functools.partial(_inproj_prompt_kernel, tiles_per_seq=tps, n_prev=layer),
        grid=(rows // tm,),
        in_specs=[rowblk(d), pl.BlockSpec((6, 1, 1, d), lambda i: (0, i // tps, 0, 0)), full(g1),
                  full(w), full(bn), full(sn), full(bt), full(st), full(tri)] + [kvblk(layer)] * len(prev_in),
        out_specs=[rowblk(2 * QK_PAD), rowblk(W_A), rowblk(W_A), rowblk(W_B), rowblk(LANES),
                   pl.BlockSpec((H_C, tm, LANES), lambda i: (0, i, 0)), tblk4, tblk4, kvblk(layer + 1), kvblk(layer + 1), tblk(16),
                   tblk(8), pl.BlockSpec((1, 1, 8, LANES), lambda i: (i // tps, i % tps, 0, 0))],
        out_shape=[jax.ShapeDtypeStruct((rows, 2 * QK_PAD), BF16), jax.ShapeDtypeStruct((rows, W_A), BF16),
                   jax.ShapeDtypeStruct((rows, W_A), F32), jax.ShapeDtypeStruct((rows, W_B), F32),
                   jax.ShapeDtypeStruct((rows, LANES), F32),
                   jax.ShapeDtypeStruct((H_C, rows, LANES), BF16),
                   jax.ShapeDtypeStruct((batch, H_C, LANES, seq), BF16), jax.ShapeDtypeStruct((batch, H_C, LANES, seq), BF16),
                   jax.ShapeDtypeStruct((layer + 1, batch, W_C, seq), F32), jax.ShapeDtypeStruct((layer + 1, batch, W_C, seq), F32),
                   jax.ShapeDtypeStruct((batch, 16, seq), F32),
                   jax.ShapeDtypeStruct((batch, 8, seq), F32), jax.ShapeDtypeStruct((batch, tps, 8, LANES), F32)],
        scratch_shapes=[pltpu.VMEM((8, LANES), F32)],
        compiler_params=pltpu.CompilerParams(dimension_semantics=("arbitrary",), vmem_limit_bytes=VMEM_LIMIT),
        name="inproj_prompt",
    )(x2, mod4, g1, w, bn, sn, bt, st, tri, *prev_in)
    return outs


def _inproj_sample(x2, mod4, g1, weights):
    w, bn, sn, bt, st = weights
    rows, d = x2.shape
    full = lambda a: pl.BlockSpec(a.shape, lambda i: (0,) * a.ndim)
    o2 = lambda r, c: pl.BlockSpec((r, c), lambda i: (0, 0))
    sds = lambda r, c, t: jax.ShapeDtypeStruct((r, c), t)
    return pl.pallas_call(
        _inproj_sample_kernel,
        grid=(1,),
        in_specs=[full(x2), full(mod4), full(g1), full(w), full(bn), full(sn), full(bt), full(st)],
        out_specs=[o2(rows, 2 * QK_PAD), o2(rows, W_A), o2(rows, W_A), o2(rows, W_B), o2(rows, LANES),
                   o2(rows, W_C), o2(rows, W_C), o2(rows, W_C), o2(W_C, rows), o2(W_C, rows), o2(16, rows)],
        out_shape=[sds(rows, 2 * QK_PAD, BF16), sds(rows, W_A, BF16), sds(rows, W_A, F32), sds(rows, W_B, F32), sds(rows, LANES, F32),
                   sds(rows, W_C, F32), sds(rows, W_C, F32), sds(rows, W_C, F32), sds(W_C, rows, F32), sds(W_C, rows, F32), sds(16, rows, F32)],
        compiler_params=pltpu.CompilerParams(dimension_semantics=("arbitrary",), vmem_limit_bytes=VMEM_LIMIT),
        name="inproj_sample",
    )(x2, mod4, g1, w, bn, sn, bt, st)


def _state_mask():
    r = np.arange(QK_PAD)[:, None]
    c = np.arange(ST_COLS)[None, :]
    m = np.zeros((QK_PAD, ST_COLS), np.float32)
    for h in range(H_A):
        rows = (r >= h * DK_A) & (r < (h + 1) * DK_A)
        cols = ((c >= h * DV_A) & (c < (h + 1) * DV_A)) | (c == W_A + h)
        m[rows & cols] = 1.0
    return m


def _seq_kernel(*refs, pos0, group, interleave):
    chains = [_seq_one(g, *refs, pos0=pos0) for g in range(group)]
    if not interleave:
        for ch in chains:
            for _ in ch:
                pass
        return
    for lead in range(group - 1):
        for ch in chains[:group - 1 - lead]:
            next(ch)
    while chains:
        chains = [ch for ch in chains if next(ch, _DONE) is not _DONE]


_DONE = object()


def _seq_one(g, qk_ref, v_ref, o_ref, u_ref, gc_ref, gt_ref, st0_ref, m0_ref, hist0_ref, gh_ref, wp_ref, sp_ref, mask_ref,
             mix_ref, st_out_ref, m_out_ref, hist_out_ref, st_scr, m_scr, z_scr, *, pos0):
    c = pl.program_id(1)
    L = CHUNK

    @pl.when(c == 0)
    def _():
        st_scr[g] = st0_ref[g]
        m_scr[g] = m0_ref[g]
        z_scr[g, 0:16, :] = hist0_ref[g]

    q = qk_ref[g, :, 0:QK_PAD]
    k = qk_ref[g, :, QK_PAD:2 * QK_PAD]
    v = v_ref[g]
    gc = gc_ref[g]
    gt = gt_ref[g]
    st = st_scr[g]
    m_row = m_scr[g, 0:1, :]

    row = _iota((L, L), 0)
    colm = _iota((L, L), 1)
    tri_l = jnp.where(colm <= row, 1.0, 0.0).astype(BF16)
    tri_u = jnp.where(row <= colm, 1.0, 0.0).astype(BF16)
    lane128 = _iota((1, LANES), 1)
    lane_q = _iota((1, QK_PAD), 1)
    lane_v = _iota((1, W_A), 1)

    u = u_ref[g]
    z_scr[g, 16:16 + L, :] = u
    z = z_scr[g]
    s2 = z + pltpu.roll(z, 1, axis=0)
    s4 = s2 + pltpu.roll(s2, 2, axis=0)
    s8 = s4 + pltpu.roll(s4, 4, axis=0)
    s16 = s8 + pltpu.roll(s8, 8, axis=0)
    navail = (pos0 + c * L + 1 + _iota((L, 1), 0)).astype(F32)
    lane_u = _iota((1, W_B), 1)
    y = None
    for gi, (w, sw) in enumerate(zip(POOL_WINDOWS, (s2, s4, s8, s16))):
        yg = sw[16:16 + L, :] * (1.0 / jnp.minimum(float(w), navail))
        y = yg if y is None else jnp.where(lane_u >= gi * CG_B, yg, y)
    y = y - u
    hb = _dot(y.astype(BF16), wp_ref[...]) * sp_ref[...]
    z_scr[g, 0:16, :] = u[L - 16:L, :]

    bc_col = _dot3_rhs(tri_l, gc)
    bc_row = _dot3_lhs(gt[8:16], tri_u)
    qk_h = [_dot_nt(jnp.where((lane_q >= h * DK_A) & (lane_q < (h + 1) * DK_A), q, jnp.zeros_like(q)), k) for h in range(H_A)]
    qs = _dot(q, st.astype(BF16))
    yield
    inter = bc_col + m_row
    ia_col = pltpu.roll(gc, LANES - H_A, axis=1)

    causal = colm <= row
    mt_all = jnp.zeros((L, LANES), F32)
    s_list = []
    vblk = []
    for h in range(H_A):
        d = bc_col[:, h:h + 1] - bc_row[4 + h:5 + h, :] + gt[8 + h:9 + h, :]
        d = jnp.where(causal, d, NEG)
        mt = jnp.maximum(inter[:, h:h + 1], jnp.max(d, axis=-1, keepdims=True))
        s = qk_h[h] * jnp.exp(d - mt)
        s_list.append(s.astype(BF16))
        mt_all = jnp.where(lane128 == h, mt, mt_all)
        vm = jnp.where((lane_v >= h * DV_A) & (lane_v < (h + 1) * DV_A), v, jnp.zeros_like(v))
        vblk.append(jnp.concatenate([vm, jnp.broadcast_to(jnp.where(lane128 == h, 1.0, 0.0).astype(BF16), (L, LANES))], axis=1))
    s_cat = jnp.concatenate(s_list, axis=1)
    v_blk = jnp.concatenate(vblk, axis=0)
    pv = _dot(s_cat, v_blk)
    yield

    valid = lane128 < H_A
    m_new = mt_all[L - 1:L, :]
    g_col = jnp.where(valid, jnp.exp(bc_col[L - 1:L, :] - bc_col + ia_col - m_new), 0.0)
    a_last = jnp.where(valid, jnp.exp(inter[L - 1:L, :] - m_new), 0.0)
    gv = (v.astype(F32) * _expand_heads(g_col, W_A, DV_A)).astype(BF16)
    upd = _dot_tn(k, jnp.concatenate([gv, g_col.astype(BF16)], axis=1))
    yield
    a512 = jnp.concatenate([_expand_heads(a_last, W_A, DV_A), a_last], axis=1)
    st_new = a512 * st + mask_ref[...] * upd
    st_scr[g] = st_new
    m_rows = jnp.broadcast_to(jnp.where(valid, m_new, 0.0), (8, LANES))
    m_scr[g] = m_rows

    a_all = jnp.where(valid, jnp.exp(inter - mt_all), 0.0)
    num = _expand_heads(a_all, W_A, DV_A) * qs[:, 0:W_A] + pv[:, 0:W_A]
    den = a_all * qs[:, W_A:] + pv[:, W_A:]
    inv = 1.0 / jnp.maximum(jnp.abs(den), jnp.exp(-mt_all))
    hs = num * _expand_heads(jnp.where(valid, inv, 0.0), W_A, DV_A)

    sq = hs * hs
    ss_all = jnp.zeros((L, LANES), F32)
    for h in range(H_A):
        ssum = jnp.sum(jnp.where((lane_v >= h * DV_A) & (lane_v < (h + 1) * DV_A), sq, 0.0), axis=-1, keepdims=True)
        ss_all = jnp.where(lane128 == h, ssum, ss_all)
    r_all = lax.rsqrt(ss_all * (1.0 / DV_A) + EPS)
    ha = _sigmoid(o_ref[g]) * hs * _expand_heads(r_all, W_A, DV_A) * gh_ref[...]
    mix_ref[g] = jnp.concatenate([ha, hb], axis=1).astype(BF16)

    @pl.when(c == pl.num_programs(1) - 1)
    def _():
        st_out_ref[g] = st_new
        m_out_ref[g] = m_rows
        hist_out_ref[g] = u[L - 16:L, :]


def _sequence(qk, va, oa, u, gc, gt, st0, m0, hist0, gh, wp, sp, nb, nchunk, pos0):
    L = CHUNK
    G = SEQ_GROUP
    assert nb % G == 0
    mask = jnp.asarray(_state_mask())
    seq3 = lambda a: a.reshape(nb, nchunk * L, a.shape[-1])
    rowblk = lambda w: pl.BlockSpec((G, L, w), lambda b, c: (b, c, 0))
    perb = lambda a: pl.BlockSpec((G,) + a.shape[1:], lambda b, c: (b,) + (0,) * (a.ndim - 1))
    full = lambda a: pl.BlockSpec(a.shape, lambda b, c: (0,) * a.ndim)
    outs = pl.pallas_call(
        functools.partial(_seq_kernel, pos0=pos0, group=G, interleave=nchunk == 1),
        grid=(nb // G, nchunk),
        in_specs=[rowblk(2 * QK_PAD), rowblk(W_A), rowblk(W_A), rowblk(W_B), rowblk(LANES),
                  pl.BlockSpec((G, 16, L), lambda b, c: (b, 0, c)),
                  perb(st0), perb(m0), perb(hist0), full(gh), full(wp), full(sp), full(mask)],
        out_specs=[rowblk(MIX_AB), pl.BlockSpec((G, QK_PAD, ST_COLS), lambda b, c: (b, 0, 0)),
                   pl.BlockSpec((G, 8, LANES), lambda b, c: (b, 0, 0)), pl.BlockSpec((G, 16, W_B), lambda b, c: (b, 0, 0))],
        out_shape=[jax.ShapeDtypeStruct((nb, nchunk * L, MIX_AB), BF16), jax.ShapeDtypeStruct((nb, QK_PAD, ST_COLS), F32),
                   jax.ShapeDtypeStruct((nb, 8, LANES), F32), jax.ShapeDtypeStruct((nb, 16, W_B), F32)],
        scratch_shapes=[pltpu.VMEM((G, QK_PAD, ST_COLS), F32), pltpu.VMEM((G, 8, LANES), F32), pltpu.VMEM((G, 16 + L, W_B), F32)],
        compiler_params=pltpu.CompilerParams(dimension_semantics=("arbitrary", "arbitrary"), vmem_limit_bytes=VMEM_LIMIT),
        name="sequence",
    )(seq3(qk), seq3(va), seq3(oa), seq3(u), seq3(gc), gt, st0, m0, hist0, gh, wp, sp, mask)
    return (outs[0].reshape(nb * nchunk * L, MIX_AB),) + tuple(outs[1:])


def _pack_state(C, n, m):
    nb = C.shape[0]
    z = lambda r, c: jnp.zeros((nb, r, c), F32)
    blocks = []
    for h in range(H_A):
        blocks.append(jnp.concatenate([z(DK_A, h * DV_A), jnp.swapaxes(C[:, h], 1, 2), z(DK_A, W_A - (h + 1) * DV_A),
                                       z(DK_A, h), n[:, h][:, :, None], z(DK_A, LANES - h - 1)], axis=2))
    st = jnp.concatenate(blocks + [z(QK_PAD - H_A * DK_A, ST_COLS)], axis=1)
    mm = jnp.concatenate([jnp.broadcast_to(m[:, None, :], (nb, 8, H_A)), z(8, LANES - H_A)], axis=2)
    return st, mm


def _unpack_state(st, mm):
    C = jnp.stack([jnp.swapaxes(st[:, h * DK_A:(h + 1) * DK_A, h * DV_A:(h + 1) * DV_A], 1, 2) for h in range(H_A)], axis=1)
    n = jnp.stack([st[:, h * DK_A:(h + 1) * DK_A, W_A + h] for h in range(H_A)], axis=1)
    return C, n, mm[:, 0, 0:H_A]


def _attn_prompt_kernel(skip_ref, qt_ref, k_ref, vt_ref, o_ref, s_scr, m_scr, acc_scr, *, tq, tk, sw):
    qi = pl.program_id(2)
    m_scr[...] = jnp.full(m_scr.shape, NEG, F32)
    acc_scr[...] = jnp.zeros(acc_scr.shape, F32)
    units = [(hh, st) for hh in range(2) for st in range(tq // sw)]

    def qk_block(j, slot, skip=()):
        start = pl.multiple_of(j * tk, tk)
        for hh, st in units:
            if st not in skip:
                s_scr[slot, hh, :, st * sw:(st + 1) * sw] = _dot(k_ref[hh, pl.ds(start, tk), :], qt_ref[0, hh, :, st * sw:(st + 1) * sw])

    def softmax_pv_block(j, slot, masked=(), skip=()):
        start = pl.multiple_of(j * tk, tk)
        for hh, st in units:
            if st in skip:
                continue
            strip = slice(st * sw, (st + 1) * sw)
            s = s_scr[slot, hh, :, strip]
            if st in masked:
                keyg = j * tk + _iota((tk, sw), 0)
                qryg = qi * tq + st * sw + _iota((tk, sw), 1)
                s = jnp.where(keyg <= qryg, s, NEG)
            m = m_scr[hh, 0:1, strip]
            m_new = jnp.maximum(m, jnp.max(s, axis=0, keepdims=True))
            p = jnp.exp2(s - m_new)
            pv = _dot(vt_ref[0, hh, :, pl.ds(start, tk)], p.astype(BF16))
            acc_scr[hh, :, strip] = jnp.exp2(m - m_new) * acc_scr[hh, :, strip] + pv
            m_scr[hh, 0:1, strip] = m_new

    n_full = (qi * tq) // tk
    first_pair = jnp.minimum(skip_ref[(pl.program_id(0) * pl.num_programs(1) + pl.program_id(1)) * pl.num_programs(2) + qi],
                             n_full // 2)
    qk_block(2 * first_pair, 0)

    def body(i, carry):
        j = 2 * i
        qk_block(j + 1, 1)
        softmax_pv_block(j, 0)
        qk_block(j + 2, 0)
        softmax_pv_block(j + 1, 1)
        return carry

    lax.fori_loop(first_pair, n_full // 2, body, 0)
    n_diag = tq // tk
    for d in range(n_diag):
        if d + 1 < n_diag:
            qk_block(n_full + d + 1, (d + 1) % 2, skip=tuple(range(d + 1)))
        softmax_pv_block(n_full + d, d % 2, masked=(d,), skip=tuple(range(d)))
    outs = []
    for hh in range(2):
        acc = acc_scr[hh]
        outs.append((acc * (1.0 / acc[DH_C:DH_C + 1, :])).T)
    lane = _iota((1, LANES), 1)
    o_ref[...] = jnp.where(lane < DH_C, outs[0], pltpu.roll(outs[1], DH_C, axis=1)).astype(BF16)


def _dead_block_pairs(ft, qk_norm, seq, tq, tk):
    nb = ft.shape[0]
    nq, nk = seq // tq, seq // tk
    tiles = qk_norm.shape[1]
    qmax = jnp.max(qk_norm[:, :, 0:H_C, 0].reshape(nb, nq, tiles // nq, H_C), axis=2)
    kmax = jnp.max(qk_norm[:, :, 0:H_C, 1], axis=1)
    f_start = jnp.transpose(ft[:, 0:H_C, 0::tq], (0, 2, 1))
    f_end = jnp.transpose(ft[:, 0:H_C, tk - 1::tk], (0, 2, 1))
    bound = (NORM_SLACK * 2.0) * (qmax * kmax[:, None, :])[:, :, None, :] + f_start[:, :, None, :] - f_end[:, None, :, :]
    dead = bound < -DEAD_LOG2
    dead = jnp.all(dead.reshape(nb, nq, nk // 2, 2, H_C // 2, 2), axis=(3, 5))
    pair = jnp.arange(nk // 2, dtype=jnp.int32)[None, None, :, None]
    n_dead = jnp.min(jnp.where(dead, nk // 2, pair), axis=2)
    return jnp.transpose(n_dead, (0, 2, 1)).reshape(-1).astype(jnp.int32)


def _attn_prompt(qta, k_aug, vta, ft, qk_norm, batch, seq):
    rows = batch * seq
    tk = sw = ATTN_STRIP
    tq = min(ATTN_Q_TILE, seq)
    assert seq % tq == 0 and (tq // tk) % 2 == 0
    nq = seq // tq
    grid_spec = pltpu.PrefetchScalarGridSpec(
        num_scalar_prefetch=1, grid=(batch, H_C // 2, nq),
        in_specs=[pl.BlockSpec((1, 2, LANES, tq), lambda b, p, i, sk: (b, p, 0, i)),
                  pl.BlockSpec((2, seq, LANES), lambda b, p, i, sk: (p, b, 0)),
                  pl.BlockSpec((1, 2, LANES, seq), lambda b, p, i, sk: (b, p, 0, 0))],
        out_specs=pl.BlockSpec((tq, LANES), lambda b, p, i, sk: (b * nq + i, p)),
        scratch_shapes=[pltpu.VMEM((2, 2, tk, tq), F32), pltpu.VMEM((2, 8, tq), F32), pltpu.VMEM((2, LANES, tq), F32)])
    return pl.pallas_call(
        functools.partial(_attn_prompt_kernel, tq=tq, tk=tk, sw=sw),
        grid_spec=grid_spec,
        out_shape=jax.ShapeDtypeStruct((rows, W_C), BF16),
        compiler_params=pltpu.CompilerParams(dimension_semantics=("arbitrary", "arbitrary", "arbitrary"), vmem_limit_bytes=VMEM_LIMIT),
        name="attn_prompt",
    )(_dead_block_pairs(ft, qk_norm, seq, tq, tk), qta, k_aug, vta)


def _when(cond):
    if isinstance(cond, bool):
        return (lambda f: f()) if cond else (lambda f: None)
    return pl.when(cond)


def _sample_attn_step(t, n_total, first, last, P, pt_ref, q_ref, knt_ref, vnt_ref, lfn_ref, ck_hbm, cv_hbm, clf_hbm, o_ref,
                      kbuf, vbuf, lbuf, sems, m_scr, acc_scr, carry_scr, cq_scr):
    slot = t % 2

    def page_copies(step, sl):
        out = []
        for i in range(P):
            pg = pt_ref[step * P + i]
            out += [pltpu.make_async_copy(ck_hbm.at[pg], kbuf.at[sl, i], sems.at[sl, 0]),
                    pltpu.make_async_copy(cv_hbm.at[pg], vbuf.at[sl, i], sems.at[sl, 1]),
                    pltpu.make_async_copy(clf_hbm.at[pg], lbuf.at[sl, i], sems.at[sl, 2])]
        return out

    @_when(first and (t == 0))
    def _():
        for cp in page_copies(0, 0):
            cp.start()

    @pl.when(t + 1 < n_total)
    def _():
        for cp in page_copies(t + 1, 1 - slot):
            cp.start()

    for cp in page_copies(t, slot):
        cp.wait()

    R = 8 * SAMPLE_Q
    head_of_lane = lax.shift_right_logical(_iota((R, W_C), 1), 6)
    headmask = head_of_lane == (_iota((R, W_C), 0) & 7)
    lane = _iota((R, LANES), 1)
    rq = lax.shift_right_logical(_iota((R, LANES), 0), 3)

    def tile_q(x8):
        return jnp.concatenate([x8] * SAMPLE_Q, axis=0)

    q8 = q_ref[0]
    qrep = jnp.concatenate([jnp.broadcast_to(q8[qq:qq + 1], (8, W_C)) for qq in range(SAMPLE_Q)], axis=0)
    qbd = jnp.where(headmask, qrep, 0.0).astype(BF16)

    @_when(first)
    def _():
        a = tile_q(lfn_ref[0] * LOG2E)
        cq = jnp.sum(jnp.where(lane <= rq, a, 0.0), axis=-1, keepdims=True)
        incl = jnp.where(_iota((LANES, LANES), 0) <= _iota((LANES, LANES), 1), 1.0, 0.0).astype(BF16)
        crow = _dot3_lhs(a, incl)
        s = _dot(qbd, knt_ref[0].astype(BF16)) + cq - crow
        s = jnp.where((lane <= rq) & (lane < SAMPLE_Q), s, NEG)
        m = jnp.max(s, axis=-1, keepdims=True)
        p = jnp.exp2(s - m)
        l = jnp.sum(p, axis=-1, keepdims=True)
        acc_scr[:, 0:W_C] = _dot_nt(p.astype(BF16), vnt_ref[0].astype(BF16))
        acc_scr[:, W_C:] = jnp.broadcast_to(l, (R, LANES))
        m_scr[...] = jnp.broadcast_to(m, (R, LANES))
        cq_scr[...] = jnp.broadcast_to(cq, (R, LANES))
        carry_scr[...] = jnp.zeros_like(carry_scr)

    cq = cq_scr[:, 0:1]
    strict = jnp.where(_iota((LANES, LANES), 0) > _iota((LANES, LANES), 1), 1.0, 0.0).astype(BF16)
    lf_all = lbuf[slot].reshape(8 * P, LANES) * LOG2E
    suf = _dot3_lhs(lf_all, strict)
    carry = carry_scr[:, 0:1]
    bias = [None] * P
    for i in range(P):
        bias[i] = tile_q(suf[8 * i:8 * i + 8] + carry)
        carry = carry + suf[8 * i:8 * i + 8, 0:1] + lf_all[8 * i:8 * i + 8, 0:1]
    carry_scr[...] = jnp.broadcast_to(carry, carry_scr.shape)

    kcat = jnp.concatenate([kbuf[slot, i].astype(BF16) for i in range(P)], axis=1)
    vcat = jnp.concatenate([vbuf[slot, i].astype(BF16) for i in range(P)], axis=1)
    s = _dot(qbd, kcat) + jnp.concatenate(bias, axis=1) + cq
    yield
    m_old = m_scr[:, 0:1]
    m_new = jnp.maximum(m_old, jnp.max(s, axis=-1, keepdims=True))
    alpha = jnp.exp2(m_old - m_new)
    p = jnp.exp2(s - m_new)
    l = alpha * acc_scr[:, W_C:W_C + 1] + jnp.sum(p, axis=-1, keepdims=True)
    acc = alpha * acc_scr[:, 0:W_C] + _dot_nt(p.astype(BF16), vcat)
    acc_scr[:, 0:W_C] = acc
    acc_scr[:, W_C:] = jnp.broadcast_to(l, (R, LANES))
    m_scr[...] = jnp.broadcast_to(m_new, (R, LANES))

    @_when(last)
    def _():
        o = jnp.where(headmask, acc * (1.0 / l), 0.0)
        rows = [jnp.sum(o[8 * qq:8 * qq + 8], axis=0, keepdims=True) for qq in range(SAMPLE_Q)]
        o_ref[0] = jnp.concatenate(rows + [jnp.zeros((8 - SAMPLE_Q, W_C), F32)], axis=0)


def _sample_attn_scratch(P):
    R = 8 * SAMPLE_Q
    return [pltpu.VMEM((2, P, W_C, LANES), F32), pltpu.VMEM((2, P, W_C, LANES), F32), pltpu.VMEM((2, P, 8, LANES), F32),
            pltpu.SemaphoreType.DMA((2, 3)),
            pltpu.VMEM((R, LANES), F32), pltpu.VMEM((R, W_C + LANES), F32), pltpu.VMEM((8, LANES), F32), pltpu.VMEM((R, LANES), F32)]


def _mlp_kernel(*refs, final, n_groups, pages_per_group):
    fused = n_groups > 0
    if fused:
        pt_ref, refs = refs[0], refs[1:]
    x_ref, mab_ref, hc_ref, mod_ref, g2_ref, gf_ref, wo_ref, wu_ref, wd_ref = refs[0:9]
    rest = refs[9:]
    if fused:
        sample_in, rest = rest[0:7], rest[7:]
    n_out = (2 if final else 1) + (1 if fused else 0)
    out_refs, scratch = rest[0:n_out], rest[n_out:]

    x = x_ref[...]
    mix = _dot(mab_ref[...], wo_ref[0, 0:MIX_AB, :]) + _dot(hc_ref[...], wo_ref[0, MIX_AB:, :])
    x1 = x + mod_ref[2, 0] * mix
    h2 = _norm_mod(x1, g2_ref[...], mod_ref[4, 0], mod_ref[3, 0]).astype(BF16)
    ff = jnp.zeros_like(x)
    n_chunks = n_groups if fused else MLP_CHUNKS
    fc = D_FF // n_chunks
    for f in range(n_chunks):
        a = jnp.maximum(_dot(h2, wu_ref[0, :, f * fc:(f + 1) * fc]), 0.0)
        if fused:
            step = _sample_attn_step(pl.program_id(0) * n_groups + f, pl.num_programs(0) * n_groups, f == 0, f == n_groups - 1,
                                     pages_per_group, pt_ref, *sample_in, out_refs[n_out - 1], *scratch)
            next(step)
        ff = ff + _dot((a * a).astype(BF16), wd_ref[0, f * fc:(f + 1) * fc, :])
        if fused:
            next(step, None)
    x2 = x1 + mod_ref[5, 0] * ff
    out_refs[0][...] = x2
    if final:
        ms = jnp.mean(x2 * x2, axis=-1, keepdims=True)
        out_refs[1][...] = x2 * lax.rsqrt(ms + EPS) * gf_ref[...]


def _mlp(x2, mab, hc, mod4, g2, gf, wo, wu, wd, layer, tm, rows_per_mod, final, sample=None):
    rows, d = x2.shape
    s = mod4.shape[2]
    fused = sample is not None
    im = (lambda f: (lambda i, pt: f(i))) if fused else (lambda f: f)
    const = lambda a: pl.BlockSpec(a.shape, im(lambda i: (0,) * a.ndim), pipeline_mode=pl.Buffered(1))
    layerw = lambda a: pl.BlockSpec((1,) + a.shape[1:], im(lambda i: (layer, 0, 0)), pipeline_mode=pl.Buffered(1))
    rowblk = lambda w: pl.BlockSpec((tm, w), im(lambda i: (i, 0)))
    n_out = 2 if final else 1
    in_specs = [rowblk(d), rowblk(MIX_AB), rowblk(W_C),
                pl.BlockSpec((6, 1, s, d), im(lambda i: (0, (i * tm) // rows_per_mod, 0, 0))),
                const(g2), const(gf), layerw(wo), layerw(wu), layerw(wd)]
    out_specs = [rowblk(d)] * n_out
    out_shape = [jax.ShapeDtypeStruct((rows, d), F32)] * n_out
    args = [x2, mab, hc, mod4, g2, gf, wo, wu, wd]
    scratch, n_groups, P, n_prefetch = [], 0, 0, 0
    if fused:
        page_ids, q8, knt, vnt, lfn = sample[0:5]
        nb, n_pages = q8.shape[0], page_ids.shape[0] // q8.shape[0]
        assert nb == rows // tm
        P = min(SAMPLE_PAGES_PER_STEP, n_pages)
        n_groups = n_pages // P
        assert D_FF % (n_groups * LANES) == 0
        perb = lambda a: pl.BlockSpec((1,) + a.shape[1:], lambda i, pt: (i, 0, 0))
        in_specs += [perb(q8), perb(knt), perb(vnt), perb(lfn)] + [pl.BlockSpec(memory_space=pl.ANY)] * 3
        out_specs += [pl.BlockSpec((1, 8, W_C), lambda i, pt: (i, 0, 0))]
        out_shape += [jax.ShapeDtypeStruct((nb, 8, W_C), F32)]
        args = [page_ids] + args + list(sample[1:])
        scratch, n_prefetch = _sample_attn_scratch(P), 1
    grid_spec = pltpu.PrefetchScalarGridSpec(num_scalar_prefetch=n_prefetch, grid=(rows // tm,), in_specs=in_specs,
                                             out_specs=out_specs, scratch_shapes=scratch)
    return pl.pallas_call(
        functools.partial(_mlp_kernel, final=final, n_groups=n_groups, pages_per_group=P),
        grid_spec=grid_spec,
        out_shape=out_shape,
        compiler_params=pltpu.CompilerParams(dimension_semantics=("arbitrary",), vmem_limit_bytes=VMEM_LIMIT),
        name="mlp_attn_sample" if fused else "mlp",
    )(*args)


def kernel(x_prompt, x_sample, c_prompt, c_sample, cache_k, cache_v, cache_logf, page_table, state_C, state_n, state_m, state_pool, w_ada, b_ada, g_norm1, g_norm2, w_in, b_in, g_head_a, w_pool, s_pool, w_out, w_up, w_down, g_final):
    depth = w_ada.shape[0]
    B, T, D = x_prompt.shape
    SB, ST, _ = x_sample.shape
    n_phys, page = cache_k.shape[1], cache_k.shape[2]
    n_pages = page_table.shape[1]
    rows_p, rows_s = B * T, SB * ST
    pos0_s = n_pages * page
    assert T % CHUNK == 0 and ST == SAMPLE_Q and page == LANES

    nc = B + SB
    c_all = jnp.concatenate([c_prompt, c_sample, jnp.zeros((-nc % 8, D), F32)], axis=0)
    mod = _modulation(c_all, w_ada, b_ada)

    ckt = jnp.transpose(cache_k, (0, 1, 3, 4, 2)).reshape(depth * n_phys, W_C, page)
    cvt = jnp.transpose(cache_v, (0, 1, 3, 4, 2)).reshape(depth * n_phys, W_C, page)
    clf = jnp.pad(jnp.transpose(cache_logf, (0, 1, 3, 2)), ((0, 0), (0, 0), (0, 8 - H_C), (0, 0))).reshape(depth * n_phys, 8, page)
    wo_all, wu_all, wd_all = w_out.astype(BF16), w_up.astype(BF16), w_down.astype(BF16)

    xp = x_prompt.reshape(rows_p, D)
    xs = x_sample.reshape(rows_s, D)
    zeros_state = (jnp.zeros((B, QK_PAD, ST_COLS), F32), jnp.zeros((B, 8, LANES), F32))
    outs_p, outs_s = [], []
    yp = ys = None
    kv_all = None
    for l in range(depth):
        final = l == depth - 1
        g1, g2, gf = g_norm1[l][None, :], g_norm2[l][None, :], g_final[None, :]
        gh = g_head_a[l].reshape(1, W_A)
        wp = jax.scipy.linalg.block_diag(*[w_pool[l, g] for g in range(len(POOL_WINDOWS))]).astype(BF16)
        sp = s_pool[l][None, :]
        weights = _inproj_weights(w_in[l], b_in[l])
        modl = mod[l].reshape(-1, 6, D)
        mod_p = jnp.transpose(modl[0:B], (1, 0, 2))[:, :, None, :]
        mod_s = jnp.transpose(jnp.repeat(modl[B:B + SB], ST, axis=0), (1, 0, 2))[:, None]

        (qk, va, oa, u, gc, k_aug, qta, vta, kt_all, vt_all, gt, ft, qk_norm) = _inproj_prompt(xp, mod_p, g1, weights, B, T, l, kv_all)
        kv_all = (kt_all, vt_all)
        mab, st1, m1, hist1 = _sequence(qk, va, oa, u, gc, gt, zeros_state[0], zeros_state[1], jnp.zeros((B, 16, W_B), F32),
                                        gh, wp, sp, B, T // CHUNK, 0)
        hc = _attn_prompt(qta, k_aug, vta, ft, qk_norm, B, T)
        mab_p, hc_p = mab, hc
        C1, n1, mm1 = _unpack_state(st1, m1)
        outs_p.append((jnp.transpose(gt[:, 0:H_C, :], (0, 2, 1)), C1, n1, mm1, hist1[:, 1:]))

        (qk, va, oa, u, gc, qc, kc, vc, kt, vt, gt) = _inproj_sample(xs, mod_s, g1, weights)

        def pad_rows(a, fill=0.0):
            a3 = a.reshape(SB, ST, a.shape[-1])
            return jnp.pad(a3, ((0, 0), (0, CHUNK - ST), (0, 0)), constant_values=fill).reshape(SB * CHUNK, a.shape[-1])

        lane = jnp.arange(LANES)[None, :]
        gc_pad = jnp.where((lane >= H_A) & (lane < 2 * H_A), pad_rows(gc, NEG), pad_rows(gc))
        gt3 = jnp.transpose(gt.reshape(16, SB, ST), (1, 0, 2))
        rr = jnp.arange(16)[None, :, None]
        gt_pad = jnp.where((rr >= 8) & (rr < 12), jnp.pad(gt3, ((0, 0), (0, 0), (0, CHUNK - ST)), constant_values=NEG),
                           jnp.pad(gt3, ((0, 0), (0, 0), (0, CHUNK - ST))))
        st0, m0 = _pack_state(state_C[l], state_n[l], state_m[l])
        hist0 = jnp.pad(state_pool[l], ((0, 0), (1, 0), (0, 0)))
        mab, st1, m1, _ = _sequence(pad_rows(qk), pad_rows(va), pad_rows(oa), pad_rows(u), gc_pad, gt_pad, st0, m0, hist0,
                                    gh, wp, sp, SB, 1, pos0_s)
        mab = mab.reshape(SB, CHUNK, MIX_AB)[:, 0:ST].reshape(rows_s, MIX_AB)
        q8 = jnp.pad(qc.reshape(SB, ST, W_C), ((0, 0), (0, 8 - ST), (0, 0)))
        tpad = lambda a: jnp.pad(jnp.transpose(a.reshape(a.shape[0], SB, ST), (1, 0, 2)), ((0, 0), (0, 0), (0, LANES - ST)))
        knt, vnt = tpad(kt), tpad(vt)
        lfn = jnp.where(rr < H_C, jnp.pad(gt3, ((0, 0), (0, 0), (0, LANES - ST))), 0.0)[:, 0:8]
        page_ids = (l * n_phys + page_table[:, ::-1]).reshape(-1)
        res = _mlp(xp, mab_p, hc_p, mod_p, g2, gf, wo_all, wu_all, wd_all, l, rows_p // SB, T, final,
                   sample=(page_ids, q8, knt, vnt, lfn, ckt, cvt, clf))
        xp = res[0]
        if final:
            yp = res[1]
        hc8 = res[-1]
        hc = hc8[:, 0:ST].reshape(rows_s, W_C).astype(BF16)
        res = _mlp(xs, mab, hc, mod_s, g2, gf, wo_all, wu_all, wd_all, l, rows_s, rows_s, final)
        xs = res[0]
        if final:
            ys = res[1]
        C1, n1, mm1 = _unpack_state(st1, m1)
        lf_s = jnp.transpose(gt3[:, 0:H_C, :], (0, 2, 1))
        pool_s = jnp.concatenate([state_pool[l], u.reshape(SB, ST, W_B)], axis=1)[:, -POOL_HIST:]
        outs_s.append((kc.reshape(SB, ST, H_C, DH_C), vc.reshape(SB, ST, H_C, DH_C), lf_s, C1, n1, mm1, pool_s))

    to_bthd = lambda a: jnp.transpose(a.reshape(depth, B, H_C, DH_C, T), (0, 1, 4, 2, 3))
    sp_ = [jnp.stack(a) for a in zip(*outs_p)]
    ss_ = [jnp.stack(a) for a in zip(*outs_s)]
    return (yp.reshape(B, T, D), ys.reshape(SB, ST, D), to_bthd(kv_all[0]), to_bthd(kv_all[1]), *sp_, *ss_)
```

```python
import functools

import numpy as np
import jax
import jax.numpy as jnp
from jax import lax
from jax.experimental import pallas as pl
from jax.experimental.pallas import tpu as pltpu

F32 = jnp.float32
BF16 = jnp.bfloat16

D_MODEL = 1024
H_A = 4
W_A = 384
DV_A = 96
DK_A = 48
W_B = 256
CG_B = 64
POOL_HIST = 15
POOL_WINDOWS = (2, 4, 8, 16)
DH_C = 64
W_C = 384
H_C = 6
D_FF = 4096
EPS = 1e-6
CHUNK = 128
LOG2E = 1.4426950408889634
NEG = -1e30

LANES = 128
QK_PAD = 256
ST_COLS = W_A + LANES
MIX_AB = W_A + W_B
AUG_F0 = DH_C + 8
VMEM_LIMIT = 56 * 1024 * 1024
SEQ_GROUP = 2
ATTN_Q_TILE = 1024
ATTN_STRIP = 256
DEAD_LOG2 = 152.0
NORM_SLACK = 1.05
MLP_CHUNKS = 4
SAMPLE_Q = 4
SAMPLE_PAGES_PER_STEP = 16


def _dot(a, b):
    return jnp.dot(a, b, preferred_element_type=F32)


def _dot_nt(a, b):
    return lax.dot_general(a, b, (((1,), (1,)), ((), ())), preferred_element_type=F32)


def _dot_tn(a, b):
    return lax.dot_general(a, b, (((0,), (0,)), ((), ())), preferred_element_type=F32)


def _split3(x):
    a = x.astype(BF16)
    r = x - a.astype(F32)
    b = r.astype(BF16)
    c = (r - b.astype(F32)).astype(BF16)
    return a, b, c


def _dot3_rhs(m01, x):
    a, b, c = _split3(x)
    return _dot(m01, a) + _dot(m01, b) + _dot(m01, c)


def _dot3_lhs(x, m01):
    a, b, c = _split3(x)
    return _dot(a, m01) + _dot(b, m01) + _dot(c, m01)


def _log_sigmoid(x):
    return jnp.minimum(x, 0.0) - jnp.log1p(jnp.exp(-jnp.abs(x)))


def _sigmoid(x):
    return 1.0 / (1.0 + jnp.exp(-x))


def _iota(shape, dim):
    return lax.broadcasted_iota(jnp.int32, shape, dim)


def _expand_heads(src, width, per_head):
    lane = _iota((1, width), 1)
    out = jnp.zeros(src.shape[:-1] + (width,), F32)
    for h in range(H_A):
        out = jnp.where((lane >= h * per_head) & (lane < (h + 1) * per_head), src[:, h:h + 1], out)
    return out


def _mod_kernel(c_ref, w_ref, b_ref, o_ref):
    c = c_ref[...]
    s = (c * _sigmoid(c)).astype(BF16)
    o_ref[0] = _dot(s, w_ref[0].astype(BF16)) + b_ref[0]


def _modulation(c_all, w_ada, b_ada):
    depth, d, n6 = w_ada.shape
    rows = c_all.shape[0]
    tn = 1536
    return pl.pallas_call(
        _mod_kernel,
        grid=(depth, n6 // tn),
        in_specs=[
            pl.BlockSpec((rows, d), lambda l, j: (0, 0)),
            pl.BlockSpec((1, d, tn), lambda l, j: (l, 0, j)),
            pl.BlockSpec((1, 1, tn), lambda l, j: (l, 0, j)),
        ],
        out_specs=pl.BlockSpec((1, rows, tn), lambda l, j: (l, 0, j)),
        out_shape=jax.ShapeDtypeStruct((depth, rows, n6), F32),
        compiler_params=pltpu.CompilerParams(dimension_semantics=("arbitrary", "arbitrary"), vmem_limit_bytes=VMEM_LIMIT),
        name="modulation",
    )(c_all, w_ada, b_ada.reshape(depth, 1, n6))


NC_QA, NC_KA, NC_VA, NC_OA, NC_U, NC_G = 0, 256, 512, 896, 1280, 1536
N_TOKEN_MAJOR = 1664
TR_K, TR_V, TR_G, TR_Q = 0, 384, 768, 784
N_FEATURE_MAJOR = TR_Q + W_C


def _norm_mod(x, g, scale, shift):
    ms = jnp.mean(x * x, axis=-1, keepdims=True)
    return (x * lax.rsqrt(ms + EPS) * g) * (1.0 + scale) + shift


def _gate_rows(graw):
    r = _iota(graw.shape, 0)
    ls = _log_sigmoid(graw)
    return jnp.where((r < H_C) | (r >= 12), ls, jnp.where(r < 8, 0.0, graw))


def _gate_cols(graw):
    ln = _iota(graw.shape, 1)
    return jnp.where(ln < H_A, _log_sigmoid(graw), jnp.where(ln < 2 * H_A, graw, 0.0))


def _inproj_common(x_ref, mod_ref, g_ref, w_ref, bn_ref, sn_ref, bt_ref, st_ref, qk_ref, va_ref, oa_ref, u_ref, gc_ref):
    h = _norm_mod(x_ref[...], g_ref[...], mod_ref[1, 0], mod_ref[0, 0])
    hb = h.astype(BF16)

    def seg(off, width):
        return (_dot_nt(hb, w_ref[off:off + width, :]) + bn_ref[:, off:off + width]) * sn_ref[:, off:off + width]

    def tseg(off, rows):
        return (_dot_nt(w_ref[N_TOKEN_MAJOR + off:N_TOKEN_MAJOR + off + rows, :], hb) + bt_ref[off:off + rows, :]) * st_ref[off:off + rows, :]

    qk_ref[...] = seg(NC_QA, 2 * QK_PAD).astype(BF16)
    va_ref[...] = seg(NC_VA, W_A).astype(BF16)
    oa_ref[...] = seg(NC_OA, W_A)
    u_ref[...] = seg(NC_U, W_B)
    gc_ref[...] = _gate_cols(seg(NC_G, LANES))
    return seg, tseg


def _inproj_prompt_kernel(x_ref, mod_ref, g_ref, w_ref, bn_ref, sn_ref, bt_ref, st_ref, tri_ref, *rest, tiles_per_seq, n_prev):
    prev = rest[0:2 * min(n_prev, 1)]
    (qk_ref, va_ref, oa_ref, u_ref, gc_ref, kaug_ref, qta_ref, vta_ref, kt_ref, vt_ref, gt_ref, ft_ref, nrm_ref, carry_ref) = rest[len(prev):]
    i = pl.program_id(0)
    tm = x_ref.shape[0]
    _, tseg = _inproj_common(x_ref, mod_ref, g_ref, w_ref, bn_ref, sn_ref, bt_ref, st_ref, qk_ref, va_ref, oa_ref, u_ref, gc_ref)

    kt = tseg(TR_K, W_C)
    vt = tseg(TR_V, W_C)
    gt = _gate_rows(tseg(TR_G, 16))
    qt = tseg(TR_Q, W_C)
    if n_prev:
        kt_ref[0:n_prev, 0] = prev[0][:, 0]
        vt_ref[0:n_prev, 0] = prev[1][:, 0]
    kt_ref[n_prev, 0] = kt
    vt_ref[n_prev, 0] = vt
    gt_ref[0] = gt

    first = (i % tiles_per_seq) == 0
    carry = jnp.where(first, 0.0, carry_ref[:, 0:1])
    ft = _dot3_lhs(gt[0:8] * LOG2E, tri_ref[...]) + carry
    carry_ref[...] = jnp.broadcast_to(ft[:, tm - 1:tm], carry_ref.shape)
    ft_ref[0] = ft
    f1, f2, f3 = [p.astype(F32) for p in _split3(ft)]

    nr, nl = _iota((8, LANES), 0), _iota((8, LANES), 1)
    nrm = jnp.zeros((8, LANES), F32)
    for hh in range(H_C):
        for col, x in ((0, qt), (1, kt)):
            xh = x[hh * DH_C:(hh + 1) * DH_C]
            big = jnp.sqrt(jnp.max(jnp.sum(xh * xh, axis=0, keepdims=True), axis=1, keepdims=True))
            nrm = jnp.where((nr == hh) & (nl == col), big, nrm)
    nrm_ref[0, 0] = nrm

    r8 = _iota((8, tm), 0)
    ktail = jnp.concatenate([jnp.where(r8 < 3, 1.0, 0.0), -f1, -f2, -f3,
                             jnp.zeros((LANES - AUG_F0 - 24, tm), F32)], axis=0)
    r64 = _iota((LANES - DH_C, tm), 0)
    for hh in range(H_C):
        kaug_ref[hh] = jnp.concatenate([kt[hh * DH_C:(hh + 1) * DH_C], ktail], axis=0).T.astype(BF16)
        qa = jnp.where(r64 == 0, f1[hh:hh + 1], jnp.where(r64 == 1, f2[hh:hh + 1], jnp.where(r64 == 2, f3[hh:hh + 1], 0.0)))
        qa = jnp.where((r64 == 8 + hh) | (r64 == 16 + hh) | (r64 == 24 + hh), 1.0, qa)
        qta_ref[0, hh] = jnp.concatenate([qt[hh * DH_C:(hh + 1) * DH_C].astype(BF16), qa.astype(BF16)], axis=0)
        vta_ref[0, hh] = jnp.concatenate([vt[hh * DH_C:(hh + 1) * DH_C].astype(BF16),
                                          jnp.where(r64 == 0, 1.0, 0.0).astype(BF16)], axis=0)


def _inproj_sample_kernel(x_ref, mod_ref, g_ref, w_ref, bn_ref, sn_ref, bt_ref, st_ref,
                          qk_ref, va_ref, oa_ref, u_ref, gc_ref, qc_ref, kc_ref, vc_ref, kt_ref, vt_ref, gt_ref):
    _, tseg = _inproj_common(x_ref, mod_ref, g_ref, w_ref, bn_ref, sn_ref, bt_ref, st_ref, qk_ref, va_ref, oa_ref, u_ref, gc_ref)
    kt = tseg(TR_K, W_C)
    vt = tseg(TR_V, W_C)
    kt_ref[...] = kt
    vt_ref[...] = vt
    gt_ref[...] = _gate_rows(tseg(TR_G, 16))
    qc_ref[...] = tseg(TR_Q, W_C).T
    kc_ref[...] = kt.T
    vc_ref[...] = vt.T


def _inproj_weights(w_in, b_in):
    o = np.cumsum((0, H_A * DK_A, H_A * DK_A, W_A, H_A, H_A, W_A, W_B, W_C, W_C, W_C, H_C))
    wt_full = w_in.T
    d = w_in.shape[0]
    rows = lambda k: (wt_full[o[k]:o[k + 1]], b_in[o[k]:o[k + 1]])
    (wqa, bqa), (wka, bka), (wva, bva), (wia, bia), (wfa, bfa), (woa, boa), (wu, bu), (wqc, bqc), (wkc, bkc), (wvc, bvc), (wfc, bfc) = [rows(k) for k in range(11)]
    zw = lambda n: jnp.zeros((n, d), F32)
    zb = lambda n: jnp.zeros((n,), F32)
    ws = [wqa, zw(QK_PAD - 192), wka, zw(QK_PAD - 192), wva, woa, wu, wfa, wia, zw(LANES - 8)]
    bs = [bqa, zb(QK_PAD - 192), bka, zb(QK_PAD - 192), bva, boa, bu, bfa, bia, zb(LANES - 8)]
    sn =jnp.concatenate([jnp.ones((QK_PAD,), F32), jnp.full((QK_PAD,), DK_A ** -0.5, F32),
                          jnp.ones((N_TOKEN_MAJOR - 2 * QK_PAD,), F32)])[None, :]
    bn = jnp.concatenate(bs)[None, :]
    ws += [wkc, wvc, wfc, zw(2), wia, wfa, wqc]
    bt = jnp.concatenate([bkc, bvc, bfc, zb(2), bia, bfa, bqc])[:, None]
    st = jnp.concatenate([jnp.ones((TR_Q,), F32), jnp.full((W_C,), DH_C ** -0.5 * LOG2E, F32)])[:, None]
    return jnp.concatenate(ws, axis=0).astype(BF16), bn, sn, bt, st


def _inproj_prompt(x2, mod4, g1, weights, batch, seq, layer, kv_prev):
    w, bn, sn, bt, st = weights
    rows, d = x2.shape
    tm = min(512, seq)
    tps = seq // tm
    tri = jnp.asarray(np.triu(np.ones((tm, tm), np.float32)), BF16)
    full = lambda a: pl.BlockSpec(a.shape, lambda i: (0,) * a.ndim)
    rowblk = lambda wd: pl.BlockSpec((tm, wd), lambda i: (i, 0))
    tblk = lambda r: pl.BlockSpec((1, r, tm), lambda i: (i // tps, 0, i % tps))
    tblk4 = pl.BlockSpec((1, H_C, LANES, tm), lambda i: (i // tps, 0, 0, i % tps))
    kvblk = lambda n: pl.BlockSpec((n, 1, W_C, tm), lambda i: (0, i // tps, 0, i % tps))
    prev_in = [] if kv_prev is None else list(kv_prev)
    outs = pl.pallas_call(
        functools.partial(_inproj_prompt_kernel, tiles_per_seq=tps, n_prev=layer),
        grid=(rows // tm,),
        in_specs=[rowblk(d), pl.BlockSpec((6, 1, 1, d), lambda i: (0, i // tps, 0, 0)), full(g1),
                  full(w), full(bn), full(sn), full(bt), full(st), full(tri)] + [kvblk(layer)] * len(prev_in),
        out_specs=[rowblk(2 * QK_PAD), rowblk(W_A), rowblk(W_A), rowblk(W_B), rowblk(LANES),
                   pl.BlockSpec((H_C, tm, LANES), lambda i: (0, i, 0)), tblk4, tblk4, kvblk(layer + 1), kvblk(layer + 1), tblk(16),
                   tblk(8), pl.BlockSpec((1, 1, 8, LANES), lambda i: (i // tps, i % tps, 0, 0))],
        out_shape=[jax.ShapeDtypeStruct((rows, 2 * QK_PAD), BF16), jax.ShapeDtypeStruct((rows, W_A), BF16),
                   jax.ShapeDtypeStruct((rows, W_A), F32), jax.ShapeDtypeStruct((rows, W_B), F32),
                   jax.ShapeDtypeStruct((rows, LANES), F32),
                   jax.ShapeDtypeStruct((H_C, rows, LANES), BF16),
                   jax.ShapeDtypeStruct((batch, H_C, LANES, seq), BF16), jax.ShapeDtypeStruct((batch, H_C, LANES, seq), BF16),
                   jax.ShapeDtypeStruct((layer + 1, batch, W_C, seq), F32), jax.ShapeDtypeStruct((layer + 1, batch, W_C, seq), F32),
                   jax.ShapeDtypeStruct((batch, 16, seq), F32),
                   jax.ShapeDtypeStruct((batch, 8, seq), F32), jax.ShapeDtypeStruct((batch, tps, 8, LANES), F32)],
        scratch_shapes=[pltpu.VMEM((8, LANES), F32)],
        compiler_params=pltpu.CompilerParams(dimension_semantics=("arbitrary",), vmem_limit_bytes=VMEM_LIMIT),
        name="inproj_prompt",
    )(x2, mod4, g1, w, bn, sn, bt, st, tri, *prev_in)
    return outs


def _inproj_sample(x2, mod4, g1, weights):
    w, bn, sn, bt, st = weights
    rows, d = x2.shape
    full = lambda a: pl.BlockSpec(a.shape, lambda i: (0,) * a.ndim)
    o2 = lambda r, c: pl.BlockSpec((r, c), lambda i: (0, 0))
    sds = lambda r, c, t: jax.ShapeDtypeStruct((r, c), t)
    return pl.pallas_call(
        _inproj_sample_kernel,
        grid=(1,),
        in_specs=[full(x2), full(mod4), full(g1), full(w), full(bn), full(sn), full(bt), full(st)],
        out_specs=[o2(rows, 2 * QK_PAD), o2(rows, W_A), o2(rows, W_A), o2(rows, W_B), o2(rows, LANES),
                   o2(rows, W_C), o2(rows, W_C), o2(rows, W_C), o2(W_C, rows), o2(W_C, rows), o2(16, rows)],
        out_shape=[sds(rows, 2 * QK_PAD, BF16), sds(rows, W_A, BF16), sds(rows, W_A, F32), sds(rows, W_B, F32), sds(rows, LANES, F32),
                   sds(rows, W_C, F32), sds(rows, W_C, F32), sds(rows, W_C, F32), sds(W_C, rows, F32), sds(W_C, rows, F32), sds(16, rows, F32)],
        compiler_params=pltpu.CompilerParams(dimension_semantics=("arbitrary",), vmem_limit_bytes=VMEM_LIMIT),
        name="inproj_sample",
    )(x2, mod4, g1, w, bn, sn, bt, st)


def _state_mask():
    r = np.arange(QK_PAD)[:, None]
    c = np.arange(ST_COLS)[None, :]
    m = np.zeros((QK_PAD, ST_COLS), np.float32)
    for h in range(H_A):
        rows = (r >= h * DK_A) & (r < (h + 1) * DK_A)
        cols = ((c >= h * DV_A) & (c < (h + 1) * DV_A)) | (c == W_A + h)
        m[rows & cols] = 1.0
    return m


def _seq_kernel(*refs, pos0, group, interleave):
    chains = [_seq_one(g, *refs, pos0=pos0) for g in range(group)]
    if not interleave:
        for ch in chains:
            for _ in ch:
                pass
        return
    for lead in range(group - 1):
        for ch in chains[:group - 1 - lead]:
            next(ch)
    while chains:
        chains = [ch for ch in chains if next(ch, _DONE) is not _DONE]


_DONE = object()


def _seq_one(g, qk_ref, v_ref, o_ref, u_ref, gc_ref, gt_ref, st0_ref, m0_ref, hist0_ref, gh_ref, wp_ref, sp_ref, mask_ref,
             mix_ref, st_out_ref, m_out_ref, hist_out_ref, st_scr, m_scr, z_scr, *, pos0):
    c = pl.program_id(1)
    L = CHUNK

    @pl.when(c == 0)
    def _():
        st_scr[g] = st0_ref[g]
        m_scr[g] = m0_ref[g]
        z_scr[g, 0:16, :] = hist0_ref[g]

    q = qk_ref[g, :, 0:QK_PAD]
    k = qk_ref[g, :, QK_PAD:2 * QK_PAD]
    v = v_ref[g]
    gc = gc_ref[g]
    gt = gt_ref[g]
    st = st_scr[g]
    m_row = m_scr[g, 0:1, :]

    row = _iota((L, L), 0)
    colm = _iota((L, L), 1)
    tri_l = jnp.where(colm <= row, 1.0, 0.0).astype(BF16)
    tri_u = jnp.where(row <= colm, 1.0, 0.0).astype(BF16)
    lane128 = _iota((1, LANES), 1)
    lane_q = _iota((1, QK_PAD), 1)
    lane_v = _iota((1, W_A), 1)

    u = u_ref[g]
    z_scr[g, 16:16 + L, :] = u
    z = z_scr[g]
    s2 = z + pltpu.roll(z, 1, axis=0)
    s4 = s2 + pltpu.roll(s2, 2, axis=0)
    s8 = s4 + pltpu.roll(s4, 4, axis=0)
    s16 = s8 + pltpu.roll(s8, 8, axis=0)
    navail = (pos0 + c * L + 1 + _iota((L, 1), 0)).astype(F32)
    lane_u = _iota((1, W_B), 1)
    y = None
    for gi, (w, sw) in enumerate(zip(POOL_WINDOWS, (s2, s4, s8, s16))):
        yg = sw[16:16 + L, :] * (1.0 / jnp.minimum(float(w), navail))
        y = yg if y is None else jnp.where(lane_u >= gi * CG_B, yg, y)
    y = y - u
    hb = _dot(y.astype(BF16), wp_ref[...]) * sp_ref[...]
    z_scr[g, 0:16, :] = u[L - 16:L, :]

    bc_col = _dot3_rhs(tri_l, gc)
    bc_row = _dot3_lhs(gt[8:16], tri_u)
    qk_h = [_dot_nt(jnp.where((lane_q >= h * DK_A) & (lane_q < (h + 1) * DK_A), q, jnp.zeros_like(q)), k) for h in range(H_A)]
    qs = _dot(q, st.astype(BF16))
    yield
    inter = bc_col + m_row
    ia_col = pltpu.roll(gc, LANES - H_A, axis=1)

    causal = colm <= row
    mt_all = jnp.zeros((L, LANES), F32)
    s_list = []
    vblk = []
    for h in range(H_A):
        d = bc_col[:, h:h + 1] - bc_row[4 + h:5 + h, :] + gt[8 + h:9 + h, :]
        d = jnp.where(causal, d, NEG)
        mt = jnp.maximum(inter[:, h:h + 1], jnp.max(d, axis=-1, keepdims=True))
        s = qk_h[h] * jnp.exp(d - mt)
        s_list.append(s.astype(BF16))
        mt_all = jnp.where(lane128 == h, mt, mt_all)
        vm = jnp.where((lane_v >= h * DV_A) & (lane_v < (h + 1) * DV_A), v, jnp.zeros_like(v))
        vblk.append(jnp.concatenate([vm, jnp.broadcast_to(jnp.where(lane128 == h, 1.0, 0.0).astype(BF16), (L, LANES))], axis=1))
    s_cat = jnp.concatenate(s_list, axis=1)
    v_blk = jnp.concatenate(vblk, axis=0)
    pv = _dot(s_cat, v_blk)
    yield

    valid = lane128 < H_A
    m_new = mt_all[L - 1:L, :]
    g_col = jnp.where(valid, jnp.exp(bc_col[L - 1:L, :] - bc_col + ia_col - m_new), 0.0)
    a_last = jnp.where(valid, jnp.exp(inter[L - 1:L, :] - m_new), 0.0)
    gv = (v.astype(F32) * _expand_heads(g_col, W_A, DV_A)).astype(BF16)
    upd = _dot_tn(k, jnp.concatenate([gv, g_col.astype(BF16)], axis=1))
    yield
    a512 = jnp.concatenate([_expand_heads(a_last, W_A, DV_A), a_last], axis=1)
    st_new = a512 * st + mask_ref[...] * upd
    st_scr[g] = st_new
    m_rows = jnp.broadcast_to(jnp.where(valid, m_new, 0.0), (8, LANES))
    m_scr[g] = m_rows

    a_all = jnp.where(valid, jnp.exp(inter - mt_all), 0.0)
    num = _expand_heads(a_all, W_A, DV_A) * qs[:, 0:W_A] + pv[:, 0:W_A]
    den = a_all * qs[:, W_A:] + pv[:, W_A:]
    inv = 1.0 / jnp.maximum(jnp.abs(den), jnp.exp(-mt_all))
    hs = num * _expand_heads(jnp.where(valid, inv, 0.0), W_A, DV_A)

    sq = hs * hs
    ss_all = jnp.zeros((L, LANES), F32)
    for h in range(H_A):
        ssum = jnp.sum(jnp.where((lane_v >= h * DV_A) & (lane_v < (h + 1) * DV_A), sq, 0.0), axis=-1, keepdims=True)
        ss_all = jnp.where(lane128 == h, ssum, ss_all)
    r_all = lax.rsqrt(ss_all * (1.0 / DV_A) + EPS)
    ha = _sigmoid(o_ref[g]) * hs * _expand_heads(r_all, W_A, DV_A) * gh_ref[...]
    mix_ref[g] = jnp.concatenate([ha, hb], axis=1).astype(BF16)

    @pl.when(c == pl.num_programs(1) - 1)
    def _():
        st_out_ref[g] = st_new
        m_out_ref[g] = m_rows
        hist_out_ref[g] = u[L - 16:L, :]


def _sequence(qk, va, oa, u, gc, gt, st0, m0, hist0, gh, wp, sp, nb, nchunk, pos0):
    L = CHUNK
    G = SEQ_GROUP
    assert nb % G == 0
    mask = jnp.asarray(_state_mask())
    seq3 = lambda a: a.reshape(nb, nchunk * L, a.shape[-1])
    rowblk = lambda w: pl.BlockSpec((G, L, w), lambda b, c: (b, c, 0))
    perb = lambda a: pl.BlockSpec((G,) + a.shape[1:], lambda b, c: (b,) + (0,) * (a.ndim - 1))
    full = lambda a: pl.BlockSpec(a.shape, lambda b, c: (0,) * a.ndim)
    outs = pl.pallas_call(
        functools.partial(_seq_kernel, pos0=pos0, group=G, interleave=nchunk == 1),
        grid=(nb // G, nchunk),
        in_specs=[rowblk(2 * QK_PAD), rowblk(W_A), rowblk(W_A), rowblk(W_B), rowblk(LANES),
                  pl.BlockSpec((G, 16, L), lambda b, c: (b, 0, c)),
                  perb(st0), perb(m0), perb(hist0), full(gh), full(wp), full(sp), full(mask)],
        out_specs=[rowblk(MIX_AB), pl.BlockSpec((G, QK_PAD, ST_COLS), lambda b, c: (b, 0, 0)),
                   pl.BlockSpec((G, 8, LANES), lambda b, c: (b, 0, 0)), pl.BlockSpec((G, 16, W_B), lambda b, c: (b, 0, 0))],
        out_shape=[jax.ShapeDtypeStruct((nb, nchunk * L, MIX_AB), BF16), jax.ShapeDtypeStruct((nb, QK_PAD, ST_COLS), F32),
                   jax.ShapeDtypeStruct((nb, 8, LANES), F32), jax.ShapeDtypeStruct((nb, 16, W_B), F32)],
        scratch_shapes=[pltpu.VMEM((G, QK_PAD, ST_COLS), F32), pltpu.VMEM((G, 8, LANES), F32), pltpu.VMEM((G, 16 + L, W_B), F32)],
        compiler_params=pltpu.CompilerParams(dimension_semantics=("arbitrary", "arbitrary"), vmem_limit_bytes=VMEM_LIMIT),
        name="sequence",
    )(seq3(qk), seq3(va), seq3(oa), seq3(u), seq3(gc), gt, st0, m0, hist0, gh, wp, sp, mask)
    return (outs[0].reshape(nb * nchunk * L, MIX_AB),) + tuple(outs[1:])


def _pack_state(C, n, m):
    nb = C.shape[0]
    z = lambda r, c: jnp.zeros((nb, r, c), F32)
    blocks = []
    for h in range(H_A):
        blocks.append(jnp.concatenate([z(DK_A, h * DV_A), jnp.swapaxes(C[:, h], 1, 2), z(DK_A, W_A - (h + 1) * DV_A),
                                       z(DK_A, h), n[:, h][:, :, None], z(DK_A, LANES - h - 1)], axis=2))
    st = jnp.concatenate(blocks + [z(QK_PAD - H_A * DK_A, ST_COLS)], axis=1)
    mm = jnp.concatenate([jnp.broadcast_to(m[:, None, :], (nb, 8, H_A)), z(8, LANES - H_A)], axis=2)
    return st, mm


def _unpack_state(st, mm):
    C = jnp.stack([jnp.swapaxes(st[:, h * DK_A:(h + 1) * DK_A, h * DV_A:(h + 1) * DV_A], 1, 2) for h in range(H_A)], axis=1)
    n = jnp.stack([st[:, h * DK_A:(h + 1) * DK_A, W_A + h] for h in range(H_A)], axis=1)
    return C, n, mm[:, 0, 0:H_A]


def _attn_prompt_kernel(skip_ref, qt_ref, k_ref, vt_ref, o_ref, s_scr, m_scr, acc_scr, *, tq, tk, sw):
    qi = pl.program_id(2)
    m_scr[...] = jnp.full(m_scr.shape, NEG, F32)
    acc_scr[...] = jnp.zeros(acc_scr.shape, F32)
    units = [(hh, st) for hh in range(2) for st in range(tq // sw)]

    def qk_block(j, slot, skip=()):
        start = pl.multiple_of(j * tk, tk)
        for hh, st in units:
            if st not in skip:
                s_scr[slot, hh, :, st * sw:(st + 1) * sw] = _dot(k_ref[hh, pl.ds(start, tk), :], qt_ref[0, hh, :, st * sw:(st + 1) * sw])

    def softmax_pv_block(j, slot, masked=(), skip=()):
        start = pl.multiple_of(j * tk, tk)
        for hh, st in units:
            if st in skip:
                continue
            strip = slice(st * sw, (st + 1) * sw)
            s = s_scr[slot, hh, :, strip]
            if st in masked:
                keyg = j * tk + _iota((tk, sw), 0)
                qryg = qi * tq + st * sw + _iota((tk, sw), 1)
                s = jnp.where(keyg <= qryg, s, NEG)
            m = m_scr[hh, 0:1, strip]
            m_new = jnp.maximum(m, jnp.max(s, axis=0, keepdims=True))
            p = jnp.exp2(s - m_new)
            pv = _dot(vt_ref[0, hh, :, pl.ds(start, tk)], p.astype(BF16))
            acc_scr[hh, :, strip] = jnp.exp2(m - m_new) * acc_scr[hh, :, strip] + pv
            m_scr[hh, 0:1, strip] = m_new

    n_full = (qi * tq) // tk
    first_pair = jnp.minimum(skip_ref[(pl.program_id(0) * pl.num_programs(1) + pl.program_id(1)) * pl.num_programs(2) + qi],
                             n_full // 2)
    qk_block(2 * first_pair, 0)

    def body(i, carry):
        j = 2 * i
        qk_block(j + 1, 1)
        softmax_pv_block(j, 0)
        qk_block(j + 2, 0)
        softmax_pv_block(j + 1, 1)
        return carry

    lax.fori_loop(first_pair, n_full // 2, body, 0)
    n_diag = tq // tk
    for d in range(n_diag):
        if d + 1 < n_diag:
            qk_block(n_full + d + 1, (d + 1) % 2, skip=tuple(range(d + 1)))
        softmax_pv_block(n_full + d, d % 2, masked=(d,), skip=tuple(range(d)))
    outs = []
    for hh in range(2):
        acc = acc_scr[hh]
        outs.append((acc * (1.0 / acc[DH_C:DH_C + 1, :])).T)
    lane = _iota((1, LANES), 1)
    o_ref[...] = jnp.where(lane < DH_C, outs[0], pltpu.roll(outs[1], DH_C, axis=1)).astype(BF16)


def _dead_block_pairs(ft, qk_norm, seq, tq, tk):
    nb = ft.shape[0]
    nq, nk = seq // tq, seq // tk
    tiles = qk_norm.shape[1]
    qmax = jnp.max(qk_norm[:, :, 0:H_C, 0].reshape(nb, nq, tiles // nq, H_C), axis=2)
    kmax = jnp.max(qk_norm[:, :, 0:H_C, 1], axis=1)
    f_start = jnp.transpose(ft[:, 0:H_C, 0::tq], (0, 2, 1))
    f_end = jnp.transpose(ft[:, 0:H_C, tk - 1::tk], (0, 2, 1))
    bound = (NORM_SLACK * 2.0) * (qmax * kmax[:, None, :])[:, :, None, :] + f_start[:, :, None, :] - f_end[:, None, :, :]
    dead = bound < -DEAD_LOG2
    dead = jnp.all(dead.reshape(nb, nq, nk // 2, 2, H_C // 2, 2), axis=(3, 5))
    pair = jnp.arange(nk // 2, dtype=jnp.int32)[None, None, :, None]
    n_dead = jnp.min(jnp.where(dead, nk // 2, pair), axis=2)
    return jnp.transpose(n_dead, (0, 2, 1)).reshape(-1).astype(jnp.int32)


def _attn_prompt(qta, k_aug, vta, ft, qk_norm, batch, seq):
    rows = batch * seq
    tk = sw = ATTN_STRIP
    tq = min(ATTN_Q_TILE, seq)
    assert seq % tq == 0 and (tq // tk) % 2 == 0
    nq = seq // tq
    grid_spec = pltpu.PrefetchScalarGridSpec(
        num_scalar_prefetch=1, grid=(batch, H_C // 2, nq),
        in_specs=[pl.BlockSpec((1, 2, LANES, tq), lambda b, p, i, sk: (b, p, 0, i)),
                  pl.BlockSpec((2, seq, LANES), lambda b, p, i, sk: (p, b, 0)),
                  pl.BlockSpec((1, 2, LANES, seq), lambda b, p, i, sk: (b, p, 0, 0))],
        out_specs=pl.BlockSpec((tq, LANES), lambda b, p, i, sk: (b * nq + i, p)),
        scratch_shapes=[pltpu.VMEM((2, 2, tk, tq), F32), pltpu.VMEM((2, 8, tq), F32), pltpu.VMEM((2, LANES, tq), F32)])
    return pl.pallas_call(
        functools.partial(_attn_prompt_kernel, tq=tq, tk=tk, sw=sw),
        grid_spec=grid_spec,
        out_shape=jax.ShapeDtypeStruct((rows, W_C), BF16),
        compiler_params=pltpu.CompilerParams(dimension_semantics=("arbitrary", "arbitrary", "arbitrary"), vmem_limit_bytes=VMEM_LIMIT),
        name="attn_prompt",
    )(_dead_block_pairs(ft, qk_norm, seq, tq, tk), qta, k_aug, vta)


def _when(cond):
    if isinstance(cond, bool):
        return (lambda f: f()) if cond else (lambda f: None)
    return pl.when(cond)


def _sample_attn_step(t, n_total, first, last, P, pt_ref, q_ref, knt_ref, vnt_ref, lfn_ref, ck_hbm, cv_hbm, clf_hbm, o_ref,
                      kbuf, vbuf, lbuf, sems, m_scr, acc_scr, carry_scr, cq_scr):
    slot = t % 2

    def page_copies(step, sl):
        out = []
        for i in range(P):
            pg = pt_ref[step * P + i]
            out += [pltpu.make_async_copy(ck_hbm.at[pg], kbuf.at[sl, i], sems.at[sl, 0]),
                    pltpu.make_async_copy(cv_hbm.at[pg], vbuf.at[sl, i], sems.at[sl, 1]),
                    pltpu.make_async_copy(clf_hbm.at[pg], lbuf.at[sl, i], sems.at[sl, 2])]
        return out

    @_when(first and (t == 0))
    def _():
        for cp in page_copies(0, 0):
            cp.start()

    @pl.when(t + 1 < n_total)
    def _():
        for cp in page_copies(t + 1, 1 - slot):
            cp.start()

    for cp in page_copies(t, slot):
        cp.wait()

    R = 8 * SAMPLE_Q
    head_of_lane = lax.shift_right_logical(_iota((R, W_C), 1), 6)
    headmask = head_of_lane == (_iota((R, W_C), 0) & 7)
    lane = _iota((R, LANES), 1)
    rq = lax.shift_right_logical(_iota((R, LANES), 0), 3)

    def tile_q(x8):
        return jnp.concatenate([x8] * SAMPLE_Q, axis=0)

    q8 = q_ref[0]
    qrep = jnp.concatenate([jnp.broadcast_to(q8[qq:qq + 1], (8, W_C)) for qq in range(SAMPLE_Q)], axis=0)
    qbd = jnp.where(headmask, qrep, 0.0).astype(BF16)

    @_when(first)
    def _():
        a = tile_q(lfn_ref[0] * LOG2E)
        cq = jnp.sum(jnp.where(lane <= rq, a, 0.0), axis=-1, keepdims=True)
        incl = jnp.where(_iota((LANES, LANES), 0) <= _iota((LANES, LANES), 1), 1.0, 0.0).astype(BF16)
        crow = _dot3_lhs(a, incl)
        s = _dot(qbd, knt_ref[0].astype(BF16)) + cq - crow
        s = jnp.where((lane <= rq) & (lane < SAMPLE_Q), s, NEG)
        m = jnp.max(s, axis=-1, keepdims=True)
        p = jnp.exp2(s - m)
        l = jnp.sum(p, axis=-1, keepdims=True)
        acc_scr[:, 0:W_C] = _dot_nt(p.astype(BF16), vnt_ref[0].astype(BF16))
        acc_scr[:, W_C:] = jnp.broadcast_to(l, (R, LANES))
        m_scr[...] = jnp.broadcast_to(m, (R, LANES))
        cq_scr[...] = jnp.broadcast_to(cq, (R, LANES))
        carry_scr[...] = jnp.zeros_like(carry_scr)

    cq = cq_scr[:, 0:1]
    strict = jnp.where(_iota((LANES, LANES), 0) > _iota((LANES, LANES), 1), 1.0, 0.0).astype(BF16)
    lf_all = lbuf[slot].reshape(8 * P, LANES) * LOG2E
    suf = _dot3_lhs(lf_all, strict)
    carry = carry_scr[:, 0:1]
    bias = [None] * P
    for i in range(P):
        bias[i] = tile_q(suf[8 * i:8 * i + 8] + carry)
        carry = carry + suf[8 * i:8 * i + 8, 0:1] + lf_all[8 * i:8 * i + 8, 0:1]
    carry_scr[...] = jnp.broadcast_to(carry, carry_scr.shape)

    kcat = jnp.concatenate([kbuf[slot, i].astype(BF16) for i in range(P)], axis=1)
    vcat = jnp.concatenate([vbuf[slot, i].astype(BF16) for i in range(P)], axis=1)
    s = _dot(qbd, kcat) + jnp.concatenate(bias, axis=1) + cq
    yield
    m_old = m_scr[:, 0:1]
    m_new = jnp.maximum(m_old, jnp.max(s, axis=-1, keepdims=True))
    alpha = jnp.exp2(m_old - m_new)
    p = jnp.exp2(s - m_new)
    l = alpha * acc_scr[:, W_C:W_C + 1] + jnp.sum(p, axis=-1, keepdims=True)
    acc = alpha * acc_scr[:, 0:W_C] + _dot_nt(p.astype(BF16), vcat)
    acc_scr[:, 0:W_C] = acc
    acc_scr[:, W_C:] = jnp.broadcast_to(l, (R, LANES))
    m_scr[...] = jnp.broadcast_to(m_new, (R, LANES))

    @_when(last)
    def _():
        o = jnp.where(headmask, acc * (1.0 / l), 0.0)
        rows = [jnp.sum(o[8 * qq:8 * qq + 8], axis=0, keepdims=True) for qq in range(SAMPLE_Q)]
        o_ref[0] = jnp.concatenate(rows + [jnp.zeros((8 - SAMPLE_Q, W_C), F32)], axis=0)


def _sample_attn_scratch(P):
    R = 8 * SAMPLE_Q
    return [pltpu.VMEM((2, P, W_C, LANES), F32), pltpu.VMEM((2, P, W_C, LANES), F32), pltpu.VMEM((2, P, 8, LANES), F32),
            pltpu.SemaphoreType.DMA((2, 3)),
            pltpu.VMEM((R, LANES), F32), pltpu.VMEM((R, W_C + LANES), F32), pltpu.VMEM((8, LANES), F32), pltpu.VMEM((R, LANES), F32)]


def _mlp_kernel(*refs, final, n_groups, pages_per_group):
    fused = n_groups > 0
    if fused:
        pt_ref, refs = refs[0], refs[1:]
    x_ref, mab_ref, hc_ref, mod_ref, g2_ref, gf_ref, wo_ref, wu_ref, wd_ref = refs[0:9]
    rest = refs[9:]
    if fused:
        sample_in, rest = rest[0:7], rest[7:]
    n_out = (2 if final else 1) + (1 if fused else 0)
    out_refs, scratch = rest[0:n_out], rest[n_out:]

    x = x_ref[...]
    mix = _dot(mab_ref[...], wo_ref[0, 0:MIX_AB, :]) + _dot(hc_ref[...], wo_ref[0, MIX_AB:, :])
    x1 = x + mod_ref[2, 0] * mix
    h2 = _norm_mod(x1, g2_ref[...], mod_ref[4, 0], mod_ref[3, 0]).astype(BF16)
    ff = jnp.zeros_like(x)
    n_chunks = n_groups if fused else MLP_CHUNKS
    fc = D_FF // n_chunks
    for f in range(n_chunks):
        a = jnp.maximum(_dot(h2, wu_ref[0, :, f * fc:(f + 1) * fc]), 0.0)
        if fused:
            step = _sample_attn_step(pl.program_id(0) * n_groups + f, pl.num_programs(0) * n_groups, f == 0, f == n_groups - 1,
                                     pages_per_group, pt_ref, *sample_in, out_refs[n_out - 1], *scratch)
            next(step)
        ff = ff + _dot((a * a).astype(BF16), wd_ref[0, f * fc:(f + 1) * fc, :])
        if fused:
            next(step, None)
    x2 = x1 + mod_ref[5, 0] * ff
    out_refs[0][...] = x2
    if final:
        ms = jnp.mean(x2 * x2, axis=-1, keepdims=True)
        out_refs[1][...] = x2 * lax.rsqrt(ms + EPS) * gf_ref[...]


def _mlp(x2, mab, hc, mod4, g2, gf, wo, wu, wd, layer, tm, rows_per_mod, final, sample=None):
    rows, d = x2.shape
    s = mod4.shape[2]
    fused = sample is not None
    im = (lambda f: (lambda i, pt: f(i))) if fused else (lambda f: f)
    const = lambda a: pl.BlockSpec(a.shape, im(lambda i: (0,) * a.ndim), pipeline_mode=pl.Buffered(1))
    layerw = lambda a: pl.BlockSpec((1,) + a.shape[1:], im(lambda i: (layer, 0, 0)), pipeline_mode=pl.Buffered(1))
    rowblk = lambda w: pl.BlockSpec((tm, w), im(lambda i: (i, 0)))
    n_out = 2 if final else 1
    in_specs = [rowblk(d), rowblk(MIX_AB), rowblk(W_C),
                pl.BlockSpec((6, 1, s, d), im(lambda i: (0, (i * tm) // rows_per_mod, 0, 0))),
                const(g2), const(gf), layerw(wo), layerw(wu), layerw(wd)]
    out_specs = [rowblk(d)] * n_out
    out_shape = [jax.ShapeDtypeStruct((rows, d), F32)] * n_out
    args = [x2, mab, hc, mod4, g2, gf, wo, wu, wd]
    scratch, n_groups, P, n_prefetch = [], 0, 0, 0
    if fused:
        page_ids, q8, knt, vnt, lfn = sample[0:5]
        nb, n_pages = q8.shape[0], page_ids.shape[0] // q8.shape[0]
        assert nb == rows // tm
        P = min(SAMPLE_PAGES_PER_STEP, n_pages)
        n_groups = n_pages // P
        assert D_FF % (n_groups * LANES) == 0
        perb = lambda a: pl.BlockSpec((1,) + a.shape[1:], lambda i, pt: (i, 0, 0))
        in_specs += [perb(q8), perb(knt), perb(vnt), perb(lfn)] + [pl.BlockSpec(memory_space=pl.ANY)] * 3
        out_specs += [pl.BlockSpec((1, 8, W_C), lambda i, pt: (i, 0, 0))]
        out_shape += [jax.ShapeDtypeStruct((nb, 8, W_C), F32)]
        args = [page_ids] + args + list(sample[1:])
        scratch, n_prefetch = _sample_attn_scratch(P), 1
    grid_spec = pltpu.PrefetchScalarGridSpec(num_scalar_prefetch=n_prefetch, grid=(rows // tm,), in_specs=in_specs,
                                             out_specs=out_specs, scratch_shapes=scratch)
    return pl.pallas_call(
        functools.partial(_mlp_kernel, final=final, n_groups=n_groups, pages_per_group=P),
        grid_spec=grid_spec,
        out_shape=out_shape,
        compiler_params=pltpu.CompilerParams(dimension_semantics=("arbitrary",), vmem_limit_bytes=VMEM_LIMIT),
        name="mlp_attn_sample" if fused else "mlp",
    )(*args)


def kernel(x_prompt, x_sample, c_prompt, c_sample, cache_k, cache_v, cache_logf, page_table, state_C, state_n, state_m, state_pool, w_ada, b_ada, g_norm1, g_norm2, w_in, b_in, g_head_a, w_pool, s_pool, w_out, w_up, w_down, g_final):
    depth = w_ada.shape[0]
    B, T, D = x_prompt.shape
    SB, ST, _ = x_sample.shape
    n_phys, page = cache_k.shape[1], cache_k.shape[2]
    n_pages = page_table.shape[1]
    rows_p, rows_s = B * T, SB * ST
    pos0_s = n_pages * page
    assert T % CHUNK == 0 and ST == SAMPLE_Q and page == LANES

    nc = B + SB
    c_all = jnp.concatenate([c_prompt, c_sample, jnp.zeros((-nc % 8, D), F32)], axis=0)
    mod = _modulation(c_all, w_ada, b_ada)

    ckt = jnp.transpose(cache_k, (0, 1, 3, 4, 2)).reshape(depth * n_phys, W_C, page)
    cvt = jnp.transpose(cache_v, (0, 1, 3, 4, 2)).reshape(depth * n_phys, W_C, page)
    clf = jnp.pad(jnp.transpose(cache_logf, (0, 1, 3, 2)), ((0, 0), (0, 0), (0, 8 - H_C), (0, 0))).reshape(depth * n_phys, 8, page)
    wo_all, wu_all, wd_all = w_out.astype(BF16), w_up.astype(BF16), w_down.astype(BF16)

    xp = x_prompt.reshape(rows_p, D)
    xs = x_sample.reshape(rows_s, D)
    zeros_state = (jnp.zeros((B, QK_PAD, ST_COLS), F32), jnp.zeros((B, 8, LANES), F32))
    outs_p, outs_s = [], []
    yp = ys = None
    kv_all = None
    for l in range(depth):
        final = l == depth - 1
        g1, g2, gf = g_norm1[l][None, :], g_norm2[l][None, :], g_final[None, :]
        gh = g_head_a[l].reshape(1, W_A)
        wp = jax.scipy.linalg.block_diag(*[w_pool[l, g] for g in range(len(POOL_WINDOWS))]).astype(BF16)
        sp = s_pool[l][None, :]
        weights = _inproj_weights(w_in[l], b_in[l])
        modl = mod[l].reshape(-1, 6, D)
        mod_p = jnp.transpose(modl[0:B], (1, 0, 2))[:, :, None, :]
        mod_s = jnp.transpose(jnp.repeat(modl[B:B + SB], ST, axis=0), (1, 0, 2))[:, None]

        (qk, va, oa, u, gc, k_aug, qta, vta, kt_all, vt_all, gt, ft, qk_norm) = _inproj_prompt(xp, mod_p, g1, weights, B, T, l, kv_all)
        kv_all = (kt_all, vt_all)
        mab, st1, m1, hist1 = _sequence(qk, va, oa, u, gc, gt, zeros_state[0], zeros_state[1], jnp.zeros((B, 16, W_B), F32),
                                        gh, wp, sp, B, T // CHUNK, 0)
        hc = _attn_prompt(qta, k_aug, vta, ft, qk_norm, B, T)
        mab_p, hc_p = mab, hc
        C1, n1, mm1 = _unpack_state(st1, m1)
        outs_p.append((jnp.transpose(gt[:, 0:H_C, :], (0, 2, 1)), C1, n1, mm1, hist1[:, 1:]))

        (qk, va, oa, u, gc, qc, kc, vc, kt, vt, gt) = _inproj_sample(xs, mod_s, g1, weights)

        def pad_rows(a, fill=0.0):
            a3 = a.reshape(SB, ST, a.shape[-1])
            return jnp.pad(a3, ((0, 0), (0, CHUNK - ST), (0, 0)), constant_values=fill).reshape(SB * CHUNK, a.shape[-1])

        lane = jnp.arange(LANES)[None, :]
        gc_pad = jnp.where((lane >= H_A) & (lane < 2 * H_A), pad_rows(gc, NEG), pad_rows(gc))
        gt3 = jnp.transpose(gt.reshape(16, SB, ST), (1, 0, 2))
        rr = jnp.arange(16)[None, :, None]
        gt_pad = jnp.where((rr >= 8) & (rr < 12), jnp.pad(gt3, ((0, 0), (0, 0), (0, CHUNK - ST)), constant_values=NEG),
                           jnp.pad(gt3, ((0, 0), (0, 0), (0, CHUNK - ST))))
        st0, m0 = _pack_state(state_C[l], state_n[l], state_m[l])
        hist0 = jnp.pad(state_pool[l], ((0, 0), (1, 0), (0, 0)))
        mab, st1, m1, _ = _sequence(pad_rows(qk), pad_rows(va), pad_rows(oa), pad_rows(u), gc_pad, gt_pad, st0, m0, hist0,
                                    gh, wp, sp, SB, 1, pos0_s)
        mab = mab.reshape(SB, CHUNK, MIX_AB)[:, 0:ST].reshape(rows_s, MIX_AB)
        q8 = jnp.pad(qc.reshape(SB, ST, W_C), ((0, 0), (0, 8 - ST), (0, 0)))
        tpad = lambda a: jnp.pad(jnp.transpose(a.reshape(a.shape[0], SB, ST), (1, 0, 2)), ((0, 0), (0, 0), (0, LANES - ST)))
        knt, vnt = tpad(kt), tpad(vt)
        lfn = jnp.where(rr < H_C, jnp.pad(gt3, ((0, 0), (0, 0), (0, LANES - ST))), 0.0)[:, 0:8]
        page_ids = (l * n_phys + page_table[:, ::-1]).reshape(-1)
        res = _mlp(xp, mab_p, hc_p, mod_p, g2, gf, wo_all, wu_all, wd_all, l, rows_p // SB, T, final,
                   sample=(page_ids, q8, knt, vnt, lfn, ckt, cvt, clf))
        xp = res[0]
        if final:
            yp = res[1]
        hc8 = res[-1]
        hc = hc8[:, 0:ST].reshape(rows_s, W_C).astype(BF16)
        res = _mlp(xs, mab, hc, mod_s, g2, gf, wo_all, wu_all, wd_all, l, rows_s, rows_s, final)
        xs = res[0]
        if final:
            ys = res[1]
        C1, n1, mm1 = _unpack_state(st1, m1)
        lf_s = jnp.transpose(gt3[:, 0:H_C, :], (0, 2, 1))
        pool_s = jnp.concatenate([state_pool[l], u.reshape(SB, ST, W_B)], axis=1)[:, -POOL_HIST:]
        outs_s.append((kc.reshape(SB, ST, H_C, DH_C), vc.reshape(SB, ST, H_C, DH_C), lf_s, C1, n1, mm1, pool_s))

    to_bthd = lambda a: jnp.transpose(a.reshape(depth, B, H_C, DH_C, T), (0, 1, 4, 2, 3))
    sp_ = [jnp.stack(a) for a in zip(*outs_p)]
    ss_ = [jnp.stack(a) for a in zip(*outs_s)]
    return (yp.reshape(B, T, D), ys.reshape(SB, ST, D), to_bthd(kv_all[0]), to_bthd(kv_all[1]), *sp_, *ss_)
```

```python
import functools

import numpy as np
import jax
import jax.numpy as jnp
from jax import lax
from jax.experimental import pallas as pl
from jax.experimental.pallas import tpu as pltpu

F32 = jnp.float32
BF16 = jnp.bfloat16

D_MODEL = 1024
H_A = 4
W_A = 384
DV_A = 96
DK_A = 48
W_B = 256
CG_B = 64
POOL_HIST = 15
POOL_WINDOWS = (2, 4, 8, 16)
DH_C = 64
W_C = 384
H_C = 6
D_FF = 4096
EPS = 1e-6
CHUNK = 128
LOG2E = 1.4426950408889634
NEG = -1e30

LANES = 128
QK_PAD = 256
ST_COLS = W_A + LANES
MIX_AB = W_A + W_B
AUG_F0 = DH_C + 8
VMEM_LIMIT = 56 * 1024 * 1024
SEQ_GROUP = 2
ATTN_Q_TILE = 1024
ATTN_STRIP = 256
DEAD_LOG2 = 152.0
NORM_SLACK = 1.05
MLP_CHUNKS = 4
SAMPLE_Q = 4
SAMPLE_PAGES_PER_STEP = 16


def _dot(a, b):
    return jnp.dot(a, b, preferred_element_type=F32)


def _dot_nt(a, b):
    return lax.dot_general(a, b, (((1,), (1,)), ((), ())), preferred_element_type=F32)


def _dot_tn(a, b):
    return lax.dot_general(a, b, (((0,), (0,)), ((), ())), preferred_element_type=F32)


def _split3(x):
    a = x.astype(BF16)
    r = x - a.astype(F32)
    b = r.astype(BF16)
    c = (r - b.astype(F32)).astype(BF16)
    return a, b, c


def _dot3_rhs(m01, x):
    a, b, c = _split3(x)
    return _dot(m01, a) + _dot(m01, b) + _dot(m01, c)


def _dot3_lhs(x, m01):
    a, b, c = _split3(x)
    return _dot(a, m01) + _dot(b, m01) + _dot(c, m01)


def _log_sigmoid(x):
    return jnp.minimum(x, 0.0) - jnp.log1p(jnp.exp(-jnp.abs(x)))


def _sigmoid(x):
    return 1.0 / (1.0 + jnp.exp(-x))


def _iota(shape, dim):
    return lax.broadcasted_iota(jnp.int32, shape, dim)


def _expand_heads(src, width, per_head):
    lane = _iota((1, width), 1)
    out = jnp.zeros(src.shape[:-1] + (width,), F32)
    for h in range(H_A):
        out = jnp.where((lane >= h * per_head) & (lane < (h + 1) * per_head), src[:, h:h + 1], out)
    return out


def _mod_kernel(c_ref, w_ref, b_ref, o_ref):
    c = c_ref[...]
    s = (c * _sigmoid(c)).astype(BF16)
    o_ref[0] = _dot(s, w_ref[0].astype(BF16)) + b_ref[0]


def _modulation(c_all, w_ada, b_ada):
    depth, d, n6 = w_ada.shape
    rows = c_all.shape[0]
    tn = 1536
    return pl.pallas_call(
        _mod_kernel,
        grid=(depth, n6 // tn),
        in_specs=[
            pl.BlockSpec((rows, d), lambda l, j: (0, 0)),
            pl.BlockSpec((1, d, tn), lambda l, j: (l, 0, j)),
            pl.BlockSpec((1, 1, tn), lambda l, j: (l, 0, j)),
        ],
        out_specs=pl.BlockSpec((1, rows, tn), lambda l, j: (l, 0, j)),
        out_shape=jax.ShapeDtypeStruct((depth, rows, n6), F32),
        compiler_params=pltpu.CompilerParams(dimension_semantics=("arbitrary", "arbitrary"), vmem_limit_bytes=VMEM_LIMIT),
        name="modulation",
    )(c_all, w_ada, b_ada.reshape(depth, 1, n6))


NC_QA, NC_KA, NC_VA, NC_OA, NC_U, NC_G = 0, 256, 512, 896, 1280, 1536
N_TOKEN_MAJOR = 1664
TR_K, TR_V, TR_G, TR_Q = 0, 384, 768, 784
N_FEATURE_MAJOR = TR_Q + W_C


def _norm_mod(x, g, scale, shift):
    ms = jnp.mean(x * x, axis=-1, keepdims=True)
    return (x * lax.rsqrt(ms + EPS) * g) * (1.0 + scale) + shift


def _gate_rows(graw):
    r = _iota(graw.shape, 0)
    ls = _log_sigmoid(graw)
    return jnp.where((r < H_C) | (r >= 12), ls, jnp.where(r < 8, 0.0, graw))


def _gate_cols(graw):
    ln = _iota(graw.shape, 1)
    return jnp.where(ln < H_A, _log_sigmoid(graw), jnp.where(ln < 2 * H_A, graw, 0.0))


def _inproj_common(x_ref, mod_ref, g_ref, w_ref, bn_ref, sn_ref, bt_ref, st_ref, qk_ref, va_ref, oa_ref, u_ref, gc_ref):
    h = _norm_mod(x_ref[...], g_ref[...], mod_ref[1, 0], mod_ref[0, 0])
    hb = h.astype(BF16)

    def seg(off, width):
        return (_dot_nt(hb, w_ref[off:off + width, :]) + bn_ref[:, off:off + width]) * sn_ref[:, off:off + width]

    def tseg(off, rows):
        return (_dot_nt(w_ref[N_TOKEN_MAJOR + off:N_TOKEN_MAJOR + off + rows, :], hb) + bt_ref[off:off + rows, :]) * st_ref[off:off + rows, :]

    qk_ref[...] = seg(NC_QA, 2 * QK_PAD).astype(BF16)
    va_ref[...] = seg(NC_VA, W_A).astype(BF16)
    oa_ref[...] = seg(NC_OA, W_A)
    u_ref[...] = seg(NC_U, W_B)
    gc_ref[...] = _gate_cols(seg(NC_G, LANES))
    return seg, tseg


def _inproj_prompt_kernel(x_ref, mod_ref, g_ref, w_ref, bn_ref, sn_ref, bt_ref, st_ref, tri_ref, *rest, tiles_per_seq, n_prev):
    prev = rest[0:2 * min(n_prev, 1)]
    (qk_ref, va_ref, oa_ref, u_ref, gc_ref, kaug_ref, qta_ref, vta_ref, kt_ref, vt_ref, gt_ref, ft_ref, nrm_ref, carry_ref) = rest[len(prev):]
    i = pl.program_id(0)
    tm = x_ref.shape[0]
    _, tseg = _inproj_common(x_ref, mod_ref, g_ref, w_ref, bn_ref, sn_ref, bt_ref, st_ref, qk_ref, va_ref, oa_ref, u_ref, gc_ref)

    kt = tseg(TR_K, W_C)
    vt = tseg(TR_V, W_C)
    gt = _gate_rows(tseg(TR_G, 16))
    qt = tseg(TR_Q, W_C)
    if n_prev:
        kt_ref[0:n_prev, 0] = prev[0][:, 0]
        vt_ref[0:n_prev, 0] = prev[1][:, 0]
    kt_ref[n_prev, 0] = kt
    vt_ref[n_prev, 0] = vt
    gt_ref[0] = gt

    first = (i % tiles_per_seq) == 0
    carry = jnp.where(first, 0.0, carry_ref[:, 0:1])
    ft = _dot3_lhs(gt[0:8] * LOG2E, tri_ref[...]) + carry
    carry_ref[...] = jnp.broadcast_to(ft[:, tm - 1:tm], carry_ref.shape)
    ft_ref[0] = ft
    f1, f2, f3 = [p.astype(F32) for p in _split3(ft)]

    nr, nl = _iota((8, LANES), 0), _iota((8, LANES), 1)
    nrm = jnp.zeros((8, LANES), F32)
    for hh in range(H_C):
        for col, x in ((0, qt), (1, kt)):
            xh = x[hh * DH_C:(hh + 1) * DH_C]
            big = jnp.sqrt(jnp.max(jnp.sum(xh * xh, axis=0, keepdims=True), axis=1, keepdims=True))
            nrm = jnp.where((nr == hh) & (nl == col), big, nrm)
    nrm_ref[0, 0] = nrm

    r8 = _iota((8, tm), 0)
    ktail = jnp.concatenate([jnp.where(r8 < 3, 1.0, 0.0), -f1, -f2, -f3,
                             jnp.zeros((LANES - AUG_F0 - 24, tm), F32)], axis=0)
    r64 = _iota((LANES - DH_C, tm), 0)
    for hh in range(H_C):
        kaug_ref[hh] = jnp.concatenate([kt[hh * DH_C:(hh + 1) * DH_C], ktail], axis=0).T.astype(BF16)
        qa = jnp.where(r64 == 0, f1[hh:hh + 1], jnp.where(r64 == 1, f2[hh:hh + 1], jnp.where(r64 == 2, f3[hh:hh + 1], 0.0)))
        qa = jnp.where((r64 == 8 + hh) | (r64 == 16 + hh) | (r64 == 24 + hh), 1.0, qa)
        qta_ref[0, hh] = jnp.concatenate([qt[hh * DH_C:(hh + 1) * DH_C].astype(BF16), qa.astype(BF16)], axis=0)
        vta_ref[0, hh] = jnp.concatenate([vt[hh * DH_C:(hh + 1) * DH_C].astype(BF16),
                                          jnp.where(r64 == 0, 1.0, 0.0).astype(BF16)], axis=0)


def _inproj_sample_kernel(x_ref, mod_ref, g_ref, w_ref, bn_ref, sn_ref, bt_ref, st_ref,
                          qk_ref, va_ref, oa_ref, u_ref, gc_ref, qc_ref, kc_ref, vc_ref, kt_ref, vt_ref, gt_ref):
    _, tseg = _inproj_common(x_ref, mod_ref, g_ref, w_ref, bn_ref, sn_ref, bt_ref, st_ref, qk_ref, va_ref, oa_ref, u_ref, gc_ref)
    kt = tseg(TR_K, W_C)
    vt = tseg(TR_V, W_C)
    kt_ref[...] = kt
    vt_ref[...] = vt
    gt_ref[...] = _gate_rows(tseg(TR_G, 16))
    qc_ref[...] = tseg(TR_Q, W_C).T
    kc_ref[...] = kt.T
    vc_ref[...] = vt.T


def _inproj_weights(w_in, b_in):
    o = np.cumsum((0, H_A * DK_A, H_A * DK_A, W_A, H_A, H_A, W_A, W_B, W_C, W_C, W_C, H_C))
    wt_full = w_in.T
    d = w_in.shape[0]
    rows = lambda k: (wt_full[o[k]:o[k + 1]], b_in[o[k]:o[k + 1]])
    (wqa, bqa), (wka, bka), (wva, bva), (wia, bia), (wfa, bfa), (woa, boa), (wu, bu), (wqc, bqc), (wkc, bkc), (wvc, bvc), (wfc, bfc) = [rows(k) for k in range(11)]
    zw = lambda n: jnp.zeros((n, d), F32)
    zb = lambda n: jnp.zeros((n,), F32)
    ws = [wqa, zw(QK_PAD - 192), wka, zw(QK_PAD - 192), wva, woa, wu, wfa, wia, zw(LANES - 8)]
    bs = [bqa, zb(QK_PAD - 192), bka, zb(QK_PAD - 192), bva, boa, bu, bfa, bia, zb(LANES - 8)]
    sn =jnp.concatenate([jnp.ones((QK_PAD,), F32), jnp.full((QK_PAD,), DK_A ** -0.5, F32),
                          jnp.ones((N_TOKEN_MAJOR - 2 * QK_PAD,), F32)])[None, :]
    bn = jnp.concatenate(bs)[None, :]
    ws += [wkc, wvc, wfc, zw(2), wia, wfa, wqc]
    bt = jnp.concatenate([bkc, bvc, bfc, zb(2), bia, bfa, bqc])[:, None]
    st = jnp.concatenate([jnp.ones((TR_Q,), F32), jnp.full((W_C,), DH_C ** -0.5 * LOG2E, F32)])[:, None]
    return jnp.concatenate(ws, axis=0).astype(BF16), bn, sn, bt, st


def _inproj_prompt(x2, mod4, g1, weights, batch, seq, layer, kv_prev):
    w, bn, sn, bt, st = weights
    rows, d = x2.shape
    tm = min(512, seq)
    tps = seq // tm
    tri = jnp.asarray(np.triu(np.ones((tm, tm), np.float32)), BF16)
    full = lambda a: pl.BlockSpec(a.shape, lambda i: (0,) * a.ndim)
    rowblk = lambda wd: pl.BlockSpec((tm, wd), lambda i: (i, 0))
    tblk = lambda r: pl.BlockSpec((1, r, tm), lambda i: (i // tps, 0, i % tps))
    tblk4 = pl.BlockSpec((1, H_C, LANES, tm), lambda i: (i // tps, 0, 0, i % tps))
    kvblk = lambda n: pl.BlockSpec((n, 1, W_C, tm), lambda i: (0, i // tps, 0, i % tps))
    prev_in = [] if kv_prev is None else list(kv_prev)
    outs = pl.pallas_call(
        functools.partial(_inproj_prompt_kernel, tiles_per_seq=tps, n_prev=layer),
        grid=(rows // tm,),
        in_specs=[rowblk(d), pl.BlockSpec((6, 1, 1, d), lambda i: (0, i // tps, 0, 0)), full(g1),
                  full(w), full(bn), full(sn), full(bt), full(st), full(tri)] + [kvblk(layer)] * len(prev_in),
        out_specs=[rowblk(2 * QK_PAD), rowblk(W_A), rowblk(W_A), rowblk(W_B), rowblk(LANES),
                   pl.BlockSpec((H_C, tm, LANES), lambda i: (0, i, 0)), tblk4, tblk4, kvblk(layer + 1), kvblk(layer + 1), tblk(16),
                   tblk(8), pl.BlockSpec((1, 1, 8, LANES), lambda i: (i // tps, i % tps, 0, 0))],
        out_shape=[jax.ShapeDtypeStruct((rows, 2 * QK_PAD), BF16), jax.ShapeDtypeStruct((rows, W_A), BF16),
                   jax.ShapeDtypeStruct((rows, W_A), F32), jax.ShapeDtypeStruct((rows, W_B), F32),
                   jax.ShapeDtypeStruct((rows, LANES), F32),
                   jax.ShapeDtypeStruct((H_C, rows, LANES), BF16),
                   jax.ShapeDtypeStruct((batch, H_C, LANES, seq), BF16), jax.ShapeDtypeStruct((batch, H_C, LANES, seq), BF16),
                   jax.ShapeDtypeStruct((layer + 1, batch, W_C, seq), F32), jax.ShapeDtypeStruct((layer + 1, batch, W_C, seq), F32),
                   jax.ShapeDtypeStruct((batch, 16, seq), F32),
                   jax.ShapeDtypeStruct((batch, 8, seq), F32), jax.ShapeDtypeStruct((batch, tps, 8, LANES), F32)],
        scratch_shapes=[pltpu.VMEM((8, LANES), F32)],
        compiler_params=pltpu.CompilerParams(dimension_semantics=("arbitrary",), vmem_limit_bytes=VMEM_LIMIT),
        name="inproj_prompt",
    )(x2, mod4, g1, w, bn, sn, bt, st, tri, *prev_in)
    return outs


def _inproj_sample(x2, mod4, g1, weights):
    w, bn, sn, bt, st = weights
    rows, d = x2.shape
    full = lambda a: pl.BlockSpec(a.shape, lambda i: (0,) * a.ndim)
    o2 = lambda r, c: pl.BlockSpec((r, c), lambda i: (0, 0))
    sds = lambda r, c, t: jax.ShapeDtypeStruct((r, c), t)
    return pl.pallas_call(
        _inproj_sample_kernel,
        grid=(1,),
        in_specs=[full(x2), full(mod4), full(g1), full(w), full(bn), full(sn), full(bt), full(st)],
        out_specs=[o2(rows, 2 * QK_PAD), o2(rows, W_A), o2(rows, W_A), o2(rows, W_B), o2(rows, LANES),
                   o2(rows, W_C), o2(rows, W_C), o2(rows, W_C), o2(W_C, rows), o2(W_C, rows), o2(16, rows)],
        out_shape=[sds(rows, 2 * QK_PAD, BF16), sds(rows, W_A, BF16), sds(rows, W_A, F32), sds(rows, W_B, F32), sds(rows, LANES, F32),
                   sds(rows, W_C, F32), sds(rows, W_C, F32), sds(rows, W_C, F32), sds(W_C, rows, F32), sds(W_C, rows, F32), sds(16, rows, F32)],
        compiler_params=pltpu.CompilerParams(dimension_semantics=("arbitrary",), vmem_limit_bytes=VMEM_LIMIT),
        name="inproj_sample",
    )(x2, mod4, g1, w, bn, sn, bt, st)


def _state_mask():
    r = np.arange(QK_PAD)[:, None]
    c = np.arange(ST_COLS)[None, :]
    m = np.zeros((QK_PAD, ST_COLS), np.float32)
    for h in range(H_A):
        rows = (r >= h * DK_A) & (r < (h + 1) * DK_A)
        cols = ((c >= h * DV_A) & (c < (h + 1) * DV_A)) | (c == W_A + h)
        m[rows & cols] = 1.0
    return m


def _seq_kernel(*refs, pos0, group, interleave):
    chains = [_seq_one(g, *refs, pos0=pos0) for g in range(group)]
    if not interleave:
        for ch in chains:
            for _ in ch:
                pass
        return
    for lead in range(group - 1):
        for ch in chains[:group - 1 - lead]:
            next(ch)
    while chains:
        chains = [ch for ch in chains if next(ch, _DONE) is not _DONE]


_DONE = object()


def _seq_one(g, qk_ref, v_ref, o_ref, u_ref, gc_ref, gt_ref, st0_ref, m0_ref, hist0_ref, gh_ref, wp_ref, sp_ref, mask_ref,
             mix_ref, st_out_ref, m_out_ref, hist_out_ref, st_scr, m_scr, z_scr, *, pos0):
    c = pl.program_id(1)
    L = CHUNK

    @pl.when(c == 0)
    def _():
        st_scr[g] = st0_ref[g]
        m_scr[g] = m0_ref[g]
        z_scr[g, 0:16, :] = hist0_ref[g]

    q = qk_ref[g, :, 0:QK_PAD]
    k = qk_ref[g, :, QK_PAD:2 * QK_PAD]
    v = v_ref[g]
    gc = gc_ref[g]
    gt = gt_ref[g]
    st = st_scr[g]
    m_row = m_scr[g, 0:1, :]

    row = _iota((L, L), 0)
    colm = _iota((L, L), 1)
    tri_l = jnp.where(colm <= row, 1.0, 0.0).astype(BF16)
    tri_u = jnp.where(row <= colm, 1.0, 0.0).astype(BF16)
    lane128 = _iota((1, LANES), 1)
    lane_q = _iota((1, QK_PAD), 1)
    lane_v = _iota((1, W_A), 1)

    u = u_ref[g]
    z_scr[g, 16:16 + L, :] = u
    z = z_scr[g]
    s2 = z + pltpu.roll(z, 1, axis=0)
    s4 = s2 + pltpu.roll(s2, 2, axis=0)
    s8 = s4 + pltpu.roll(s4, 4, axis=0)
    s16 = s8 + pltpu.roll(s8, 8, axis=0)
    navail = (pos0 + c * L + 1 + _iota((L, 1), 0)).astype(F32)
    lane_u = _iota((1, W_B), 1)
    y = None
    for gi, (w, sw) in enumerate(zip(POOL_WINDOWS, (s2, s4, s8, s16))):
        yg = sw[16:16 + L, :] * (1.0 / jnp.minimum(float(w), navail))
        y = yg if y is None else jnp.where(lane_u >= gi * CG_B, yg, y)
    y = y - u
    hb = _dot(y.astype(BF16), wp_ref[...]) * sp_ref[...]
    z_scr[g, 0:16, :] = u[L - 16:L, :]

    bc_col = _dot3_rhs(tri_l, gc)
    bc_row = _dot3_lhs(gt[8:16], tri_u)
    qk_h = [_dot_nt(jnp.where((lane_q >= h * DK_A) & (lane_q < (h + 1) * DK_A), q, jnp.zeros_like(q)), k) for h in range(H_A)]
    qs = _dot(q, st.astype(BF16))
    yield
    inter = bc_col + m_row
    ia_col = pltpu.roll(gc, LANES - H_A, axis=1)

    causal = colm <= row
    mt_all = jnp.zeros((L, LANES), F32)
    s_list = []
    vblk = []
    for h in range(H_A):
        d = bc_col[:, h:h + 1] - bc_row[4 + h:5 + h, :] + gt[8 + h:9 + h, :]
        d = jnp.where(causal, d, NEG)
        mt = jnp.maximum(inter[:, h:h + 1], jnp.max(d, axis=-1, keepdims=True))
        s = qk_h[h] * jnp.exp(d - mt)
        s_list.append(s.astype(BF16))
        mt_all = jnp.where(lane128 == h, mt, mt_all)
        vm = jnp.where((lane_v >= h * DV_A) & (lane_v < (h + 1) * DV_A), v, jnp.zeros_like(v))
        vblk.append(jnp.concatenate([vm, jnp.broadcast_to(jnp.where(lane128 == h, 1.0, 0.0).astype(BF16), (L, LANES))], axis=1))
    s_cat = jnp.concatenate(s_list, axis=1)
    v_blk = jnp.concatenate(vblk, axis=0)
    pv = _dot(s_cat, v_blk)
    yield

    valid = lane128 < H_A
    m_new = mt_all[L - 1:L, :]
    g_col = jnp.where(valid, jnp.exp(bc_col[L - 1:L, :] - bc_col + ia_col - m_new), 0.0)
    a_last = jnp.where(valid, jnp.exp(inter[L - 1:L, :] - m_new), 0.0)
    gv = (v.astype(F32) * _expand_heads(g_col, W_A, DV_A)).astype(BF16)
    upd = _dot_tn(k, jnp.concatenate([gv, g_col.astype(BF16)], axis=1))
    yield
    a512 = jnp.concatenate([_expand_heads(a_last, W_A, DV_A), a_last], axis=1)
    st_new = a512 * st + mask_ref[...] * upd
    st_scr[g] = st_new
    m_rows = jnp.broadcast_to(jnp.where(valid, m_new, 0.0), (8, LANES))
    m_scr[g] = m_rows

    a_all = jnp.where(valid, jnp.exp(inter - mt_all), 0.0)
    num = _expand_heads(a_all, W_A, DV_A) * qs[:, 0:W_A] + pv[:, 0:W_A]
    den = a_all * qs[:, W_A:] + pv[:, W_A:]
    inv = 1.0 / jnp.maximum(jnp.abs(den), jnp.exp(-mt_all))
    hs = num * _expand_heads(jnp.where(valid, inv, 0.0), W_A, DV_A)

    sq = hs * hs
    ss_all = jnp.zeros((L, LANES), F32)
    for h in range(H_A):
        ssum = jnp.sum(jnp.where((lane_v >= h * DV_A) & (lane_v < (h + 1) * DV_A), sq, 0.0), axis=-1, keepdims=True)
        ss_all = jnp.where(lane128 == h, ssum, ss_all)
    r_all = lax.rsqrt(ss_all * (1.0 / DV_A) + EPS)
    ha = _sigmoid(o_ref[g]) * hs * _expand_heads(r_all, W_A, DV_A) * gh_ref[...]
    mix_ref[g] = jnp.concatenate([ha, hb], axis=1).astype(BF16)

    @pl.when(c == pl.num_programs(1) - 1)
    def _():
        st_out_ref[g] = st_new
        m_out_ref[g] = m_rows
        hist_out_ref[g] = u[L - 16:L, :]


def _sequence(qk, va, oa, u, gc, gt, st0, m0, hist0, gh, wp, sp, nb, nchunk, pos0):
    L = CHUNK
    G = SEQ_GROUP
    assert nb % G == 0
    mask = jnp.asarray(_state_mask())
    seq3 = lambda a: a.reshape(nb, nchunk * L, a.shape[-1])
    rowblk = lambda w: pl.BlockSpec((G, L, w), lambda b, c: (b, c, 0))
    perb = lambda a: pl.BlockSpec((G,) + a.shape[1:], lambda b, c: (b,) + (0,) * (a.ndim - 1))
    full = lambda a: pl.BlockSpec(a.shape, lambda b, c: (0,) * a.ndim)
    outs = pl.pallas_call(
        functools.partial(_seq_kernel, pos0=pos0, group=G, interleave=nchunk == 1),
        grid=(nb // G, nchunk),
        in_specs=[rowblk(2 * QK_PAD), rowblk(W_A), rowblk(W_A), rowblk(W_B), rowblk(LANES),
                  pl.BlockSpec((G, 16, L), lambda b, c: (b, 0, c)),
                  perb(st0), perb(m0), perb(hist0), full(gh), full(wp), full(sp), full(mask)],
        out_specs=[rowblk(MIX_AB), pl.BlockSpec((G, QK_PAD, ST_COLS), lambda b, c: (b, 0, 0)),
                   pl.BlockSpec((G, 8, LANES), lambda b, c: (b, 0, 0)), pl.BlockSpec((G, 16, W_B), lambda b, c: (b, 0, 0))],
        out_shape=[jax.ShapeDtypeStruct((nb, nchunk * L, MIX_AB), BF16), jax.ShapeDtypeStruct((nb, QK_PAD, ST_COLS), F32),
                   jax.ShapeDtypeStruct((nb, 8, LANES), F32), jax.ShapeDtypeStruct((nb, 16, W_B), F32)],
        scratch_shapes=[pltpu.VMEM((G, QK_PAD, ST_COLS), F32), pltpu.VMEM((G, 8, LANES), F32), pltpu.VMEM((G, 16 + L, W_B), F32)],
        compiler_params=pltpu.CompilerParams(dimension_semantics=("arbitrary", "arbitrary"), vmem_limit_bytes=VMEM_LIMIT),
        name="sequence",
    )(seq3(qk), seq3(va), seq3(oa), seq3(u), seq3(gc), gt, st0, m0, hist0, gh, wp, sp, mask)
    return (outs[0].reshape(nb * nchunk * L, MIX_AB),) + tuple(outs[1:])


def _pack_state(C, n, m):
    nb = C.shape[0]
    z = lambda r, c: jnp.zeros((nb, r, c), F32)
    blocks = []
    for h in range(H_A):
        blocks.append(jnp.concatenate([z(DK_A, h * DV_A), jnp.swapaxes(C[:, h], 1, 2), z(DK_A, W_A - (h + 1) * DV_A),
                                       z(DK_A, h), n[:, h][:, :, None], z(DK_A, LANES - h - 1)], axis=2))
    st = jnp.concatenate(blocks + [z(QK_PAD - H_A * DK_A, ST_COLS)], axis=1)
    mm = jnp.concatenate([jnp.broadcast_to(m[:, None, :], (nb, 8, H_A)), z(8, LANES - H_A)], axis=2)
    return st, mm


def _unpack_state(st, mm):
    C = jnp.stack([jnp.swapaxes(st[:, h * DK_A:(h + 1) * DK_A, h * DV_A:(h + 1) * DV_A], 1, 2) for h in range(H_A)], axis=1)
    n = jnp.stack([st[:, h * DK_A:(h + 1) * DK_A, W_A + h] for h in range(H_A)], axis=1)
    return C, n, mm[:, 0, 0:H_A]


def _attn_prompt_kernel(skip_ref, qt_ref, k_ref, vt_ref, o_ref, s_scr, m_scr, acc_scr, *, tq, tk, sw):
    qi = pl.program_id(2)
    m_scr[...] = jnp.full(m_scr.shape, NEG, F32)
    acc_scr[...] = jnp.zeros(acc_scr.shape, F32)
    units = [(hh, st) for hh in range(2) for st in range(tq // sw)]

    def qk_block(j, slot, skip=()):
        start = pl.multiple_of(j * tk, tk)
        for hh, st in units:
            if st not in skip:
                s_scr[slot, hh, :, st * sw:(st + 1) * sw] = _dot(k_ref[hh, pl.ds(start, tk), :], qt_ref[0, hh, :, st * sw:(st + 1) * sw])

    def softmax_pv_block(j, slot, masked=(), skip=()):
        start = pl.multiple_of(j * tk, tk)
        for hh, st in units:
            if st in skip:
                continue
            strip = slice(st * sw, (st + 1) * sw)
            s = s_scr[slot, hh, :, strip]
            if st in masked:
                keyg = j * tk + _iota((tk, sw), 0)
                qryg = qi * tq + st * sw + _iota((tk, sw), 1)
                s = jnp.where(keyg <= qryg, s, NEG)
            m = m_scr[hh, 0:1, strip]
            m_new = jnp.maximum(m, jnp.max(s, axis=0, keepdims=True))
            p = jnp.exp2(s - m_new)
            pv = _dot(vt_ref[0, hh, :, pl.ds(start, tk)], p.astype(BF16))
            acc_scr[hh, :, strip] = jnp.exp2(m - m_new) * acc_scr[hh, :, strip] + pv
            m_scr[hh, 0:1, strip] = m_new

    n_full = (qi * tq) // tk
    first_pair = jnp.minimum(skip_ref[(pl.program_id(0) * pl.num_programs(1) + pl.program_id(1)) * pl.num_programs(2) + qi],
                             n_full // 2)
    qk_block(2 * first_pair, 0)

    def body(i, carry):
        j = 2 * i
        qk_block(j + 1, 1)
        softmax_pv_block(j, 0)
        qk_block(j + 2, 0)
        softmax_pv_block(j + 1, 1)
        return carry

    lax.fori_loop(first_pair, n_full // 2, body, 0)
    n_diag = tq // tk
    for d in range(n_diag):
        if d + 1 < n_diag:
            qk_block(n_full + d + 1, (d + 1) % 2, skip=tuple(range(d + 1)))
        softmax_pv_block(n_full + d, d % 2, masked=(d,), skip=tuple(range(d)))
    outs = []
    for hh in range(2):
        acc = acc_scr[hh]
        outs.append((acc * (1.0 / acc[DH_C:DH_C + 1, :])).T)
    lane = _iota((1, LANES), 1)
    o_ref[...] = jnp.where(lane < DH_C, outs[0], pltpu.roll(outs[1], DH_C, axis=1)).astype(BF16)


def _dead_block_pairs(ft, qk_norm, seq, tq, tk):
    nb = ft.shape[0]
    nq, nk = seq // tq, seq // tk
    tiles = qk_norm.shape[1]
    qmax = jnp.max(qk_norm[:, :, 0:H_C, 0].reshape(nb, nq, tiles // nq, H_C), axis=2)
    kmax = jnp.max(qk_norm[:, :, 0:H_C, 1], axis=1)
    f_start = jnp.transpose(ft[:, 0:H_C, 0::tq], (0, 2, 1))
    f_end = jnp.transpose(ft[:, 0:H_C, tk - 1::tk], (0, 2, 1))
    bound = (NORM_SLACK * 2.0) * (qmax * kmax[:, None, :])[:, :, None, :] + f_start[:, :, None, :] - f_end[:, None, :, :]
    dead = bound < -DEAD_LOG2
    dead = jnp.all(dead.reshape(nb, nq, nk // 2, 2, H_C // 2, 2), axis=(3, 5))
    pair = jnp.arange(nk // 2, dtype=jnp.int32)[None, None, :, None]
    n_dead = jnp.min(jnp.where(dead, nk // 2, pair), axis=2)
    return jnp.transpose(n_dead, (0, 2, 1)).reshape(-1).astype(jnp.int32)


def _attn_prompt(qta, k_aug, vta, ft, qk_norm, batch, seq):
    rows = batch * seq
    tk = sw = ATTN_STRIP
    tq = min(ATTN_Q_TILE, seq)
    assert seq % tq == 0 and (tq // tk) % 2 == 0
    nq = seq // tq
    grid_spec = pltpu.PrefetchScalarGridSpec(
        num_scalar_prefetch=1, grid=(batch, H_C // 2, nq),
        in_specs=[pl.BlockSpec((1, 2, LANES, tq), lambda b, p, i, sk: (b, p, 0, i)),
                  pl.BlockSpec((2, seq, LANES), lambda b, p, i, sk: (p, b, 0)),
                  pl.BlockSpec((1, 2, LANES, seq), lambda b, p, i, sk: (b, p, 0, 0))],
        out_specs=pl.BlockSpec((tq, LANES), lambda b, p, i, sk: (b * nq + i, p)),
        scratch_shapes=[pltpu.VMEM((2, 2, tk, tq), F32), pltpu.VMEM((2, 8, tq), F32), pltpu.VMEM((2, LANES, tq), F32)])
    return pl.pallas_call(
        functools.partial(_attn_prompt_kernel, tq=tq, tk=tk, sw=sw),
        grid_spec=grid_spec,
        out_shape=jax.ShapeDtypeStruct((rows, W_C), BF16),
        compiler_params=pltpu.CompilerParams(dimension_semantics=("arbitrary", "arbitrary", "arbitrary"), vmem_limit_bytes=VMEM_LIMIT),
        name="attn_prompt",
    )(_dead_block_pairs(ft, qk_norm, seq, tq, tk), qta, k_aug, vta)


def _when(cond):
    if isinstance(cond, bool):
        return (lambda f: f()) if cond else (lambda f: None)
    return pl.when(cond)


def _sample_attn_step(t, n_total, first, last, P, pt_ref, q_ref, knt_ref, vnt_ref, lfn_ref, ck_hbm, cv_hbm, clf_hbm, o_ref,
                      kbuf, vbuf, lbuf, sems, m_scr, acc_scr, carry_scr, cq_scr):
    slot = t % 2

    def page_copies(step, sl):
        out = []
        for i in range(P):
            pg = pt_ref[step * P + i]
            out += [(pltpu.make_async_copy(ck_hbm.at[pg], kbuf.at[sl, i], sems.at[sl, 0]), 0),
                    (pltpu.make_async_copy(cv_hbm.at[pg], vbuf.at[sl, i], sems.at[sl, 1]), 0),
                    (pltpu.make_async_copy(clf_hbm.at[pg], lbuf.at[sl, i], sems.at[sl, 2]), 1)]
        return out

    @_when(first and (t == 0))
    def _():
        for cp, pr in page_copies(0, 0):
            cp.start(priority=pr)

    @pl.when(t + 1 < n_total)
    def _():
        for cp, pr in page_copies(t + 1, 1 - slot):
            cp.start(priority=pr)

    for cp, _ in page_copies(t, slot):
        cp.wait()

    R = 8 * SAMPLE_Q
    head_of_lane = lax.shift_right_logical(_iota((R, W_C), 1), 6)
    headmask = head_of_lane == (_iota((R, W_C), 0) & 7)
    lane = _iota((R, LANES), 1)
    rq = lax.shift_right_logical(_iota((R, LANES), 0), 3)

    def tile_q(x8):
        return jnp.concatenate([x8] * SAMPLE_Q, axis=0)

    q8 = q_ref[0]
    qrep = jnp.concatenate([jnp.broadcast_to(q8[qq:qq + 1], (8, W_C)) for qq in range(SAMPLE_Q)], axis=0)
    qbd = jnp.where(headmask, qrep, 0.0).astype(BF16)

    @_when(first)
    def _():
        a = tile_q(lfn_ref[0] * LOG2E)
        cq = jnp.sum(jnp.where(lane <= rq, a, 0.0), axis=-1, keepdims=True)
        incl = jnp.where(_iota((LANES, LANES), 0) <= _iota((LANES, LANES), 1), 1.0, 0.0).astype(BF16)
        crow = _dot3_lhs(a, incl)
        s = _dot(qbd, knt_ref[0].astype(BF16)) + cq - crow
        s = jnp.where((lane <= rq) & (lane < SAMPLE_Q), s, NEG)
        m = jnp.max(s, axis=-1, keepdims=True)
        p = jnp.exp2(s - m)
        l = jnp.sum(p, axis=-1, keepdims=True)
        acc_scr[:, 0:W_C] = _dot_nt(p.astype(BF16), vnt_ref[0].astype(BF16))
        acc_scr[:, W_C:] = jnp.broadcast_to(l, (R, LANES))
        m_scr[...] = jnp.broadcast_to(m, (R, LANES))
        cq_scr[...] = jnp.broadcast_to(cq, (R, LANES))
        carry_scr[...] = jnp.zeros_like(carry_scr)

    cq = cq_scr[:, 0:1]
    strict = jnp.where(_iota((LANES, LANES), 0) > _iota((LANES, LANES), 1), 1.0, 0.0).astype(BF16)
    lf_all = lbuf[slot].reshape(8 * P, LANES) * LOG2E
    suf = _dot3_lhs(lf_all, strict)
    carry = carry_scr[:, 0:1]
    bias = [None] * P
    for i in range(P):
        bias[i] = tile_q(suf[8 * i:8 * i + 8] + carry)
        carry = carry + suf[8 * i:8 * i + 8, 0:1] + lf_all[8 * i:8 * i + 8, 0:1]
    carry_scr[...] = jnp.broadcast_to(carry, carry_scr.shape)

    kcat = jnp.concatenate([kbuf[slot, i].astype(BF16) for i in range(P)], axis=1)
    vcat = jnp.concatenate([vbuf[slot, i].astype(BF16) for i in range(P)], axis=1)
    s = _dot(qbd, kcat) + jnp.concatenate(bias, axis=1) + cq
    yield
    m_old = m_scr[:, 0:1]
    m_new = jnp.maximum(m_old, jnp.max(s, axis=-1, keepdims=True))
    alpha = jnp.exp2(m_old - m_new)
    p = jnp.exp2(s - m_new)
    l = alpha * acc_scr[:, W_C:W_C + 1] + jnp.sum(p, axis=-1, keepdims=True)
    acc = alpha * acc_scr[:, 0:W_C] + _dot_nt(p.astype(BF16), vcat)
    acc_scr[:, 0:W_C] = acc
    acc_scr[:, W_C:] = jnp.broadcast_to(l, (R, LANES))
    m_scr[...] = jnp.broadcast_to(m_new, (R, LANES))

    @_when(last)
    def _():
        o = jnp.where(headmask, acc * (1.0 / l), 0.0)
        rows = [jnp.sum(o[8 * qq:8 * qq + 8], axis=0, keepdims=True) for qq in range(SAMPLE_Q)]
        o_ref[0] = jnp.concatenate(rows + [jnp.zeros((8 - SAMPLE_Q, W_C), F32)], axis=0)


def _sample_attn_scratch(P):
    R = 8 * SAMPLE_Q
    return [pltpu.VMEM((2, P, W_C, LANES), F32), pltpu.VMEM((2, P, W_C, LANES), F32), pltpu.VMEM((2, P, 8, LANES), F32),
            pltpu.SemaphoreType.DMA((2, 3)),
            pltpu.VMEM((R, LANES), F32), pltpu.VMEM((R, W_C + LANES), F32), pltpu.VMEM((8, LANES), F32), pltpu.VMEM((R, LANES), F32)]


def _mlp_kernel(*refs, final, n_groups, pages_per_group):
    fused = n_groups > 0
    if fused:
        pt_ref, refs = refs[0], refs[1:]
    x_ref, mab_ref, hc_ref, mod_ref, g2_ref, gf_ref, wo_ref, wu_ref, wd_ref = refs[0:9]
    rest = refs[9:]
    if fused:
        sample_in, rest = rest[0:7], rest[7:]
    n_out = (2 if final else 1) + (1 if fused else 0)
    out_refs, scratch = rest[0:n_out], rest[n_out:]

    x = x_ref[...]
    mix = _dot(mab_ref[...], wo_ref[0, 0:MIX_AB, :]) + _dot(hc_ref[...], wo_ref[0, MIX_AB:, :])
    x1 = x + mod_ref[2, 0] * mix
    h2 = _norm_mod(x1, g2_ref[...], mod_ref[4, 0], mod_ref[3, 0]).astype(BF16)
    ff = jnp.zeros_like(x)
    n_chunks = n_groups if fused else MLP_CHUNKS
    fc = D_FF // n_chunks
    for f in range(n_chunks):
        a = jnp.maximum(_dot(h2, wu_ref[0, :, f * fc:(f + 1) * fc]), 0.0)
        if fused:
            step = _sample_attn_step(pl.program_id(0) * n_groups + f, pl.num_programs(0) * n_groups, f == 0, f == n_groups - 1,
                                     pages_per_group, pt_ref, *sample_in, out_refs[n_out - 1], *scratch)
            next(step)
        ff = ff + _dot((a * a).astype(BF16), wd_ref[0, f * fc:(f + 1) * fc, :])
        if fused:
            next(step, None)
    x2 = x1 + mod_ref[5, 0] * ff
    out_refs[0][...] = x2
    if final:
        ms = jnp.mean(x2 * x2, axis=-1, keepdims=True)
        out_refs[1][...] = x2 * lax.rsqrt(ms + EPS) * gf_ref[...]


def _mlp(x2, mab, hc, mod4, g2, gf, wo, wu, wd, layer, tm, rows_per_mod, final, sample=None):
    rows, d = x2.shape
    s = mod4.shape[2]
    fused = sample is not None
    im = (lambda f: (lambda i, pt: f(i))) if fused else (lambda f: f)
    const = lambda a: pl.BlockSpec(a.shape, im(lambda i: (0,) * a.ndim), pipeline_mode=pl.Buffered(1))
    layerw = lambda a: pl.BlockSpec((1,) + a.shape[1:], im(lambda i: (layer, 0, 0)), pipeline_mode=pl.Buffered(1))
    rowblk = lambda w: pl.BlockSpec((tm, w), im(lambda i: (i, 0)))
    n_out = 2 if final else 1
    in_specs = [rowblk(d), rowblk(MIX_AB), rowblk(W_C),
                pl.BlockSpec((6, 1, s, d), im(lambda i: (0, (i * tm) // rows_per_mod, 0, 0))),
                const(g2), const(gf), layerw(wo), layerw(wu), layerw(wd)]
    out_specs = [rowblk(d)] * n_out
    out_shape = [jax.ShapeDtypeStruct((rows, d), F32)] * n_out
    args = [x2, mab, hc, mod4, g2, gf, wo, wu, wd]
    scratch, n_groups, P, n_prefetch = [], 0, 0, 0
    if fused:
        page_ids, q8, knt, vnt, lfn = sample[0:5]
        nb, n_pages = q8.shape[0], page_ids.shape[0] // q8.shape[0]
        assert nb == rows // tm
        P = min(SAMPLE_PAGES_PER_STEP, n_pages)
        n_groups = n_pages // P
        assert D_FF % (n_groups * LANES) == 0
        perb = lambda a: pl.BlockSpec((1,) + a.shape[1:], lambda i, pt: (i, 0, 0))
        in_specs += [perb(q8), perb(knt), perb(vnt), perb(lfn)] + [pl.BlockSpec(memory_space=pl.ANY)] * 3
        out_specs += [pl.BlockSpec((1, 8, W_C), lambda i, pt: (i, 0, 0))]
        out_shape += [jax.ShapeDtypeStruct((nb, 8, W_C), F32)]
        args = [page_ids] + args + list(sample[1:])
        scratch, n_prefetch = _sample_attn_scratch(P), 1
    grid_spec = pltpu.PrefetchScalarGridSpec(num_scalar_prefetch=n_prefetch, grid=(rows // tm,), in_specs=in_specs,
                                             out_specs=out_specs, scratch_shapes=scratch)
    return pl.pallas_call(
        functools.partial(_mlp_kernel, final=final, n_groups=n_groups, pages_per_group=P),
        grid_spec=grid_spec,
        out_shape=out_shape,
        compiler_params=pltpu.CompilerParams(dimension_semantics=("arbitrary",), vmem_limit_bytes=VMEM_LIMIT),
        name="mlp_attn_sample" if fused else "mlp",
    )(*args)


def kernel(x_prompt, x_sample, c_prompt, c_sample, cache_k, cache_v, cache_logf, page_table, state_C, state_n, state_m, state_pool, w_ada, b_ada, g_norm1, g_norm2, w_in, b_in, g_head_a, w_pool, s_pool, w_out, w_up, w_down, g_final):
    depth = w_ada.shape[0]
    B, T, D = x_prompt.shape
    SB, ST, _ = x_sample.shape
    n_phys, page = cache_k.shape[1], cache_k.shape[2]
    n_pages = page_table.shape[1]
    rows_p, rows_s = B * T, SB * ST
    pos0_s = n_pages * page
    assert T % CHUNK == 0 and ST == SAMPLE_Q and page == LANES

    nc = B + SB
    c_all = jnp.concatenate([c_prompt, c_sample, jnp.zeros((-nc % 8, D), F32)], axis=0)
    mod = _modulation(c_all, w_ada, b_ada)

    ckt = jnp.transpose(cache_k, (0, 1, 3, 4, 2)).reshape(depth * n_phys, W_C, page)
    cvt = jnp.transpose(cache_v, (0, 1, 3, 4, 2)).reshape(depth * n_phys, W_C, page)
    clf = jnp.pad(jnp.transpose(cache_logf, (0, 1, 3, 2)), ((0, 0), (0, 0), (0, 8 - H_C), (0, 0))).reshape(depth * n_phys, 8, page)
    wo_all, wu_all, wd_all = w_out.astype(BF16), w_up.astype(BF16), w_down.astype(BF16)

    xp = x_prompt.reshape(rows_p, D)
    xs = x_sample.reshape(rows_s, D)
    zeros_state = (jnp.zeros((B, QK_PAD, ST_COLS), F32), jnp.zeros((B, 8, LANES), F32))
    outs_p, outs_s = [], []
    yp = ys = None
    kv_all = None
    for l in range(depth):
        final = l == depth - 1
        g1, g2, gf = g_norm1[l][None, :], g_norm2[l][None, :], g_final[None, :]
        gh = g_head_a[l].reshape(1, W_A)
        wp = jax.scipy.linalg.block_diag(*[w_pool[l, g] for g in range(len(POOL_WINDOWS))]).astype(BF16)
        sp = s_pool[l][None, :]
        weights = _inproj_weights(w_in[l], b_in[l])
        modl = mod[l].reshape(-1, 6, D)
        mod_p = jnp.transpose(modl[0:B], (1, 0, 2))[:, :, None, :]
        mod_s = jnp.transpose(jnp.repeat(modl[B:B + SB], ST, axis=0), (1, 0, 2))[:, None]

        (qk, va, oa, u, gc, k_aug, qta, vta, kt_all, vt_all, gt, ft, qk_norm) = _inproj_prompt(xp, mod_p, g1, weights, B, T, l, kv_all)
        kv_all = (kt_all, vt_all)
        mab, st1, m1, hist1 = _sequence(qk, va, oa, u, gc, gt, zeros_state[0], zeros_state[1], jnp.zeros((B, 16, W_B), F32),
                                        gh, wp, sp, B, T // CHUNK, 0)
        hc = _attn_prompt(qta, k_aug, vta, ft, qk_norm, B, T)
        mab_p, hc_p = mab, hc
        C1, n1, mm1 = _unpack_state(st1, m1)
        outs_p.append((jnp.transpose(gt[:, 0:H_C, :], (0, 2, 1)), C1, n1, mm1, hist1[:, 1:]))

        (qk, va, oa, u, gc, qc, kc, vc, kt, vt, gt) = _inproj_sample(xs, mod_s, g1, weights)

        def pad_rows(a, fill=0.0):
            a3 = a.reshape(SB, ST, a.shape[-1])
            return jnp.pad(a3, ((0, 0), (0, CHUNK - ST), (0, 0)), constant_values=fill).reshape(SB * CHUNK, a.shape[-1])

        lane = jnp.arange(LANES)[None, :]
        gc_pad = jnp.where((lane >= H_A) & (lane < 2 * H_A), pad_rows(gc, NEG), pad_rows(gc))
        gt3 = jnp.transpose(gt.reshape(16, SB, ST), (1, 0, 2))
        rr = jnp.arange(16)[None, :, None]
        gt_pad = jnp.where((rr >= 8) & (rr < 12), jnp.pad(gt3, ((0, 0), (0, 0), (0, CHUNK - ST)), constant_values=NEG),
                           jnp.pad(gt3, ((0, 0), (0, 0), (0, CHUNK - ST))))
        st0, m0 = _pack_state(state_C[l], state_n[l], state_m[l])
        hist0 = jnp.pad(state_pool[l], ((0, 0), (1, 0), (0, 0)))
        mab, st1, m1, _ = _sequence(pad_rows(qk), pad_rows(va), pad_rows(oa), pad_rows(u), gc_pad, gt_pad, st0, m0, hist0,
                                    gh, wp, sp, SB, 1, pos0_s)
        mab = mab.reshape(SB, CHUNK, MIX_AB)[:, 0:ST].reshape(rows_s, MIX_AB)
        q8 = jnp.pad(qc.reshape(SB, ST, W_C), ((0, 0), (0, 8 - ST), (0, 0)))
        tpad = lambda a: jnp.pad(jnp.transpose(a.reshape(a.shape[0], SB, ST), (1, 0, 2)), ((0, 0), (0, 0), (0, LANES - ST)))
        knt, vnt = tpad(kt), tpad(vt)
        lfn = jnp.where(rr < H_C, jnp.pad(gt3, ((0, 0), (0, 0), (0, LANES - ST))), 0.0)[:, 0:8]
        page_ids = (l * n_phys + page_table[:, ::-1]).reshape(-1)
        res = _mlp(xp, mab_p, hc_p, mod_p, g2, gf, wo_all, wu_all, wd_all, l, rows_p // SB, T, final,
                   sample=(page_ids, q8, knt, vnt, lfn, ckt, cvt, clf))
        xp = res[0]
        if final:
            yp = res[1]
        hc8 = res[-1]
        hc = hc8[:, 0:ST].reshape(rows_s, W_C).astype(BF16)
        res = _mlp(xs, mab, hc, mod_s, g2, gf, wo_all, wu_all, wd_all, l, rows_s, rows_s, final)
        xs = res[0]
        if final:
            ys = res[1]
        C1, n1, mm1 = _unpack_state(st1, m1)
        lf_s = jnp.transpose(gt3[:, 0:H_C, :], (0, 2, 1))
        pool_s = jnp.concatenate([state_pool[l], u.reshape(SB, ST, W_B)], axis=1)[:, -POOL_HIST:]
        outs_s.append((kc.reshape(SB, ST, H_C, DH_C), vc.reshape(SB, ST, H_C, DH_C), lf_s, C1, n1, mm1, pool_s))

    to_bthd = lambda a: jnp.transpose(a.reshape(depth, B, H_C, DH_C, T), (0, 1, 4, 2, 3))
    sp_ = [jnp.stack(a) for a in zip(*outs_p)]
    ss_ = [jnp.stack(a) for a in zip(*outs_s)]
    return (yp.reshape(B, T, D), ys.reshape(SB, ST, D), to_bthd(kv_all[0]), to_bthd(kv_all[1]), *sp_, *ss_)
```
